```python
import jax, jax.numpy as jnp
from jax import lax
import numpy as np

D_MODEL = 2048
BATCH = 4
SEQ = 8192
DEPTH = 1

HEAD_DIM = 128
SB_HEADS = D_MODEL // (2 * HEAD_DIM)
RET_HEADS = D_MODEL // (2 * HEAD_DIM)
SB_WIDTH = SB_HEADS * HEAD_DIM
RET_WIDTH = RET_HEADS * HEAD_DIM
MIX_WIDTH = SB_WIDTH + RET_WIDTH
IN_WIDTH = 4 * SB_WIDTH + 4 * RET_WIDTH
Q_BLOCK = 128
RET_CHUNK = 128
ROPE_BASE = 10000.0
EPS = 1e-6

kernel_name = "hymba_stickbreak_retention_adaln"


def rms_norm(x, gain):
    xf = x.astype(jnp.float32)
    y = xf * lax.rsqrt(jnp.mean(xf * xf, axis=-1, keepdims=True) + EPS)
    return (y * gain.astype(jnp.float32)).astype(x.dtype)


def split_heads(t, n_heads):
    b, s, _ = t.shape
    return t.reshape(b, s, n_heads, HEAD_DIM).transpose(0, 2, 1, 3)


def merge_heads(t):
    b, h, s, d = t.shape
    return t.transpose(0, 2, 1, 3).reshape(b, s, h * d)


def rotary(t):
    s, d = t.shape[2], t.shape[3]
    half = d // 2
    inv_freq = ROPE_BASE ** (-jnp.arange(half, dtype=jnp.float32) / half)
    ang = jnp.arange(s, dtype=jnp.float32)[:, None] * inv_freq[None, :]
    cos, sin = jnp.cos(ang), jnp.sin(ang)
    tf = t.astype(jnp.float32)
    t1, t2 = tf[..., :half], tf[..., half:]
    return jnp.concatenate([t1 * cos - t2 * sin, t1 * sin + t2 * cos], axis=-1)


def stick_breaking_attention(q, k, v):
    b, h, s, d = q.shape
    nb = s // Q_BLOCK
    qb = q.astype(jnp.float32).reshape(b, h, nb, Q_BLOCK, d).transpose(2, 0, 1, 3, 4)
    kf = k.astype(jnp.float32)
    vf = v.astype(jnp.float32)
    key_pos = jnp.arange(s)
    inv_sqrt_d = float(1.0 / np.sqrt(d))

    def one_block(args):
        q_blk, blk = args
        q_pos = blk * Q_BLOCK + jnp.arange(Q_BLOCK)
        z = jnp.einsum('bhqd,bhsd->bhqs', q_blk, kf) * inv_sqrt_d
        past = key_pos[None, :] < q_pos[:, None]
        log_not = jnp.where(past, jax.nn.log_sigmoid(-z), 0.0)
        later = lax.cumsum(log_not, axis=3, reverse=True) - log_not
        w = jnp.where(past, jnp.exp(jax.nn.log_sigmoid(z) + later), 0.0)
        return jnp.einsum('bhqs,bhsd->bhqd', w, vf)

    out = lax.map(one_block, (qb, jnp.arange(nb)))
    return out.transpose(1, 2, 0, 3, 4).reshape(b, h, s, d)


def chunkwise_retention(q, k, v):
    b, h, s, d = q.shape
    c = RET_CHUNK
    n = s // c
    log_gamma = jnp.log1p(-jnp.exp2(-5.0 - jnp.arange(h, dtype=jnp.float32)))
    idx = jnp.arange(c, dtype=jnp.float32)
    rel = idx[:, None] - idx[None, :]
    decay_in = jnp.where(rel[None] >= 0, jnp.exp(jnp.maximum(rel, 0.0)[None] * log_gamma[:, None, None]), 0.0)
    q_decay = jnp.exp((idx[None, :] + 1.0) * log_gamma[:, None])[..., None]
    k_decay = jnp.exp((c - 1.0 - idx[None, :]) * log_gamma[:, None])[..., None]
    chunk_decay = jnp.exp(c * log_gamma)[:, None, None]

    def chunks(t):
        return t.reshape(b, h, n, c, d).transpose(2, 0, 1, 3, 4)

    def step(state, inp):
        qc, kc, vc = inp
        inner = jnp.einsum('bhij,bhjd->bhid', jnp.einsum('bhid,bhjd->bhij', qc, kc) * decay_in, vc)
        cross = jnp.einsum('bhid,bhde->bhie', qc * q_decay, state)
        new_state = state * chunk_decay + jnp.einsum('bhjd,bhje->bhde', kc * k_decay, vc)
        return new_state, inner + cross

    state0 = jnp.zeros((b, h, d, d), jnp.float32)
    _, out = lax.scan(step, state0, (chunks(q), chunks(k), chunks(v)))
    return out.transpose(1, 2, 0, 3, 4).reshape(b, h, s, d)


def head_group_norm(o, gain):
    mu = jnp.mean(o, axis=-1, keepdims=True)
    var = jnp.mean(jnp.square(o - mu), axis=-1, keepdims=True)
    return merge_heads((o - mu) * lax.rsqrt(var + EPS)) * gain.astype(jnp.float32)


def setup_inputs(seed: int = 0) -> dict:
    key = jax.random.key(seed)
    ks = jax.random.split(key, 10)
    f32 = jnp.float32
    x = jax.random.normal(ks[0], (BATCH, SEQ, D_MODEL), f32)
    c = jax.random.normal(ks[1], (BATCH, D_MODEL), f32)
    w_ada = jax.random.normal(ks[2], (DEPTH, D_MODEL, 3 * D_MODEL), f32) * D_MODEL ** -0.5
    b_ada = 0.02 * jax.random.normal(ks[3], (DEPTH, 3 * D_MODEL), f32)
    norm_gain = 1.0 + 0.02 * jax.random.normal(ks[4], (DEPTH, D_MODEL), f32)
    w_in = jax.random.normal(ks[5], (DEPTH, D_MODEL, IN_WIDTH), f32) * D_MODEL ** -0.5
    sb_q_gain = 1.0 + 0.02 * jax.random.normal(ks[6], (DEPTH, HEAD_DIM), f32)
    sb_k_gain = 1.0 + 0.02 * jax.random.normal(ks[7], (DEPTH, HEAD_DIM), f32)
    ret_norm_gain = 1.0 + 0.02 * jax.random.normal(ks[8], (DEPTH, RET_WIDTH), f32)
    w_out = jax.random.normal(ks[9], (DEPTH, MIX_WIDTH, D_MODEL), f32) * MIX_WIDTH ** -0.5
    return {"x": x, "c": c, "w_ada": w_ada, "b_ada": b_ada, "norm_gain": norm_gain,
            "w_in": w_in, "sb_q_gain": sb_q_gain, "sb_k_gain": sb_k_gain,
            "ret_norm_gain": ret_norm_gain, "w_out": w_out}


def reference(x, c, w_ada, b_ada, norm_gain, w_in, sb_q_gain, sb_k_gain, ret_norm_gain, w_out):
    split_at = [SB_WIDTH * i for i in range(1, 5)] + [4 * SB_WIDTH + RET_WIDTH * i for i in range(1, 4)]
    for layer in range(DEPTH):
        mod = jax.nn.silu(c) @ w_ada[layer] + b_ada[layer]
        shift, scale, gate = jnp.split(mod, 3, axis=-1)
        h = rms_norm(x, norm_gain[layer]) * (1.0 + scale[:, None, :]) + shift[:, None, :]

        proj = h @ w_in[layer]
        q_sb, k_sb, v_sb, z_sb, q_r, k_r, v_r, z_r = jnp.split(proj, split_at, axis=-1)

        qa = rms_norm(split_heads(q_sb, SB_HEADS), sb_q_gain[layer])
        ka = rms_norm(split_heads(k_sb, SB_HEADS), sb_k_gain[layer])
        oa = merge_heads(stick_breaking_attention(qa, ka, split_heads(v_sb, SB_HEADS)))
        oa = oa * jax.nn.silu(z_sb.astype(jnp.float32))

        qr = rotary(split_heads(q_r, RET_HEADS))
        kr = rotary(split_heads(k_r, RET_HEADS)) * (HEAD_DIM ** -0.5)
        vr = split_heads(v_r, RET_HEADS).astype(jnp.float32)
        ob = head_group_norm(chunkwise_retention(qr, kr, vr), ret_norm_gain[layer])
        ob = ob * jax.nn.silu(z_r.astype(jnp.float32))

        y = jnp.concatenate([oa, ob], axis=-1).astype(x.dtype) @ w_out[layer]
        x = x + (gate[:, None, :] * y).astype(x.dtype)
    return x
```

```python
import functools

import jax
import jax.numpy as jnp
import numpy as np
from jax import lax
from jax.experimental import pallas as pl
from jax.experimental.pallas import tpu as pltpu

HEAD_DIM = 128
ROPE_BASE = 10000.0
EPS = 1e-6
N_GROUPS = 8
F32 = jnp.float32
BF16 = jnp.bfloat16

PROJ_ROWS = 1024
SB_Q_ROWS = 512
SB_K_ROWS = 256
RET_CHUNK = 256
OUT_ROWS = 512
VMEM_LIMIT = 56 * 1024 * 1024


def _silu(v):
    return v / (1.0 + jnp.exp(-v))


def _adaln_kernel(c_ref, w_ref, b_ref, o_ref):
    s = _silu(c_ref[...])
    o_ref[...] = jnp.dot(s, w_ref[...], preferred_element_type=F32,
                         precision=lax.Precision.HIGHEST) + b_ref[...]


def _adaln(c_pad, w, b):
    rows, d = c_pad.shape
    n = w.shape[1]
    tn = n // 4
    return pl.pallas_call(
        _adaln_kernel,
        grid=(n // tn,),
        in_specs=[pl.BlockSpec((rows, d), lambda j: (0, 0)),
                  pl.BlockSpec((d, tn), lambda j: (0, j)),
                  pl.BlockSpec((1, tn), lambda j: (0, j))],
        out_specs=pl.BlockSpec((rows, tn), lambda j: (0, j)),
        out_shape=jax.ShapeDtypeStruct((rows, n), F32),
        compiler_params=pltpu.CompilerParams(
            dimension_semantics=("arbitrary",), vmem_limit_bytes=VMEM_LIMIT),
        name="adaln",
    )(c_pad, w, b)


def _proj_kernel(x_ref, mod_ref, ng_ref, w_ref, qg_ref, kg_ref, cos_ref, sin_ref,
                 o_ref, h_scr, *, heads):
    n = pl.program_id(2)

    @pl.when(n == 0)
    def _():
        x = x_ref[0]
        ms = jnp.mean(x * x, axis=-1, keepdims=True)
        y = x * lax.rsqrt(ms + EPS) * ng_ref[...]
        shift = mod_ref[0, 0:1, :]
        scale = mod_ref[0, 1:2, :]
        h_scr[...] = (y * (1.0 + scale) + shift).astype(BF16)

    acc = jnp.dot(h_scr[...], w_ref[...], preferred_element_type=F32)

    def per_head(fn):
        for j in range(heads):
            o_ref[0, j] = fn(acc[:, j * HEAD_DIM:(j + 1) * HEAD_DIM]).astype(BF16)

    def qk_norm(gain_ref, mult):
        def fn(a):
            ms = jnp.mean(a * a, axis=-1, keepdims=True)
            return a * lax.rsqrt(ms + EPS) * (gain_ref[...] * mult)
        return fn

    def rotary(mult):
        def fn(a):
            r = a * cos_ref[...] + pltpu.roll(a, HEAD_DIM // 2, 1) * sin_ref[...]
            return r * mult if mult != 1.0 else r
        return fn

    inv_sqrt_d = float(1.0 / np.sqrt(HEAD_DIM))
    pl.when(n == 0)(lambda: per_head(qk_norm(qg_ref, inv_sqrt_d)))
    pl.when(n == 1)(lambda: per_head(qk_norm(kg_ref, 1.0)))
    pl.when((n == 2) | (n == 6))(lambda: per_head(lambda a: a))
    pl.when((n == 3) | (n == 7))(lambda: per_head(_silu))
    pl.when(n == 4)(lambda: per_head(rotary(1.0)))
    pl.when(n == 5)(lambda: per_head(rotary(float(HEAD_DIM ** -0.5))))


def _in_proj(x, mod3, norm_gain, w_bf16, q_gain, k_gain, cos_t, sin_t):
    b, s, d = x.shape
    width = w_bf16.shape[1]
    tn = width // N_GROUPS
    heads = tn // HEAD_DIM
    tm = PROJ_ROWS
    return pl.pallas_call(
        functools.partial(_proj_kernel, heads=heads),
        grid=(b, s // tm, N_GROUPS),
        in_specs=[
            pl.BlockSpec((1, tm, d), lambda bi, si, n: (bi, si, 0)),
            pl.BlockSpec((1, 3, d), lambda bi, si, n: (bi, 0, 0)),
            pl.BlockSpec((1, d), lambda bi, si, n: (0, 0)),
            pl.BlockSpec((d, tn), lambda bi, si, n: (0, n)),
            pl.BlockSpec((1, HEAD_DIM), lambda bi, si, n: (0, 0)),
            pl.BlockSpec((1, HEAD_DIM), lambda bi, si, n: (0, 0)),
            pl.BlockSpec((tm, HEAD_DIM), lambda bi, si, n: (si, 0)),
            pl.BlockSpec((tm, HEAD_DIM), lambda bi, si, n: (si, 0)),
        ],
        out_specs=pl.BlockSpec((1, heads, tm, HEAD_DIM), lambda bi, si, n: (bi, n, si, 0)),
        out_shape=jax.ShapeDtypeStruct((b, N_GROUPS * heads, s, HEAD_DIM), BF16),
        scratch_shapes=[pltpu.VMEM((tm, d), BF16)],
        compiler_params=pltpu.CompilerParams(
            dimension_semantics=("parallel", "parallel", "arbitrary"),
            vmem_limit_bytes=VMEM_LIMIT),
        name="in_proj",
    )(x, mod3, norm_gain, w_bf16, q_gain, k_gain, cos_t, sin_t)


def _sb_tile(q, k_blk, v_blk, neg_upper, carry, acc, past):
    z = lax.dot_general(q, k_blk, (((1,), (1,)), ((), ())), preferred_element_type=F32)
    l = jnp.log(1.0 + jnp.exp(-jnp.abs(z)))
    sp = jnp.maximum(z, 0.0) + l
    logb = jnp.minimum(z, 0.0) - l
    if past is not None:
        sp = jnp.where(past, sp, 0.0)
    later = jnp.dot(sp.astype(BF16), neg_upper, preferred_element_type=F32)
    w = jnp.exp(logb + later + carry)
    if past is not None:
        w = jnp.where(past, w, 0.0)
    acc = acc + jnp.dot(w.astype(BF16), v_blk, preferred_element_type=F32)
    carry = carry - jnp.sum(sp, axis=1, keepdims=True)
    return carry, acc


def _sb_kernel(q_ref, k_ref, v_ref, g_ref, o_ref, *, tq, tk):
    i = pl.program_id(2)
    q = q_ref[0, 0]
    sub = tq // tk
    row = lax.broadcasted_iota(jnp.int32, (tk, tk), 0)
    col = lax.broadcasted_iota(jnp.int32, (tk, tk), 1)
    neg_upper = jnp.where(row > col, -1.0, 0.0).astype(BF16)

    carry = jnp.zeros((tq, 1), F32)
    acc = jnp.zeros((tq, HEAD_DIM), F32)
    base = pl.multiple_of(i * tq, tq)

    q_pos = lax.broadcasted_iota(jnp.int32, (tq, tk), 0)
    k_pos = lax.broadcasted_iota(jnp.int32, (tq, tk), 1)
    for c in range(sub - 1, -1, -1):
        start = pl.multiple_of(base + c * tk, tk)
        k_blk = k_ref[0, 0, pl.ds(start, tk), :]
        v_blk = v_ref[0, 0, pl.ds(start, tk), :]
        past = (k_pos + c * tk) < q_pos
        carry, acc = _sb_tile(q, k_blk, v_blk, neg_upper, carry, acc, past)

    def body(jj, state):
        carry, acc = state
        start = pl.multiple_of(base - (jj + 1) * tk, tk)
        k_blk = k_ref[0, 0, pl.ds(start, tk), :]
        v_blk = v_ref[0, 0, pl.ds(start, tk), :]
        return _sb_tile(q, k_blk, v_blk, neg_upper, carry, acc, None)

    carry, acc = lax.fori_loop(0, i * sub, body, (carry, acc))
    o_ref[0, 0] = (acc * g_ref[0, 0].astype(F32)).astype(BF16)


def _sb_attention(proj, heads):
    b, _, s, d = proj.shape
    tq, tk = SB_Q_ROWS, SB_K_ROWS
    return pl.pallas_call(
        functools.partial(_sb_kernel, tq=tq, tk=tk),
        grid=(b, heads, s // tq),
        in_specs=[
            pl.BlockSpec((1, 1, tq, d), lambda bi, h, i: (bi, h, i, 0)),
            pl.BlockSpec((1, 1, s, d), lambda bi, h, i: (bi, heads + h, 0, 0)),
            pl.BlockSpec((1, 1, s, d), lambda bi, h, i: (bi, 2 * heads + h, 0, 0)),
            pl.BlockSpec((1, 1, tq, d), lambda bi, h, i: (bi, 3 * heads + h, i, 0)),
        ],
        out_specs=pl.BlockSpec((1, 1, tq, d), lambda bi, h, i: (bi, h, i, 0)),
        out_shape=jax.ShapeDtypeStruct((b, heads, s, d), BF16),
        compiler_params=pltpu.CompilerParams(
            dimension_semantics=("parallel", "parallel", "arbitrary"),
            vmem_limit_bytes=VMEM_LIMIT),
        name="sb_attn",
    )(proj, proj, proj, proj)


def _ret_kernel(q_ref, k_ref, v_ref, g_ref, din_ref, qd_ref, kd_ref, cd_ref, gain_ref,
                o_ref, state_scr):
    ci = pl.program_id(2)

    @pl.when(ci == 0)
    def _():
        state_scr[...] = jnp.zeros_like(state_scr)

    qc = q_ref[0, 0]
    kc = k_ref[0, 0]
    vc = v_ref[0, 0]
    state = state_scr[...]

    scores = lax.dot_general(qc, kc, (((1,), (1,)), ((), ())), preferred_element_type=F32)
    inner = jnp.dot((scores * din_ref[0]).astype(BF16), vc, preferred_element_type=F32)
    q_dec = (qc.astype(F32) * qd_ref[0]).astype(BF16)
    cross = jnp.dot(q_dec, state.astype(BF16), preferred_element_type=F32)
    k_dec = (kc.astype(F32) * kd_ref[0]).astype(BF16)
    kv = lax.dot_general(k_dec, vc, (((0,), (0,)), ((), ())), preferred_element_type=F32)
    state_scr[...] = state * cd_ref[0] + kv

    o = inner + cross
    mu = jnp.mean(o, axis=-1, keepdims=True)
    cen = o - mu
    var = jnp.mean(cen * cen, axis=-1, keepdims=True)
    y = cen * lax.rsqrt(var + EPS) * gain_ref[...]
    o_ref[0, 0] = (y * g_ref[0, 0].astype(F32)).astype(BF16)


def _retention(proj, heads, din, qd, kd, cd, gain):
    b, _, s, d = proj.shape
    c = RET_CHUNK
    blk = lambda off: pl.BlockSpec((1, 1, c, d), lambda bi, h, ci: (bi, off * heads + h, ci, 0))
    return pl.pallas_call(
        _ret_kernel,
        grid=(b, heads, s // c),
        in_specs=[
            blk(4), blk(5), blk(6), blk(7),
            pl.BlockSpec((1, c, c), lambda bi, h, ci: (h, 0, 0)),
            pl.BlockSpec((1, c, d), lambda bi, h, ci: (h, 0, 0)),
            pl.BlockSpec((1, c, d), lambda bi, h, ci: (h, 0, 0)),
            pl.BlockSpec((1, 1, d), lambda bi, h, ci: (h, 0, 0)),
            pl.BlockSpec((1, d), lambda bi, h, ci: (0, h)),
        ],
        out_specs=pl.BlockSpec((1, 1, c, d), lambda bi, h, ci: (bi, h, ci, 0)),
        out_shape=jax.ShapeDtypeStruct((b, heads, s, d), BF16),
        scratch_shapes=[pltpu.VMEM((d, d), F32)],
        compiler_params=pltpu.CompilerParams(
            dimension_semantics=("parallel", "parallel", "arbitrary"),
            vmem_limit_bytes=VMEM_LIMIT),
        name="retention",
    )(proj, proj, proj, proj, din, qd, kd, cd, gain)


def _retention_tables(heads, c):
    log_gamma = jnp.log1p(-jnp.exp2(-5.0 - jnp.arange(heads, dtype=F32)))
    idx = jnp.arange(c, dtype=F32)
    rel = idx[:, None] - idx[None, :]
    din = jnp.where(rel[None] >= 0,
                    jnp.exp(jnp.maximum(rel, 0.0)[None] * log_gamma[:, None, None]), 0.0)
    qd = jnp.exp((idx[None, :] + 1.0) * log_gamma[:, None])[..., None]
    kd = jnp.exp((c - 1.0 - idx[None, :]) * log_gamma[:, None])[..., None]
    cd = jnp.exp(c * log_gamma)[:, None, None]
    bc = lambda t: jnp.broadcast_to(t, t.shape[:-1] + (HEAD_DIM,))
    return din, bc(qd), bc(kd), bc(cd)


def _out_kernel(a_ref, r_ref, w_ref, x_ref, mod_ref, o_ref, *, heads):
    parts = [a_ref[0, j] for j in range(heads)] + [r_ref[0, j] for j in range(heads)]
    mix = jnp.concatenate(parts, axis=-1)
    y = jnp.dot(mix, w_ref[...], preferred_element_type=F32)
    o_ref[0] = x_ref[0] + mod_ref[0, 2:3, :] * y


def _out_proj(oa, ob, w_bf16, x, mod3):
    b, s, d = x.shape
    heads = oa.shape[1]
    tm = OUT_ROWS
    return pl.pallas_call(
        functools.partial(_out_kernel, heads=heads),
        grid=(b, s // tm),
        in_specs=[
            pl.BlockSpec((1, heads, tm, HEAD_DIM), lambda bi, si: (bi, 0, si, 0)),
            pl.BlockSpec((1, heads, tm, HEAD_DIM), lambda bi, si: (bi, 0, si, 0)),
            pl.BlockSpec(w_bf16.shape, lambda bi, si: (0, 0)),
            pl.BlockSpec((1, tm, d), lambda bi, si: (bi, si, 0)),
            pl.BlockSpec((1, 3, d), lambda bi, si: (bi, 0, 0)),
        ],
        out_specs=pl.BlockSpec((1, tm, d), lambda bi, si: (bi, si, 0)),
        out_shape=jax.ShapeDtypeStruct((b, s, d), F32),
        compiler_params=pltpu.CompilerParams(
            dimension_semantics=("parallel", "parallel"),
            vmem_limit_bytes=VMEM_LIMIT),
        name="out_proj",
    )(oa, ob, w_bf16, x, mod3)


def _rotary_tables(s):
    half = HEAD_DIM // 2
    inv_freq = ROPE_BASE ** (-jnp.arange(half, dtype=F32) / half)
    ang = jnp.arange(s, dtype=F32)[:, None] * inv_freq[None, :]
    cos, sin = jnp.cos(ang), jnp.sin(ang)
    return jnp.concatenate([cos, cos], axis=-1), jnp.concatenate([-sin, sin], axis=-1)


def kernel(x, c, w_ada, b_ada, norm_gain, w_in, sb_q_gain, sb_k_gain, ret_norm_gain, w_out):
    b, s, d = x.shape
    depth = w_ada.shape[0]
    heads = w_in.shape[2] // (N_GROUPS * HEAD_DIM)
    cos_t, sin_t = _rotary_tables(s)
    din, qd, kd, cd = _retention_tables(heads, RET_CHUNK)
    c_pad = jnp.pad(c, ((0, 8 - b), (0, 0)))
    for layer in range(depth):
        mod = _adaln(c_pad, w_ada[layer], b_ada[layer][None, :])
        mod3 = mod[:b].reshape(b, 3, d)
        proj = _in_proj(x, mod3, norm_gain[layer][None, :], w_in[layer].astype(BF16),
                        sb_q_gain[layer][None, :], sb_k_gain[layer][None, :], cos_t, sin_t)
        oa = _sb_attention(proj, heads)
        ob = _retention(proj, heads, din, qd, kd, cd, ret_norm_gain[layer][None, :])
        x = _out_proj(oa, ob, w_out[layer].astype(BF16), x, mod3)
    return x
```

```python
import functools

import jax
import jax.numpy as jnp
import numpy as np
from jax import lax
from jax.experimental import pallas as pl
from jax.experimental.pallas import tpu as pltpu

HEAD_DIM = 128
ROPE_BASE = 10000.0
EPS = 1e-6
N_GROUPS = 8
F32 = jnp.float32
BF16 = jnp.bfloat16

PROJ_ROWS = 1024
SB_Q_ROWS = 512
SB_K_ROWS = 256
RET_CHUNK = 256
OUT_ROWS = 512
VMEM_LIMIT = 56 * 1024 * 1024
LOG_WEIGHT_FLOOR = -120.0


def _silu(v):
    return v / (1.0 + jnp.exp(-v))


def _adaln_kernel(c_ref, w_ref, b_ref, o_ref):
    s = _silu(c_ref[...])
    o_ref[...] = jnp.dot(s, w_ref[...], preferred_element_type=F32,
                         precision=lax.Precision.HIGHEST) + b_ref[...]


def _adaln(c_pad, w, b):
    rows, d = c_pad.shape
    n = w.shape[1]
    tn = n // 4
    return pl.pallas_call(
        _adaln_kernel,
        grid=(n // tn,),
        in_specs=[pl.BlockSpec((rows, d), lambda j: (0, 0)),
                  pl.BlockSpec((d, tn), lambda j: (0, j)),
                  pl.BlockSpec((1, tn), lambda j: (0, j))],
        out_specs=pl.BlockSpec((rows, tn), lambda j: (0, j)),
        out_shape=jax.ShapeDtypeStruct((rows, n), F32),
        compiler_params=pltpu.CompilerParams(
            dimension_semantics=("arbitrary",), vmem_limit_bytes=VMEM_LIMIT),
        name="adaln",
    )(c_pad, w, b)


def _proj_kernel(x_ref, mod_ref, ng_ref, w_ref, qg_ref, kg_ref, cos_ref, sin_ref,
                 o_ref, h_scr, *, heads):
    n = pl.program_id(2)

    @pl.when(n == 0)
    def _():
        x = x_ref[0]
        ms = jnp.mean(x * x, axis=-1, keepdims=True)
        y = x * lax.rsqrt(ms + EPS) * ng_ref[...]
        shift = mod_ref[0, 0:1, :]
        scale = mod_ref[0, 1:2, :]
        h_scr[...] = (y * (1.0 + scale) + shift).astype(BF16)

    acc = jnp.dot(h_scr[...], w_ref[...], preferred_element_type=F32)

    def per_head(fn):
        for j in range(heads):
            o_ref[0, j] = fn(acc[:, j * HEAD_DIM:(j + 1) * HEAD_DIM]).astype(BF16)

    def qk_norm(gain_ref, mult):
        def fn(a):
            ms = jnp.mean(a * a, axis=-1, keepdims=True)
            return a * lax.rsqrt(ms + EPS) * (gain_ref[...] * mult)
        return fn

    def rotary(mult):
        def fn(a):
            r = a * cos_ref[...] + pltpu.roll(a, HEAD_DIM // 2, 1) * sin_ref[...]
            return r * mult if mult != 1.0 else r
        return fn

    inv_sqrt_d = float(1.0 / np.sqrt(HEAD_DIM))
    pl.when(n == 0)(lambda: per_head(qk_norm(qg_ref, inv_sqrt_d)))
    pl.when(n == 1)(lambda: per_head(qk_norm(kg_ref, 1.0)))
    pl.when((n == 2) | (n == 6))(lambda: per_head(lambda a: a))
    pl.when((n == 3) | (n == 7))(lambda: per_head(_silu))
    pl.when(n == 4)(lambda: per_head(rotary(1.0)))
    pl.when(n == 5)(lambda: per_head(rotary(float(HEAD_DIM ** -0.5))))


def _in_proj(x, mod3, norm_gain, w_bf16, q_gain, k_gain, cos_t, sin_t):
    b, s, d = x.shape
    width = w_bf16.shape[1]
    tn = width // N_GROUPS
    heads = tn // HEAD_DIM
    tm = PROJ_ROWS
    return pl.pallas_call(
        functools.partial(_proj_kernel, heads=heads),
        grid=(b, s // tm, N_GROUPS),
        in_specs=[
            pl.BlockSpec((1, tm, d), lambda bi, si, n: (bi, si, 0)),
            pl.BlockSpec((1, 3, d), lambda bi, si, n: (bi, 0, 0)),
            pl.BlockSpec((1, d), lambda bi, si, n: (0, 0)),
            pl.BlockSpec((d, tn), lambda bi, si, n: (0, n)),
            pl.BlockSpec((1, HEAD_DIM), lambda bi, si, n: (0, 0)),
            pl.BlockSpec((1, HEAD_DIM), lambda bi, si, n: (0, 0)),
            pl.BlockSpec((tm, HEAD_DIM), lambda bi, si, n: (si, 0)),
            pl.BlockSpec((tm, HEAD_DIM), lambda bi, si, n: (si, 0)),
        ],
        out_specs=pl.BlockSpec((1, heads, tm, HEAD_DIM), lambda bi, si, n: (bi, n, si, 0)),
        out_shape=jax.ShapeDtypeStruct((b, N_GROUPS * heads, s, HEAD_DIM), BF16),
        scratch_shapes=[pltpu.VMEM((tm, d), BF16)],
        compiler_params=pltpu.CompilerParams(
            dimension_semantics=("parallel", "parallel", "arbitrary"),
            vmem_limit_bytes=VMEM_LIMIT),
        name="in_proj",
    )(x, mod3, norm_gain, w_bf16, q_gain, k_gain, cos_t, sin_t)


def _sb_tile(q, k_blk, v_blk, neg_upper, carry, acc, past):
    z = lax.dot_general(q, k_blk, (((1,), (1,)), ((), ())), preferred_element_type=F32)
    l = jnp.log(1.0 + jnp.exp(-jnp.abs(z)))
    sp = jnp.maximum(z, 0.0) + l
    logb = jnp.minimum(z, 0.0) - l
    if past is not None:
        sp = jnp.where(past, sp, 0.0)
    later = jnp.dot(sp.astype(BF16), neg_upper, preferred_element_type=F32)
    w = jnp.exp(logb + later + carry)
    if past is not None:
        w = jnp.where(past, w, 0.0)
    acc = acc + jnp.dot(w.astype(BF16), v_blk, preferred_element_type=F32)
    carry = carry - jnp.sum(sp, axis=1, keepdims=True)
    return carry, acc


def _sb_kernel(q_ref, k_ref, v_ref, g_ref, o_ref, *, tq, tk):
    i = pl.program_id(2)
    q = q_ref[0, 0]
    sub = tq // tk
    row = lax.broadcasted_iota(jnp.int32, (tk, tk), 0)
    col = lax.broadcasted_iota(jnp.int32, (tk, tk), 1)
    neg_upper = jnp.where(row > col, -1.0, 0.0).astype(BF16)

    carry = jnp.zeros((tq, 1), F32)
    acc = jnp.zeros((tq, HEAD_DIM), F32)
    base = pl.multiple_of(i * tq, tq)

    q_pos = lax.broadcasted_iota(jnp.int32, (tq, tk), 0)
    k_pos = lax.broadcasted_iota(jnp.int32, (tq, tk), 1)
    for c in range(sub - 1, -1, -1):
        start = pl.multiple_of(base + c * tk, tk)
        k_blk = k_ref[0, 0, pl.ds(start, tk), :]
        v_blk = v_ref[0, 0, pl.ds(start, tk), :]
        past = (k_pos + c * tk) < q_pos
        carry, acc = _sb_tile(q, k_blk, v_blk, neg_upper, carry, acc, past)

    def cond(state):
        jj, live, _, _ = state
        return (jj < i * sub) & live

    def body(state):
        jj, _, carry, acc = state
        start = pl.multiple_of(base - (jj + 1) * tk, tk)
        k_blk = k_ref[0, 0, pl.ds(start, tk), :]
        v_blk = v_ref[0, 0, pl.ds(start, tk), :]
        carry, acc = _sb_tile(q, k_blk, v_blk, neg_upper, carry, acc, None)
        return jj + 1, jnp.max(carry) > LOG_WEIGHT_FLOOR, carry, acc

    live = jnp.max(carry) > LOG_WEIGHT_FLOOR
    _, _, carry, acc = lax.while_loop(cond, body, (jnp.int32(0), live, carry, acc))
    o_ref[0, 0] = (acc * g_ref[0, 0].astype(F32)).astype(BF16)


def _sb_attention(proj, heads):
    b, _, s, d = proj.shape
    tq, tk = SB_Q_ROWS, SB_K_ROWS
    return pl.pallas_call(
        functools.partial(_sb_kernel, tq=tq, tk=tk),
        grid=(b, heads, s // tq),
        in_specs=[
            pl.BlockSpec((1, 1, tq, d), lambda bi, h, i: (bi, h, i, 0)),
            pl.BlockSpec((1, 1, s, d), lambda bi, h, i: (bi, heads + h, 0, 0)),
            pl.BlockSpec((1, 1, s, d), lambda bi, h, i: (bi, 2 * heads + h, 0, 0)),
            pl.BlockSpec((1, 1, tq, d), lambda bi, h, i: (bi, 3 * heads + h, i, 0)),
        ],
        out_specs=pl.BlockSpec((1, 1, tq, d), lambda bi, h, i: (bi, h, i, 0)),
        out_shape=jax.ShapeDtypeStruct((b, heads, s, d), BF16),
        compiler_params=pltpu.CompilerParams(
            dimension_semantics=("parallel", "parallel", "arbitrary"),
            vmem_limit_bytes=VMEM_LIMIT),
        name="sb_attn",
    )(proj, proj, proj, proj)


def _ret_kernel(q_ref, k_ref, v_ref, g_ref, din_ref, qd_ref, kd_ref, cd_ref, gain_ref,
                o_ref, state_scr):
    ci = pl.program_id(2)

    @pl.when(ci == 0)
    def _():
        state_scr[...] = jnp.zeros_like(state_scr)

    qc = q_ref[0, 0]
    kc = k_ref[0, 0]
    vc = v_ref[0, 0]
    state = state_scr[...]

    scores = lax.dot_general(qc, kc, (((1,), (1,)), ((), ())), preferred_element_type=F32)
    inner = jnp.dot((scores * din_ref[0]).astype(BF16), vc, preferred_element_type=F32)
    q_dec = (qc.astype(F32) * qd_ref[0]).astype(BF16)
    cross = jnp.dot(q_dec, state.astype(BF16), preferred_element_type=F32)
    k_dec = (kc.astype(F32) * kd_ref[0]).astype(BF16)
    kv = lax.dot_general(k_dec, vc, (((0,), (0,)), ((), ())), preferred_element_type=F32)
    state_scr[...] = state * cd_ref[0] + kv

    o = inner + cross
    mu = jnp.mean(o, axis=-1, keepdims=True)
    cen = o - mu
    var = jnp.mean(cen * cen, axis=-1, keepdims=True)
    y = cen * lax.rsqrt(var + EPS) * gain_ref[...]
    o_ref[0, 0] = (y * g_ref[0, 0].astype(F32)).astype(BF16)


def _retention(proj, heads, din, qd, kd, cd, gain):
    b, _, s, d = proj.shape
    c = RET_CHUNK
    blk = lambda off: pl.BlockSpec((1, 1, c, d), lambda bi, h, ci: (bi, off * heads + h, ci, 0))
    return pl.pallas_call(
        _ret_kernel,
        grid=(b, heads, s // c),
        in_specs=[
            blk(4), blk(5), blk(6), blk(7),
            pl.BlockSpec((1, c, c), lambda bi, h, ci: (h, 0, 0)),
            pl.BlockSpec((1, c, d), lambda bi, h, ci: (h, 0, 0)),
            pl.BlockSpec((1, c, d), lambda bi, h, ci: (h, 0, 0)),
            pl.BlockSpec((1, 1, d), lambda bi, h, ci: (h, 0, 0)),
            pl.BlockSpec((1, d), lambda bi, h, ci: (0, h)),
        ],
        out_specs=pl.BlockSpec((1, 1, c, d), lambda bi, h, ci: (bi, h, ci, 0)),
        out_shape=jax.ShapeDtypeStruct((b, heads, s, d), BF16),
        scratch_shapes=[pltpu.VMEM((d, d), F32)],
        compiler_params=pltpu.CompilerParams(
            dimension_semantics=("parallel", "parallel", "arbitrary"),
            vmem_limit_bytes=VMEM_LIMIT),
        name="retention",
    )(proj, proj, proj, proj, din, qd, kd, cd, gain)


def _retention_tables(heads, c):
    log_gamma = jnp.log1p(-jnp.exp2(-5.0 - jnp.arange(heads, dtype=F32)))
    idx = jnp.arange(c, dtype=F32)
    rel = idx[:, None] - idx[None, :]
    din = jnp.where(rel[None] >= 0,
                    jnp.exp(jnp.maximum(rel, 0.0)[None] * log_gamma[:, None, None]), 0.0)
    qd = jnp.exp((idx[None, :] + 1.0) * log_gamma[:, None])[..., None]
    kd = jnp.exp((c - 1.0 - idx[None, :]) * log_gamma[:, None])[..., None]
    cd = jnp.exp(c * log_gamma)[:, None, None]
    bc = lambda t: jnp.broadcast_to(t, t.shape[:-1] + (HEAD_DIM,))
    return din, bc(qd), bc(kd), bc(cd)


def _out_kernel(a_ref, r_ref, w_ref, x_ref, mod_ref, o_ref, *, heads):
    parts = [a_ref[0, j] for j in range(heads)] + [r_ref[0, j] for j in range(heads)]
    mix = jnp.concatenate(parts, axis=-1)
    y = jnp.dot(mix, w_ref[...], preferred_element_type=F32)
    o_ref[0] = x_ref[0] + mod_ref[0, 2:3, :] * y


def _out_proj(oa, ob, w_bf16, x, mod3):
    b, s, d = x.shape
    heads = oa.shape[1]
    tm = OUT_ROWS
    return pl.pallas_call(
        functools.partial(_out_kernel, heads=heads),
        grid=(b, s // tm),
        in_specs=[
            pl.BlockSpec((1, heads, tm, HEAD_DIM), lambda bi, si: (bi, 0, si, 0)),
            pl.BlockSpec((1, heads, tm, HEAD_DIM), lambda bi, si: (bi, 0, si, 0)),
            pl.BlockSpec(w_bf16.shape, lambda bi, si: (0, 0)),
            pl.BlockSpec((1, tm, d), lambda bi, si: (bi, si, 0)),
            pl.BlockSpec((1, 3, d), lambda bi, si: (bi, 0, 0)),
        ],
        out_specs=pl.BlockSpec((1, tm, d), lambda bi, si: (bi, si, 0)),
        out_shape=jax.ShapeDtypeStruct((b, s, d), F32),
        compiler_params=pltpu.CompilerParams(
            dimension_semantics=("parallel", "parallel"),
            vmem_limit_bytes=VMEM_LIMIT),
        name="out_proj",
    )(oa, ob, w_bf16, x, mod3)


def _rotary_tables(s):
    half = HEAD_DIM // 2
    inv_freq = ROPE_BASE ** (-jnp.arange(half, dtype=F32) / half)
    ang = jnp.arange(s, dtype=F32)[:, None] * inv_freq[None, :]
    cos, sin = jnp.cos(ang), jnp.sin(ang)
    return jnp.concatenate([cos, cos], axis=-1), jnp.concatenate([-sin, sin], axis=-1)


def kernel(x, c, w_ada, b_ada, norm_gain, w_in, sb_q_gain, sb_k_gain, ret_norm_gain, w_out):
    b, s, d = x.shape
    depth = w_ada.shape[0]
    heads = w_in.shape[2] // (N_GROUPS * HEAD_DIM)
    cos_t, sin_t = _rotary_tables(s)
    din, qd, kd, cd = _retention_tables(heads, RET_CHUNK)
    c_pad = jnp.pad(c, ((0, 8 - b), (0, 0)))
    for layer in range(depth):
        mod = _adaln(c_pad, w_ada[layer], b_ada[layer][None, :])
        mod3 = mod[:b].reshape(b, 3, d)
        proj = _in_proj(x, mod3, norm_gain[layer][None, :], w_in[layer].astype(BF16),
                        sb_q_gain[layer][None, :], sb_k_gain[layer][None, :], cos_t, sin_t)
        oa = _sb_attention(proj, heads)
        ob = _retention(proj, heads, din, qd, kd, cd, ret_norm_gain[layer][None, :])
        x = _out_proj(oa, ob, w_out[layer].astype(BF16), x, mod3)
    return x
```

```python
import functools

import jax
import jax.numpy as jnp
import numpy as np
from jax import lax
from jax.experimental import pallas as pl
from jax.experimental.pallas import tpu as pltpu

HEAD_DIM = 128
ROPE_BASE = 10000.0
EPS = 1e-6
N_GROUPS = 8
F32 = jnp.float32
BF16 = jnp.bfloat16

PROJ_ROWS = 1024
SB_TILE = 256
SB_HEADS_PER_STEP = 4
RET_CHUNK = 256
OUT_ROWS = 512
VMEM_LIMIT = 56 * 1024 * 1024
LOG_WEIGHT_FLOOR = -120.0
NO_TILE_CARRY = -1e30
LOG2E = 1.4426950408889634


def _silu(v):
    return v / (1.0 + jnp.exp(-v))


def _adaln_kernel(c_ref, w_ref, b_ref, o_ref):
    s = _silu(c_ref[...])
    o_ref[...] = jnp.dot(s, w_ref[...], preferred_element_type=F32,
                         precision=lax.Precision.HIGHEST) + b_ref[...]


def _adaln(c_pad, w, b):
    rows, d = c_pad.shape
    n = w.shape[1]
    tn = n // 4
    return pl.pallas_call(
        _adaln_kernel,
        grid=(n // tn,),
        in_specs=[pl.BlockSpec((rows, d), lambda j: (0, 0)),
                  pl.BlockSpec((d, tn), lambda j: (0, j)),
                  pl.BlockSpec((1, tn), lambda j: (0, j))],
        out_specs=pl.BlockSpec((rows, tn), lambda j: (0, j)),
        out_shape=jax.ShapeDtypeStruct((rows, n), F32),
        compiler_params=pltpu.CompilerParams(
            dimension_semantics=("arbitrary",), vmem_limit_bytes=VMEM_LIMIT),
        name="adaln",
    )(c_pad, w, b)


def _proj_kernel(x_ref, mod_ref, ng_ref, w_ref, qg_ref, kg_ref, cos_ref, sin_ref,
                 o_ref, h_scr, *, heads):
    n = pl.program_id(2)

    @pl.when(n == 0)
    def _():
        x = x_ref[0]
        ms = jnp.mean(x * x, axis=-1, keepdims=True)
        y = x * lax.rsqrt(ms + EPS) * ng_ref[...]
        shift = mod_ref[0, 0:1, :]
        scale = mod_ref[0, 1:2, :]
        h_scr[...] = (y * (1.0 + scale) + shift).astype(BF16)

    acc = jnp.dot(h_scr[...], w_ref[...], preferred_element_type=F32)

    def per_head(fn):
        for j in range(heads):
            o_ref[0, j] = fn(acc[:, j * HEAD_DIM:(j + 1) * HEAD_DIM]).astype(BF16)

    def qk_norm(gain_ref, mult):
        def fn(a):
            ms = jnp.mean(a * a, axis=-1, keepdims=True)
            return a * lax.rsqrt(ms + EPS) * (gain_ref[...] * mult)
        return fn

    def rotary(mult):
        def fn(a):
            r = a * cos_ref[...] + pltpu.roll(a, HEAD_DIM // 2, 1) * sin_ref[...]
            return r * mult if mult != 1.0 else r
        return fn

    inv_sqrt_d = float(1.0 / np.sqrt(HEAD_DIM))
    pl.when(n == 0)(lambda: per_head(qk_norm(qg_ref, inv_sqrt_d)))
    pl.when(n == 1)(lambda: per_head(qk_norm(kg_ref, 1.0)))
    pl.when((n == 2) | (n == 6))(lambda: per_head(lambda a: a))
    pl.when((n == 3) | (n == 7))(lambda: per_head(_silu))
    pl.when(n == 4)(lambda: per_head(rotary(1.0)))
    pl.when(n == 5)(lambda: per_head(rotary(float(HEAD_DIM ** -0.5))))


def _in_proj(x, mod3, norm_gain, w_bf16, q_gain, k_gain, cos_t, sin_t):
    b, s, d = x.shape
    width = w_bf16.shape[1]
    tn = width // N_GROUPS
    heads = tn // HEAD_DIM
    tm = PROJ_ROWS
    return pl.pallas_call(
        functools.partial(_proj_kernel, heads=heads),
        grid=(b, s // tm, N_GROUPS),
        in_specs=[
            pl.BlockSpec((1, tm, d), lambda bi, si, n: (bi, si, 0)),
            pl.BlockSpec((1, 3, d), lambda bi, si, n: (bi, 0, 0)),
            pl.BlockSpec((1, d), lambda bi, si, n: (0, 0)),
            pl.BlockSpec((d, tn), lambda bi, si, n: (0, n)),
            pl.BlockSpec((1, HEAD_DIM), lambda bi, si, n: (0, 0)),
            pl.BlockSpec((1, HEAD_DIM), lambda bi, si, n: (0, 0)),
            pl.BlockSpec((tm, HEAD_DIM), lambda bi, si, n: (si, 0)),
            pl.BlockSpec((tm, HEAD_DIM), lambda bi, si, n: (si, 0)),
        ],
        out_specs=pl.BlockSpec((1, heads, tm, HEAD_DIM), lambda bi, si, n: (bi, n, si, 0)),
        out_shape=jax.ShapeDtypeStruct((b, N_GROUPS * heads, s, HEAD_DIM), BF16),
        scratch_shapes=[pltpu.VMEM((tm, d), BF16)],
        compiler_params=pltpu.CompilerParams(
            dimension_semantics=("parallel", "parallel", "arbitrary"),
            vmem_limit_bytes=VMEM_LIMIT),
        name="in_proj",
    )(x, mod3, norm_gain, w_bf16, q_gain, k_gain, cos_t, sin_t)


def _sb_scores(q, k_blk):
    z = lax.dot_general(q, k_blk, (((1,), (1,)), ((), ())), preferred_element_type=F32)
    l = jnp.log(1.0 + jnp.exp2(jnp.abs(z) * (-LOG2E)))
    return jnp.maximum(z, 0.0) + l, jnp.minimum(z, 0.0) - l


def _sb_later(sp, neg_upper):
    return jnp.dot(sp.astype(BF16), neg_upper, preferred_element_type=F32)


def _sb_kernel(q_ref, k_ref, v_ref, g_ref, o_ref, *, t, heads_per_step):
    i = pl.program_id(2)
    row = lax.broadcasted_iota(jnp.int32, (t, t), 0)
    col = lax.broadcasted_iota(jnp.int32, (t, t), 1)
    neg_upper = jnp.where(row > col, -1.0, 0.0).astype(BF16)
    past = col < row
    diag = pl.multiple_of(i * t, t)
    prev = pl.multiple_of(jnp.maximum(i - 1, 0) * t, t)
    no_prev = jnp.where(i == 0, NO_TILE_CARRY, 0.0).astype(F32)

    states = []
    for hh in range(heads_per_step):
        q = q_ref[0, hh]
        sp_d, logb_d = _sb_scores(q, k_ref[0, hh, pl.ds(diag, t), :])
        sp_p, logb_p = _sb_scores(q, k_ref[0, hh, pl.ds(prev, t), :])
        sp_d = jnp.where(past, sp_d, 0.0)
        w_d = jnp.where(past, jnp.exp(logb_d + _sb_later(sp_d, neg_upper)), 0.0)
        carry = no_prev - jnp.sum(sp_d, axis=1, keepdims=True)
        w_p = jnp.exp(logb_p + _sb_later(sp_p, neg_upper) + carry)
        acc = (jnp.dot(w_d.astype(BF16), v_ref[0, hh, pl.ds(diag, t), :], preferred_element_type=F32)
               + jnp.dot(w_p.astype(BF16), v_ref[0, hh, pl.ds(prev, t), :], preferred_element_type=F32))
        carry = carry - jnp.sum(sp_p, axis=1, keepdims=True)
        states.append((carry, acc))

    for hh in range(heads_per_step):
        q = q_ref[0, hh]

        def cond(state):
            jj, live, _, _ = state
            return (jj <= i) & live

        def body(state, hh=hh, q=q):
            jj, _, carry, acc = state
            start = pl.multiple_of((i - jj) * t, t)
            sp, logb = _sb_scores(q, k_ref[0, hh, pl.ds(start, t), :])
            w = jnp.exp(logb + _sb_later(sp, neg_upper) + carry)
            acc = acc + jnp.dot(w.astype(BF16), v_ref[0, hh, pl.ds(start, t), :],
                                preferred_element_type=F32)
            carry = carry - jnp.sum(sp, axis=1, keepdims=True)
            return jj + 1, jnp.max(carry) > LOG_WEIGHT_FLOOR, carry, acc

        carry, acc = states[hh]
        live = jnp.max(carry) > LOG_WEIGHT_FLOOR
        _, _, carry, acc = lax.while_loop(cond, body, (jnp.int32(2), live, carry, acc))
        o_ref[0, hh] = (acc * g_ref[0, hh].astype(F32)).astype(BF16)


def _sb_attention(proj, heads):
    b, _, s, d = proj.shape
    t, hb = SB_TILE, SB_HEADS_PER_STEP
    groups = heads // hb
    return pl.pallas_call(
        functools.partial(_sb_kernel, t=t, heads_per_step=hb),
        grid=(b, groups, s // t),
        in_specs=[
            pl.BlockSpec((1, hb, t, d), lambda bi, g, i: (bi, g, i, 0)),
            pl.BlockSpec((1, hb, s, d), lambda bi, g, i: (bi, groups + g, 0, 0)),
            pl.BlockSpec((1, hb, s, d), lambda bi, g, i: (bi, 2 * groups + g, 0, 0)),
            pl.BlockSpec((1, hb, t, d), lambda bi, g, i: (bi, 3 * groups + g, i, 0)),
        ],
        out_specs=pl.BlockSpec((1, hb, t, d), lambda bi, g, i: (bi, g, i, 0)),
        out_shape=jax.ShapeDtypeStruct((b, heads, s, d), BF16),
        compiler_params=pltpu.CompilerParams(
            dimension_semantics=("parallel", "parallel", "arbitrary"),
            vmem_limit_bytes=VMEM_LIMIT),
        name="sb_attn",
    )(proj, proj, proj, proj)


def _ret_kernel(q_ref, k_ref, v_ref, g_ref, din_ref, qd_ref, kd_ref, cd_ref, gain_ref,
                o_ref, state_scr, *, heads):
    ci = pl.program_id(1)

    @pl.when(ci == 0)
    def _():
        state_scr[...] = jnp.zeros_like(state_scr)

    for h in range(heads):
        qc = q_ref[0, h]
        kc = k_ref[0, h]
        vc = v_ref[0, h]
        state = state_scr[h]

        scores = lax.dot_general(qc, kc, (((1,), (1,)), ((), ())), preferred_element_type=F32)
        inner = jnp.dot((scores * din_ref[h]).astype(BF16), vc, preferred_element_type=F32)
        q_dec = (qc.astype(F32) * qd_ref[h]).astype(BF16)
        cross = jnp.dot(q_dec, state.astype(BF16), preferred_element_type=F32)
        k_dec = (kc.astype(F32) * kd_ref[h]).astype(BF16)
        kv = lax.dot_general(k_dec, vc, (((0,), (0,)), ((), ())), preferred_element_type=F32)
        state_scr[h] = state * cd_ref[h] + kv

        o = inner + cross
        mu = jnp.mean(o, axis=-1, keepdims=True)
        cen = o - mu
        var = jnp.mean(cen * cen, axis=-1, keepdims=True)
        y = cen * lax.rsqrt(var + EPS) * gain_ref[:, h * HEAD_DIM:(h + 1) * HEAD_DIM]
        o_ref[0, h] = (y * g_ref[0, h].astype(F32)).astype(BF16)


def _retention(proj, heads, din, qd, kd, cd, gain):
    b, _, s, d = proj.shape
    c = RET_CHUNK
    blk = lambda group: pl.BlockSpec((1, heads, c, d), lambda bi, ci: (bi, group, ci, 0))
    const = lambda arr: pl.BlockSpec(arr.shape, lambda bi, ci: (0,) * arr.ndim)
    return pl.pallas_call(
        functools.partial(_ret_kernel, heads=heads),
        grid=(b, s // c),
        in_specs=[blk(4), blk(5), blk(6), blk(7),
                  const(din), const(qd), const(kd), const(cd), const(gain)],
        out_specs=pl.BlockSpec((1, heads, c, d), lambda bi, ci: (bi, 0, ci, 0)),
        out_shape=jax.ShapeDtypeStruct((b, heads, s, d), BF16),
        scratch_shapes=[pltpu.VMEM((heads, d, d), F32)],
        compiler_params=pltpu.CompilerParams(
            dimension_semantics=("parallel", "arbitrary"),
            vmem_limit_bytes=VMEM_LIMIT),
        name="retention",
    )(proj, proj, proj, proj, din, qd, kd, cd, gain)


def _retention_tables(heads, c):
    log_gamma = jnp.log1p(-jnp.exp2(-5.0 - jnp.arange(heads, dtype=F32)))
    idx = jnp.arange(c, dtype=F32)
    rel = idx[:, None] - idx[None, :]
    din = jnp.where(rel[None] >= 0,
                    jnp.exp(jnp.maximum(rel, 0.0)[None] * log_gamma[:, None, None]), 0.0)
    qd = jnp.exp((idx[None, :] + 1.0) * log_gamma[:, None])[..., None]
    kd = jnp.exp((c - 1.0 - idx[None, :]) * log_gamma[:, None])[..., None]
    cd = jnp.exp(c * log_gamma)[:, None, None]
    bc = lambda t: jnp.broadcast_to(t, t.shape[:-1] + (HEAD_DIM,))
    return din, bc(qd), bc(kd), bc(cd)


def _out_kernel(a_ref, r_ref, w_ref, x_ref, mod_ref, o_ref, *, heads):
    parts = [a_ref[0, j] for j in range(heads)] + [r_ref[0, j] for j in range(heads)]
    mix = jnp.concatenate(parts, axis=-1)
    y = jnp.dot(mix, w_ref[...], preferred_element_type=F32)
    o_ref[0] = x_ref[0] + mod_ref[0, 2:3, :] * y


def _out_proj(oa, ob, w_bf16, x, mod3):
    b, s, d = x.shape
    heads = oa.shape[1]
    tm = OUT_ROWS
    return pl.pallas_call(
        functools.partial(_out_kernel, heads=heads),
        grid=(b, s // tm),
        in_specs=[
            pl.BlockSpec((1, heads, tm, HEAD_DIM), lambda bi, si: (bi, 0, si, 0)),
            pl.BlockSpec((1, heads, tm, HEAD_DIM), lambda bi, si: (bi, 0, si, 0)),
            pl.BlockSpec(w_bf16.shape, lambda bi, si: (0, 0)),
            pl.BlockSpec((1, tm, d), lambda bi, si: (bi, si, 0)),
            pl.BlockSpec((1, 3, d), lambda bi, si: (bi, 0, 0)),
        ],
        out_specs=pl.BlockSpec((1, tm, d), lambda bi, si: (bi, si, 0)),
        out_shape=jax.ShapeDtypeStruct((b, s, d), F32),
        compiler_params=pltpu.CompilerParams(
            dimension_semantics=("parallel", "parallel"),
            vmem_limit_bytes=VMEM_LIMIT),
        name="out_proj",
    )(oa, ob, w_bf16, x, mod3)


def _rotary_tables(s):
    half = HEAD_DIM // 2
    inv_freq = ROPE_BASE ** (-jnp.arange(half, dtype=F32) / half)
    ang = jnp.arange(s, dtype=F32)[:, None] * inv_freq[None, :]
    cos, sin = jnp.cos(ang), jnp.sin(ang)
    return jnp.concatenate([cos, cos], axis=-1), jnp.concatenate([-sin, sin], axis=-1)


def kernel(x, c, w_ada, b_ada, norm_gain, w_in, sb_q_gain, sb_k_gain, ret_norm_gain, w_out):
    b, s, d = x.shape
    depth = w_ada.shape[0]
    heads = w_in.shape[2] // (N_GROUPS * HEAD_DIM)
    cos_t, sin_t = _rotary_tables(s)
    din, qd, kd, cd = _retention_tables(heads, RET_CHUNK)
    c_pad = jnp.pad(c, ((0, 8 - b), (0, 0)))
    for layer in range(depth):
        mod = _adaln(c_pad, w_ada[layer], b_ada[layer][None, :])
        mod3 = mod[:b].reshape(b, 3, d)
        proj = _in_proj(x, mod3, norm_gain[layer][None, :], w_in[layer].astype(BF16),
                        sb_q_gain[layer][None, :], sb_k_gain[layer][None, :], cos_t, sin_t)
        oa = _sb_attention(proj, heads)
        ob = _retention(proj, heads, din, qd, kd, cd, ret_norm_gain[layer][None, :])
        x = _out_proj(oa, ob, w_out[layer].astype(BF16), x, mod3)
    return x
```

```python
import functools

import jax
import jax.numpy as jnp
import numpy as np
from jax import lax
from jax.experimental import pallas as pl
from jax.experimental.pallas import tpu as pltpu

HEAD_DIM = 128
ROPE_BASE = 10000.0
EPS = 1e-6
N_GROUPS = 8
F32 = jnp.float32
BF16 = jnp.bfloat16

NORM_ROWS = 1024
PROJ_ROWS = 1024
PROJ_COLS = 256
SB_Q = 128
SB_K = 256
SB_TAIL = 128
SB_STEP_ROWS = 256
SB_HEADS_PER_STEP = 4
RET_CHUNK = 256
OUT_ROWS = 512
VMEM_LIMIT = 56 * 1024 * 1024
LOG_WEIGHT_FLOOR = -120.0
LOG2E = 1.4426950408889634


def _silu(v):
    return v / (1.0 + jnp.exp(-v))


def _adaln_kernel(c_ref, w_ref, b_ref, o_ref):
    s = _silu(c_ref[...])
    o_ref[...] = jnp.dot(s, w_ref[...], preferred_element_type=F32,
                         precision=lax.Precision.HIGHEST) + b_ref[...]


def _adaln(c_pad, w, b):
    rows, d = c_pad.shape
    n = w.shape[1]
    tn = n // 4
    return pl.pallas_call(
        _adaln_kernel,
        grid=(n // tn,),
        in_specs=[pl.BlockSpec((rows, d), lambda j: (0, 0)),
                  pl.BlockSpec((d, tn), lambda j: (0, j)),
                  pl.BlockSpec((1, tn), lambda j: (0, j))],
        out_specs=pl.BlockSpec((rows, tn), lambda j: (0, j)),
        out_shape=jax.ShapeDtypeStruct((rows, n), F32),
        compiler_params=pltpu.CompilerParams(
            dimension_semantics=("arbitrary",), vmem_limit_bytes=VMEM_LIMIT),
        name="adaln",
    )(c_pad, w, b)


def _norm_kernel(x_ref, mod_ref, ng_ref, h_ref):
    x = x_ref[0]
    ms = jnp.mean(x * x, axis=-1, keepdims=True)
    y = x * lax.rsqrt(ms + EPS) * ng_ref[...]
    shift = mod_ref[0, 0:1, :]
    scale = mod_ref[0, 1:2, :]
    h_ref[0] = (y * (1.0 + scale) + shift).astype(BF16)


def _norm_mod(x, mod3, norm_gain):
    b, s, d = x.shape
    tm = NORM_ROWS
    return pl.pallas_call(
        _norm_kernel,
        grid=(b, s // tm),
        in_specs=[pl.BlockSpec((1, tm, d), lambda bi, si: (bi, si, 0)),
                  pl.BlockSpec((1, 3, d), lambda bi, si: (bi, 0, 0)),
                  pl.BlockSpec((1, d), lambda bi, si: (0, 0))],
        out_specs=pl.BlockSpec((1, tm, d), lambda bi, si: (bi, si, 0)),
        out_shape=jax.ShapeDtypeStruct((b, s, d), BF16),
        compiler_params=pltpu.CompilerParams(
            dimension_semantics=("parallel", "parallel"), vmem_limit_bytes=VMEM_LIMIT),
        name="norm_mod",
    )(x, mod3, norm_gain)


def _proj_kernel(h_ref, w_ref, *refs, kind, heads, first_mult, second_mult):
    o_ref = refs[-1]
    g = pl.program_id(2)
    mult = jnp.where(g == 0, first_mult, second_mult).astype(F32)
    if kind == "qk_norm":
        gain = refs[0][0] * mult

        def epilogue(a):
            ms = jnp.mean(a * a, axis=-1, keepdims=True)
            return a * lax.rsqrt(ms + EPS) * gain
    elif kind == "rotary":
        cos = refs[0][...] * mult
        sin = refs[1][...] * mult

        def epilogue(a):
            return a * cos + pltpu.roll(a, HEAD_DIM // 2, 1) * sin
    elif kind == "silu":
        epilogue = _silu
    else:
        epilogue = lambda a: a

    h = h_ref[0]
    heads_per_chunk = PROJ_COLS // HEAD_DIM
    for c in range(heads // heads_per_chunk):
        acc = jnp.dot(h, w_ref[:, c * PROJ_COLS:(c + 1) * PROJ_COLS], preferred_element_type=F32)
        for j in range(heads_per_chunk):
            a = acc[:, j * HEAD_DIM:(j + 1) * HEAD_DIM]
            o_ref[0, c * heads_per_chunk + j] = epilogue(a).astype(BF16)


def _in_proj(h, w_bf16, kind, first_group, group_stride, extra=(), extra_specs=(),
             first_mult=1.0, second_mult=1.0):
    b, s, d = h.shape
    tn = w_bf16.shape[1] // N_GROUPS
    heads = tn // HEAD_DIM
    tm = PROJ_ROWS
    return pl.pallas_call(
        functools.partial(_proj_kernel, kind=kind, heads=heads,
                          first_mult=first_mult, second_mult=second_mult),
        grid=(b, s // tm, 2),
        in_specs=[
            pl.BlockSpec((1, tm, d), lambda bi, si, g: (bi, si, 0)),
            pl.BlockSpec((d, tn), lambda bi, si, g: (0, first_group + group_stride * g)),
            *extra_specs,
        ],
        out_specs=pl.BlockSpec((1, heads, tm, HEAD_DIM), lambda bi, si, g: (bi, g, si, 0)),
        out_shape=jax.ShapeDtypeStruct((b, 2 * heads, s, HEAD_DIM), BF16),
        compiler_params=pltpu.CompilerParams(
            dimension_semantics=("parallel", "parallel", "arbitrary"),
            vmem_limit_bytes=VMEM_LIMIT),
        name="in_proj_" + kind,
    )(h, w_bf16, *extra)


def _sb_scores(q, k_blk):
    z = lax.dot_general(q, k_blk, (((1,), (1,)), ((), ())), preferred_element_type=F32)
    sp = jnp.maximum(z, 0.0) + jnp.log(1.0 + jnp.exp2(jnp.abs(z) * (-LOG2E)))
    return sp, z - sp


def _sb_later(sp, neg_upper):
    return jnp.dot(sp.astype(BF16), neg_upper, preferred_element_type=F32)


def _sb_edge_window(q, k_ref, v_ref, hh, q0, neg_upper):
    row = lax.broadcasted_iota(jnp.int32, (SB_Q, SB_K), 0)
    col = lax.broadcasted_iota(jnp.int32, (SB_Q, SB_K), 1)
    d_start = jnp.maximum(q0 - SB_Q, 0)
    p_start = jnp.maximum(d_start - SB_K, 0)
    mask_d = col < row + (q0 - d_start)
    mask_p = col < (d_start - p_start)
    d_start = pl.multiple_of(d_start, SB_Q)
    p_start = pl.multiple_of(p_start, SB_Q)

    sp_d, logb_d = _sb_scores(q, k_ref[0, hh, pl.ds(d_start, SB_K), :])
    sp_p, logb_p = _sb_scores(q, k_ref[0, hh, pl.ds(p_start, SB_K), :])
    sp_d = jnp.where(mask_d, sp_d, 0.0)
    sp_p = jnp.where(mask_p, sp_p, 0.0)
    w_d = jnp.where(mask_d, jnp.exp(logb_d + _sb_later(sp_d, neg_upper)), 0.0)
    carry = -jnp.sum(sp_d, axis=1, keepdims=True)
    w_p = jnp.where(mask_p, jnp.exp(logb_p + _sb_later(sp_p, neg_upper) + carry), 0.0)
    acc = (jnp.dot(w_d.astype(BF16), v_ref[0, hh, pl.ds(d_start, SB_K), :], preferred_element_type=F32)
           + jnp.dot(w_p.astype(BF16), v_ref[0, hh, pl.ds(p_start, SB_K), :], preferred_element_type=F32))
    carry = carry - jnp.sum(sp_p, axis=1, keepdims=True)
    return carry, acc, p_start


def _sb_interior_windows(q_ref, k_ref, v_ref, step_start, neg_upper, heads_per_step):
    subs = SB_STEP_ROWS // SB_Q
    win = 2 * SB_K
    span = win + (subs - 1) * SB_Q
    w_start = pl.multiple_of(step_start - (win - SB_Q), SB_Q)
    zs = []
    for hh in range(heads_per_step):
        z = lax.dot_general(q_ref[0, hh], k_ref[0, hh, pl.ds(w_start, span), :],
                            (((1,), (1,)), ((), ())), preferred_element_type=F32)
        zs += [z[sub * SB_Q:(sub + 1) * SB_Q, sub * SB_Q:sub * SB_Q + win] for sub in range(subs)]
    z = jnp.concatenate(zs, axis=0)
    n = z.shape[0]
    sp = jnp.maximum(z, 0.0) + jnp.log(1.0 + jnp.exp2(jnp.abs(z) * (-LOG2E)))
    logb = z - sp
    row = lax.broadcasted_iota(jnp.int32, (n, SB_Q), 0) & (SB_Q - 1)
    col = lax.broadcasted_iota(jnp.int32, (n, SB_Q), 1)
    past = col < row
    diag = lambda t: jnp.concatenate([t[:, :SB_K - SB_Q], jnp.where(past, t[:, SB_K - SB_Q:], 0.0)],
                                     axis=1)
    sp_p, sp_d = sp[:, :SB_K], diag(sp[:, SB_K:])
    w_d = diag(jnp.exp(logb[:, SB_K:] + _sb_later(sp_d, neg_upper)))
    carry = -jnp.sum(sp_d, axis=1, keepdims=True)
    w_p = jnp.exp(logb[:, :SB_K] + _sb_later(sp_p, neg_upper) + carry)
    carry = carry - jnp.sum(sp_p, axis=1, keepdims=True)
    w = jnp.concatenate([w_p, w_d], axis=1).astype(BF16)
    carries, accs = [], []
    for hh in range(heads_per_step):
        for sub in range(subs):
            c = hh * subs + sub
            v_win = v_ref[0, hh, pl.ds(pl.multiple_of(w_start + sub * SB_Q, SB_Q), win), :]
            accs.append(jnp.dot(w[c * SB_Q:(c + 1) * SB_Q], v_win, preferred_element_type=F32))
            carries.append(carry[c * SB_Q:(c + 1) * SB_Q])
    return carries, accs, w_start


def _sb_kernel(q_ref, k_ref, v_ref, g_ref, o_ref, *, heads_per_step):
    i = pl.program_id(2)
    urow = lax.broadcasted_iota(jnp.int32, (SB_K, SB_K), 0)
    ucol = lax.broadcasted_iota(jnp.int32, (SB_K, SB_K), 1)
    neg_upper = jnp.where(urow > ucol, -1.0, 0.0).astype(BF16)
    neg_upper_tail = neg_upper[:SB_TAIL, :SB_TAIL]
    subs = SB_STEP_ROWS // SB_Q

    def run(edge):
        step_start = i * SB_STEP_ROWS
        if not edge:
            carries, accs, w_start = _sb_interior_windows(
                q_ref, k_ref, v_ref, step_start, neg_upper, heads_per_step)
        chains = []
        for hh in range(heads_per_step):
            for sub in range(subs):
                rows = slice(sub * SB_Q, (sub + 1) * SB_Q)
                q = q_ref[0, hh, rows, :]
                if edge:
                    carry, acc, p_start = _sb_edge_window(
                        q, k_ref, v_ref, hh, step_start + sub * SB_Q, neg_upper)
                else:
                    carry, acc = carries[hh * subs + sub], accs[hh * subs + sub]
                    p_start = w_start + sub * SB_Q
                chains.append((hh, rows, q, carry, acc, p_start,
                               jnp.max(carry) > LOG_WEIGHT_FLOOR))

        for hh, rows, q, carry, acc, p_start, live in chains:
            tiles_left = p_start // SB_TAIL

            def cond(state, tiles_left=tiles_left):
                jj, live, _, _ = state
                return (jj < tiles_left) & live

            def body(state, hh=hh, q=q, p_start=p_start):
                jj, _, carry, acc = state
                start = pl.multiple_of(p_start - (jj + 1) * SB_TAIL, SB_TAIL)
                sp, logb = _sb_scores(q, k_ref[0, hh, pl.ds(start, SB_TAIL), :])
                w = jnp.exp(logb + _sb_later(sp, neg_upper_tail) + carry)
                acc = acc + jnp.dot(w.astype(BF16), v_ref[0, hh, pl.ds(start, SB_TAIL), :],
                                    preferred_element_type=F32)
                carry = carry - jnp.sum(sp, axis=1, keepdims=True)
                return jj + 1, jnp.max(carry) > LOG_WEIGHT_FLOOR, carry, acc

            _, _, _, acc = lax.while_loop(cond, body, (jnp.int32(0), live, carry, acc))
            o_ref[0, hh, rows, :] = (acc * g_ref[0, hh, rows, :].astype(F32)).astype(BF16)

    first_interior = (SB_Q + SB_K + SB_STEP_ROWS - 1) // SB_STEP_ROWS
    pl.when(i >= first_interior)(lambda: run(False))
    pl.when(i < first_interior)(lambda: run(True))


def _sb_attention(qk, v, gate, heads):
    b, _, s, d = qk.shape
    hb = SB_HEADS_PER_STEP
    groups = heads // hb
    t = SB_STEP_ROWS
    return pl.pallas_call(
        functools.partial(_sb_kernel, heads_per_step=hb),
        grid=(b, groups, s // t),
        in_specs=[
            pl.BlockSpec((1, hb, t, d), lambda bi, g, i: (bi, g, i, 0)),
            pl.BlockSpec((1, hb, s, d), lambda bi, g, i: (bi, groups + g, 0, 0)),
            pl.BlockSpec((1, hb, s, d), lambda bi, g, i: (bi, g, 0, 0)),
            pl.BlockSpec((1, hb, t, d), lambda bi, g, i: (bi, g, i, 0)),
        ],
        out_specs=pl.BlockSpec((1, hb, t, d), lambda bi, g, i: (bi, g, i, 0)),
        out_shape=jax.ShapeDtypeStruct((b, heads, s, d), BF16),
        compiler_params=pltpu.CompilerParams(
            dimension_semantics=("parallel", "parallel", "arbitrary"),
            vmem_limit_bytes=VMEM_LIMIT),
        name="sb_attn",
    )(qk, qk, v, gate)


def _ret_kernel(q_ref, k_ref, v_ref, g_ref, din_ref, qd_ref, kd_ref, cd_ref, gain_ref,
                o_ref, state_scr, *, heads):
    ci = pl.program_id(1)

    @pl.when(ci == 0)
    def _():
        state_scr[...] = jnp.zeros_like(state_scr)

    for h in range(heads):
        qc = q_ref[0, h]
        kc = k_ref[0, h]
        vc = v_ref[0, h]
        state = state_scr[h]

        scores = lax.dot_general(qc, kc, (((1,), (1,)), ((), ())), preferred_element_type=F32)
        inner = jnp.dot((scores * din_ref[h]).astype(BF16), vc, preferred_element_type=F32)
        q_dec = (qc.astype(F32) * qd_ref[h]).astype(BF16)
        cross = jnp.dot(q_dec, state.astype(BF16), preferred_element_type=F32)
        k_dec = (kc.astype(F32) * kd_ref[h]).astype(BF16)
        kv = lax.dot_general(k_dec, vc, (((0,), (0,)), ((), ())), preferred_element_type=F32)
        state_scr[h] = state * cd_ref[h] + kv

        o = inner + cross
        mu = jnp.mean(o, axis=-1, keepdims=True)
        cen = o - mu
        var = jnp.mean(cen * cen, axis=-1, keepdims=True)
        y = cen * lax.rsqrt(var + EPS) * gain_ref[:, h * HEAD_DIM:(h + 1) * HEAD_DIM]
        o_ref[0, h] = (y * g_ref[0, h].astype(F32)).astype(BF16)


def _retention(qk, v, gate, heads, din, qd, kd, cd, gain):
    b, _, s, d = qk.shape
    c = RET_CHUNK
    blk = lambda group: pl.BlockSpec((1, heads, c, d), lambda bi, ci: (bi, group, ci, 0))
    const = lambda arr: pl.BlockSpec(arr.shape, lambda bi, ci: (0,) * arr.ndim)
    return pl.pallas_call(
        functools.partial(_ret_kernel, heads=heads),
        grid=(b, s // c),
        in_specs=[blk(0), blk(1), blk(1), blk(1),
                  const(din), const(qd), const(kd), const(cd), const(gain)],
        out_specs=pl.BlockSpec((1, heads, c, d), lambda bi, ci: (bi, 0, ci, 0)),
        out_shape=jax.ShapeDtypeStruct((b, heads, s, d), BF16),
        scratch_shapes=[pltpu.VMEM((heads, d, d), F32)],
        compiler_params=pltpu.CompilerParams(
            dimension_semantics=("parallel", "arbitrary"),
            vmem_limit_bytes=VMEM_LIMIT),
        name="retention",
    )(qk, qk, v, gate, din, qd, kd, cd, gain)


def _retention_tables(heads, c):
    log_gamma = jnp.log1p(-jnp.exp2(-5.0 - jnp.arange(heads, dtype=F32)))
    idx = jnp.arange(c, dtype=F32)
    rel = idx[:, None] - idx[None, :]
    din = jnp.where(rel[None] >= 0,
                    jnp.exp(jnp.maximum(rel, 0.0)[None] * log_gamma[:, None, None]), 0.0)
    qd = jnp.exp((idx[None, :] + 1.0) * log_gamma[:, None])[..., None]
    kd = jnp.exp((c - 1.0 - idx[None, :]) * log_gamma[:, None])[..., None]
    cd = jnp.exp(c * log_gamma)[:, None, None]
    bc = lambda t: jnp.broadcast_to(t, t.shape[:-1] + (HEAD_DIM,))
    return din, bc(qd), bc(kd), bc(cd)


def _out_kernel(a_ref, r_ref, w_ref, x_ref, mod_ref, o_ref, *, heads):
    parts = [a_ref[0, j] for j in range(heads)] + [r_ref[0, j] for j in range(heads)]
    mix = jnp.concatenate(parts, axis=-1)
    y = jnp.dot(mix, w_ref[...], preferred_element_type=F32)
    o_ref[0] = x_ref[0] + mod_ref[0, 2:3, :] * y


def _out_proj(oa, ob, w_bf16, x, mod3):
    b, s, d = x.shape
    heads = oa.shape[1]
    tm = OUT_ROWS
    return pl.pallas_call(
        functools.partial(_out_kernel, heads=heads),
        grid=(b, s // tm),
        in_specs=[
            pl.BlockSpec((1, heads, tm, HEAD_DIM), lambda bi, si: (bi, 0, si, 0)),
            pl.BlockSpec((1, heads, tm, HEAD_DIM), lambda bi, si: (bi, 0, si, 0)),
            pl.BlockSpec(w_bf16.shape, lambda bi, si: (0, 0)),
            pl.BlockSpec((1, tm, d), lambda bi, si: (bi, si, 0)),
            pl.BlockSpec((1, 3, d), lambda bi, si: (bi, 0, 0)),
        ],
        out_specs=pl.BlockSpec((1, tm, d), lambda bi, si: (bi, si, 0)),
        out_shape=jax.ShapeDtypeStruct((b, s, d), F32),
        compiler_params=pltpu.CompilerParams(
            dimension_semantics=("parallel", "parallel"),
            vmem_limit_bytes=VMEM_LIMIT),
        name="out_proj",
    )(oa, ob, w_bf16, x, mod3)


def _rotary_tables(s):
    half = HEAD_DIM // 2
    inv_freq = ROPE_BASE ** (-jnp.arange(half, dtype=F32) / half)
    ang = jnp.arange(s, dtype=F32)[:, None] * inv_freq[None, :]
    cos, sin = jnp.cos(ang), jnp.sin(ang)
    return jnp.concatenate([cos, cos], axis=-1), jnp.concatenate([-sin, sin], axis=-1)


def kernel(x, c, w_ada, b_ada, norm_gain, w_in, sb_q_gain, sb_k_gain, ret_norm_gain, w_out):
    b, s, d = x.shape
    depth = w_ada.shape[0]
    heads = w_in.shape[2] // (N_GROUPS * HEAD_DIM)
    cos_t, sin_t = _rotary_tables(s)
    din, qd, kd, cd = _retention_tables(heads, RET_CHUNK)
    c_pad = jnp.pad(c, ((0, 8 - b), (0, 0)))
    rope_spec = pl.BlockSpec((PROJ_ROWS, HEAD_DIM), lambda bi, si, g: (si, 0))
    gain_spec = pl.BlockSpec((1, 1, HEAD_DIM), lambda bi, si, g: (g, 0, 0))
    for layer in range(depth):
        mod = _adaln(c_pad, w_ada[layer], b_ada[layer][None, :])
        mod3 = mod[:b].reshape(b, 3, d)
        h = _norm_mod(x, mod3, norm_gain[layer][None, :])
        w = w_in[layer].astype(BF16)
        qk_gain = jnp.stack([sb_q_gain[layer], sb_k_gain[layer]])[:, None, :]
        sb_qk = _in_proj(h, w, "qk_norm", 0, 1, (qk_gain,), (gain_spec,),
                         first_mult=float(1.0 / np.sqrt(HEAD_DIM)))
        vals = _in_proj(h, w, "identity", 2, 4)
        gates = _in_proj(h, w, "silu", 3, 4)
        ret_qk = _in_proj(h, w, "rotary", 4, 1, (cos_t, sin_t), (rope_spec, rope_spec),
                          second_mult=float(HEAD_DIM ** -0.5))
        oa = _sb_attention(sb_qk, vals, gates, heads)
        ob = _retention(ret_qk, vals, gates, heads, din, qd, kd, cd, ret_norm_gain[layer][None, :])
        x = _out_proj(oa, ob, w_out[layer].astype(BF16), x, mod3)
    return x
```

```python
import functools

import jax
import jax.numpy as jnp
import numpy as np
from jax import lax
from jax.experimental import pallas as pl
from jax.experimental.pallas import tpu as pltpu

HEAD_DIM = 128
ROPE_BASE = 10000.0
EPS = 1e-6
N_GROUPS = 8
F32 = jnp.float32
BF16 = jnp.bfloat16

ADALN_STEPS = 16
NORM_ROWS = 1024
NORM_CHUNK = 32
NORM_UNROLL = 8
PROJ_ROWS = 2048
PROJ_COLS = 256
SB_Q = 128
SB_K = 256
SB_TAIL = 128
SB_STEP_ROWS = 256
SB_HEADS_PER_STEP = 4
RET_CHUNK = 256
OUT_ROWS = 512
VMEM_LIMIT = 56 * 1024 * 1024
LOG_WEIGHT_FLOOR = -120.0
LOG2E = 1.4426950408889634


def _silu(v):
    return v / (1.0 + jnp.exp(-v))


def _adaln_kernel(c_ref, w_ref, b_ref, o_ref):
    s = _silu(c_ref[...])
    o_ref[...] = jnp.dot(s, w_ref[...], preferred_element_type=F32,
                         precision=lax.Precision.HIGHEST) + b_ref[...]


def _adaln(c_pad, w, b):
    rows, d = c_pad.shape
    n = w.shape[1]
    tn = n // ADALN_STEPS
    return pl.pallas_call(
        _adaln_kernel,
        grid=(n // tn,),
        in_specs=[pl.BlockSpec((rows, d), lambda j: (0, 0)),
                  pl.BlockSpec((d, tn), lambda j: (0, j)),
                  pl.BlockSpec((1, tn), lambda j: (0, j))],
        out_specs=pl.BlockSpec((rows, tn), lambda j: (0, j)),
        out_shape=jax.ShapeDtypeStruct((rows, n), F32),
        compiler_params=pltpu.CompilerParams(
            dimension_semantics=("arbitrary",), vmem_limit_bytes=VMEM_LIMIT),
        name="adaln",
    )(c_pad, w, b)


def _norm_kernel(x_ref, mod_ref, ng_ref, h_ref):
    shift = mod_ref[0, 0:1, :]
    gain = ng_ref[...] * (1.0 + mod_ref[0, 1:2, :])

    def body(r, _):
        rows = pl.ds(pl.multiple_of(r * NORM_CHUNK, NORM_CHUNK), NORM_CHUNK)
        x = x_ref[0, rows, :]
        ms = jnp.mean(x * x, axis=-1, keepdims=True)
        h_ref[0, rows, :] = (x * lax.rsqrt(ms + EPS) * gain + shift).astype(BF16)
        return 0

    lax.fori_loop(0, x_ref.shape[1] // NORM_CHUNK, body, 0, unroll=NORM_UNROLL)


def _norm_mod(x, mod3, norm_gain):
    b, s, d = x.shape
    tm = NORM_ROWS
    return pl.pallas_call(
        _norm_kernel,
        grid=(b, s // tm),
        in_specs=[pl.BlockSpec((1, tm, d), lambda bi, si: (bi, si, 0)),
                  pl.BlockSpec((1, 3, d), lambda bi, si: (bi, 0, 0)),
                  pl.BlockSpec((1, d), lambda bi, si: (0, 0))],
        out_specs=pl.BlockSpec((1, tm, d), lambda bi, si: (bi, si, 0)),
        out_shape=jax.ShapeDtypeStruct((b, s, d), BF16),
        compiler_params=pltpu.CompilerParams(
            dimension_semantics=("parallel", "parallel"), vmem_limit_bytes=VMEM_LIMIT),
        name="norm_mod",
    )(x, mod3, norm_gain)


def _proj_kernel(h_ref, w_ref, *refs, kind, heads, first_mult, second_mult):
    o_ref = refs[-1]
    g = pl.program_id(2)
    mult = jnp.where(g == 0, first_mult, second_mult).astype(F32)
    if kind == "qk_norm":
        gain = refs[0][0] * mult

        def epilogue(a):
            ms = jnp.mean(a * a, axis=-1, keepdims=True)
            return a * lax.rsqrt(ms + EPS) * gain
    elif kind == "rotary":
        cos = refs[0][...] * mult
        sin = refs[1][...] * mult

        def epilogue(a):
            return a * cos + pltpu.roll(a, HEAD_DIM // 2, 1) * sin
    elif kind == "silu":
        epilogue = _silu
    else:
        epilogue = lambda a: a

    h = h_ref[0]
    heads_per_chunk = PROJ_COLS // HEAD_DIM
    for c in range(heads // heads_per_chunk):
        acc = jnp.dot(h, w_ref[:, c * PROJ_COLS:(c + 1) * PROJ_COLS], preferred_element_type=F32)
        for j in range(heads_per_chunk):
            a = acc[:, j * HEAD_DIM:(j + 1) * HEAD_DIM]
            o_ref[0, c * heads_per_chunk + j] = epilogue(a).astype(BF16)


def _in_proj(h, w_bf16, kind, first_group, group_stride, extra=(), extra_specs=(),
             first_mult=1.0, second_mult=1.0):
    b, s, d = h.shape
    tn = w_bf16.shape[1] // N_GROUPS
    heads = tn // HEAD_DIM
    tm = PROJ_ROWS
    return pl.pallas_call(
        functools.partial(_proj_kernel, kind=kind, heads=heads,
                          first_mult=first_mult, second_mult=second_mult),
        grid=(b, s // tm, 2),
        in_specs=[
            pl.BlockSpec((1, tm, d), lambda bi, si, g: (bi, si, 0)),
            pl.BlockSpec((d, tn), lambda bi, si, g: (0, first_group + group_stride * g)),
            *extra_specs,
        ],
        out_specs=pl.BlockSpec((1, heads, tm, HEAD_DIM), lambda bi, si, g: (bi, g, si, 0)),
        out_shape=jax.ShapeDtypeStruct((b, 2 * heads, s, HEAD_DIM), BF16),
        compiler_params=pltpu.CompilerParams(
            dimension_semantics=("parallel", "parallel", "arbitrary"),
            vmem_limit_bytes=VMEM_LIMIT),
        name="in_proj_" + kind,
    )(h, w_bf16, *extra)


def _sb_scores(q, k_blk):
    z = lax.dot_general(q, k_blk, (((1,), (1,)), ((), ())), preferred_element_type=F32)
    sp = jnp.maximum(z, 0.0) + jnp.log(1.0 + jnp.exp2(jnp.abs(z) * (-LOG2E)))
    return sp, z - sp


def _sb_later(sp, neg_upper):
    return jnp.dot(sp.astype(BF16), neg_upper, preferred_element_type=F32)


def _sb_edge_window(q, k_ref, v_ref, hh, q0, neg_upper):
    row = lax.broadcasted_iota(jnp.int32, (SB_Q, SB_K), 0)
    col = lax.broadcasted_iota(jnp.int32, (SB_Q, SB_K), 1)
    d_start = jnp.maximum(q0 - SB_Q, 0)
    p_start = jnp.maximum(d_start - SB_K, 0)
    mask_d = col < row + (q0 - d_start)
    mask_p = col < (d_start - p_start)
    d_start = pl.multiple_of(d_start, SB_Q)
    p_start = pl.multiple_of(p_start, SB_Q)

    sp_d, logb_d = _sb_scores(q, k_ref[0, hh, pl.ds(d_start, SB_K), :])
    sp_p, logb_p = _sb_scores(q, k_ref[0, hh, pl.ds(p_start, SB_K), :])
    sp_d = jnp.where(mask_d, sp_d, 0.0)
    sp_p = jnp.where(mask_p, sp_p, 0.0)
    w_d = jnp.where(mask_d, jnp.exp(logb_d + _sb_later(sp_d, neg_upper)), 0.0)
    carry = -jnp.sum(sp_d, axis=1, keepdims=True)
    w_p = jnp.where(mask_p, jnp.exp(logb_p + _sb_later(sp_p, neg_upper) + carry), 0.0)
    acc = (jnp.dot(w_d.astype(BF16), v_ref[0, hh, pl.ds(d_start, SB_K), :], preferred_element_type=F32)
           + jnp.dot(w_p.astype(BF16), v_ref[0, hh, pl.ds(p_start, SB_K), :], preferred_element_type=F32))
    carry = carry - jnp.sum(sp_p, axis=1, keepdims=True)
    return carry, acc, p_start


def _sb_interior_windows(q_ref, k_ref, v_ref, step_start, neg_upper, heads_per_step):
    subs = SB_STEP_ROWS // SB_Q
    win = 2 * SB_K
    span = win + (subs - 1) * SB_Q
    w_start = pl.multiple_of(step_start - (win - SB_Q), SB_Q)
    zs = []
    for hh in range(heads_per_step):
        z = lax.dot_general(q_ref[0, hh], k_ref[0, hh, pl.ds(w_start, span), :],
                            (((1,), (1,)), ((), ())), preferred_element_type=F32)
        zs += [z[sub * SB_Q:(sub + 1) * SB_Q, sub * SB_Q:sub * SB_Q + win] for sub in range(subs)]
    z = jnp.concatenate(zs, axis=0)
    n = z.shape[0]
    sp = jnp.maximum(z, 0.0) + jnp.log(1.0 + jnp.exp2(jnp.abs(z) * (-LOG2E)))
    logb = z - sp
    row = lax.broadcasted_iota(jnp.int32, (n, SB_Q), 0) & (SB_Q - 1)
    col = lax.broadcasted_iota(jnp.int32, (n, SB_Q), 1)
    past = col < row
    diag = lambda t: jnp.concatenate([t[:, :SB_K - SB_Q], jnp.where(past, t[:, SB_K - SB_Q:], 0.0)],
                                     axis=1)
    sp_p, sp_d = sp[:, :SB_K], diag(sp[:, SB_K:])
    w_d = diag(jnp.exp(logb[:, SB_K:] + _sb_later(sp_d, neg_upper)))
    carry = -jnp.sum(sp_d, axis=1, keepdims=True)
    w_p = jnp.exp(logb[:, :SB_K] + _sb_later(sp_p, neg_upper) + carry)
    carry = carry - jnp.sum(sp_p, axis=1, keepdims=True)
    w = jnp.concatenate([w_p, w_d], axis=1).astype(BF16)
    carries, accs = [], []
    for hh in range(heads_per_step):
        for sub in range(subs):
            c = hh * subs + sub
            v_win = v_ref[0, hh, pl.ds(pl.multiple_of(w_start + sub * SB_Q, SB_Q), win), :]
            accs.append(jnp.dot(w[c * SB_Q:(c + 1) * SB_Q], v_win, preferred_element_type=F32))
            carries.append(carry[c * SB_Q:(c + 1) * SB_Q])
    return carries, accs, w_start


def _sb_kernel(q_ref, k_ref, v_ref, g_ref, o_ref, *, heads_per_step):
    i = pl.program_id(2)
    urow = lax.broadcasted_iota(jnp.int32, (SB_K, SB_K), 0)
    ucol = lax.broadcasted_iota(jnp.int32, (SB_K, SB_K), 1)
    neg_upper = jnp.where(urow > ucol, -1.0, 0.0).astype(BF16)
    neg_upper_tail = neg_upper[:SB_TAIL, :SB_TAIL]
    subs = SB_STEP_ROWS // SB_Q

    def run(edge):
        step_start = i * SB_STEP_ROWS
        if not edge:
            carries, accs, w_start = _sb_interior_windows(
                q_ref, k_ref, v_ref, step_start, neg_upper, heads_per_step)
        chains = []
        for hh in range(heads_per_step):
            for sub in range(subs):
                rows = slice(sub * SB_Q, (sub + 1) * SB_Q)
                q = q_ref[0, hh, rows, :]
                if edge:
                    carry, acc, p_start = _sb_edge_window(
                        q, k_ref, v_ref, hh, step_start + sub * SB_Q, neg_upper)
                else:
                    carry, acc = carries[hh * subs + sub], accs[hh * subs + sub]
                    p_start = w_start + sub * SB_Q
                chains.append((hh, rows, q, carry, acc, p_start,
                               jnp.max(carry) > LOG_WEIGHT_FLOOR))

        for hh, rows, q, carry, acc, p_start, live in chains:
            tiles_left = p_start // SB_TAIL

            def cond(state, tiles_left=tiles_left):
                jj, live, _, _ = state
                return (jj < tiles_left) & live

            def body(state, hh=hh, q=q, p_start=p_start):
                jj, _, carry, acc = state
                start = pl.multiple_of(p_start - (jj + 1) * SB_TAIL, SB_TAIL)
                sp, logb = _sb_scores(q, k_ref[0, hh, pl.ds(start, SB_TAIL), :])
                w = jnp.exp(logb + _sb_later(sp, neg_upper_tail) + carry)
                acc = acc + jnp.dot(w.astype(BF16), v_ref[0, hh, pl.ds(start, SB_TAIL), :],
                                    preferred_element_type=F32)
                carry = carry - jnp.sum(sp, axis=1, keepdims=True)
                return jj + 1, jnp.max(carry) > LOG_WEIGHT_FLOOR, carry, acc

            _, _, _, acc = lax.while_loop(cond, body, (jnp.int32(0), live, carry, acc))
            o_ref[0, hh, rows, :] = (acc * g_ref[0, hh, rows, :].astype(F32)).astype(BF16)

    first_interior = (SB_Q + SB_K + SB_STEP_ROWS - 1) // SB_STEP_ROWS
    pl.when(i >= first_interior)(lambda: run(False))
    pl.when(i < first_interior)(lambda: run(True))


def _sb_attention(qk, v, gate, heads):
    b, _, s, d = qk.shape
    hb = SB_HEADS_PER_STEP
    groups = heads // hb
    t = SB_STEP_ROWS
    return pl.pallas_call(
        functools.partial(_sb_kernel, heads_per_step=hb),
        grid=(b, groups, s // t),
        in_specs=[
            pl.BlockSpec((1, hb, t, d), lambda bi, g, i: (bi, g, i, 0)),
            pl.BlockSpec((1, hb, s, d), lambda bi, g, i: (bi, groups + g, 0, 0)),
            pl.BlockSpec((1, hb, s, d), lambda bi, g, i: (bi, g, 0, 0)),
            pl.BlockSpec((1, hb, t, d), lambda bi, g, i: (bi, g, i, 0)),
        ],
        out_specs=pl.BlockSpec((1, hb, t, d), lambda bi, g, i: (bi, g, i, 0)),
        out_shape=jax.ShapeDtypeStruct((b, heads, s, d), BF16),
        compiler_params=pltpu.CompilerParams(
            dimension_semantics=("parallel", "parallel", "arbitrary"),
            vmem_limit_bytes=VMEM_LIMIT),
        name="sb_attn",
    )(qk, qk, v, gate)


def _ret_kernel(q_ref, k_ref, v_ref, g_ref, din_ref, qd_ref, kd_ref, cd_ref, gain_ref,
                o_ref, state_scr, *, heads):
    ci = pl.program_id(1)

    @pl.when(ci == 0)
    def _():
        state_scr[...] = jnp.zeros_like(state_scr)

    for h in range(heads):
        qc = q_ref[0, h]
        kc = k_ref[0, h]
        vc = v_ref[0, h]
        state = state_scr[h]

        scores = lax.dot_general(qc, kc, (((1,), (1,)), ((), ())), preferred_element_type=F32)
        inner = jnp.dot((scores * din_ref[h]).astype(BF16), vc, preferred_element_type=F32)
        q_dec = (qc.astype(F32) * qd_ref[h]).astype(BF16)
        cross = jnp.dot(q_dec, state.astype(BF16), preferred_element_type=F32)
        k_dec = (kc.astype(F32) * kd_ref[h]).astype(BF16)
        kv = lax.dot_general(k_dec, vc, (((0,), (0,)), ((), ())), preferred_element_type=F32)
        state_scr[h] = state * cd_ref[h] + kv

        o = inner + cross
        mu = jnp.mean(o, axis=-1, keepdims=True)
        cen = o - mu
        var = jnp.mean(cen * cen, axis=-1, keepdims=True)
        y = cen * lax.rsqrt(var + EPS) * gain_ref[:, h * HEAD_DIM:(h + 1) * HEAD_DIM]
        o_ref[0, h] = (y * g_ref[0, h].astype(F32)).astype(BF16)


def _retention(qk, v, gate, heads, din, qd, kd, cd, gain):
    b, _, s, d = qk.shape
    c = RET_CHUNK
    blk = lambda group: pl.BlockSpec((1, heads, c, d), lambda bi, ci: (bi, group, ci, 0))
    const = lambda arr: pl.BlockSpec(arr.shape, lambda bi, ci: (0,) * arr.ndim)
    return pl.pallas_call(
        functools.partial(_ret_kernel, heads=heads),
        grid=(b, s // c),
        in_specs=[blk(0), blk(1), blk(1), blk(1),
                  const(din), const(qd), const(kd), const(cd), const(gain)],
        out_specs=pl.BlockSpec((1, heads, c, d), lambda bi, ci: (bi, 0, ci, 0)),
        out_shape=jax.ShapeDtypeStruct((b, heads, s, d), BF16),
        scratch_shapes=[pltpu.VMEM((heads, d, d), F32)],
        compiler_params=pltpu.CompilerParams(
            dimension_semantics=("parallel", "arbitrary"),
            vmem_limit_bytes=VMEM_LIMIT),
        name="retention",
    )(qk, qk, v, gate, din, qd, kd, cd, gain)


def _retention_tables(heads, c):
    log_gamma = np.log1p(-np.exp2(-5.0 - np.arange(heads, dtype=np.float64)))
    idx = np.arange(c, dtype=np.float64)
    rel = idx[:, None] - idx[None, :]
    din = np.where(rel[None] >= 0, np.exp(np.maximum(rel, 0.0)[None] * log_gamma[:, None, None]), 0.0)
    qd = np.exp((idx[None, :] + 1.0) * log_gamma[:, None])[..., None]
    kd = np.exp((c - 1.0 - idx[None, :]) * log_gamma[:, None])[..., None]
    cd = np.exp(c * log_gamma)[:, None, None]
    bc = lambda t: np.ascontiguousarray(
        np.broadcast_to(t, t.shape[:-1] + (HEAD_DIM,)), dtype=np.float32)
    return din.astype(np.float32), bc(qd), bc(kd), bc(cd)


def _out_kernel(a_ref, r_ref, w_ref, x_ref, mod_ref, o_ref, *, heads):
    parts = [a_ref[0, j] for j in range(heads)] + [r_ref[0, j] for j in range(heads)]
    mix = jnp.concatenate(parts, axis=-1)
    y = jnp.dot(mix, w_ref[...], preferred_element_type=F32)
    o_ref[0] = x_ref[0] + mod_ref[0, 2:3, :] * y


def _out_proj(oa, ob, w_bf16, x, mod3):
    b, s, d = x.shape
    heads = oa.shape[1]
    tm = OUT_ROWS
    return pl.pallas_call(
        functools.partial(_out_kernel, heads=heads),
        grid=(b, s // tm),
        in_specs=[
            pl.BlockSpec((1, heads, tm, HEAD_DIM), lambda bi, si: (bi, 0, si, 0)),
            pl.BlockSpec((1, heads, tm, HEAD_DIM), lambda bi, si: (bi, 0, si, 0)),
            pl.BlockSpec(w_bf16.shape, lambda bi, si: (0, 0)),
            pl.BlockSpec((1, tm, d), lambda bi, si: (bi, si, 0)),
            pl.BlockSpec((1, 3, d), lambda bi, si: (bi, 0, 0)),
        ],
        out_specs=pl.BlockSpec((1, tm, d), lambda bi, si: (bi, si, 0)),
        out_shape=jax.ShapeDtypeStruct((b, s, d), F32),
        compiler_params=pltpu.CompilerParams(
            dimension_semantics=("parallel", "parallel"),
            vmem_limit_bytes=VMEM_LIMIT),
        name="out_proj",
    )(oa, ob, w_bf16, x, mod3)


def _rotary_tables(s):
    half = HEAD_DIM // 2
    inv_freq = ROPE_BASE ** (-np.arange(half, dtype=np.float64) / half)
    ang = np.arange(s, dtype=np.float64)[:, None] * inv_freq[None, :]
    cos, sin = np.cos(ang), np.sin(ang)
    return (np.concatenate([cos, cos], axis=-1).astype(np.float32),
            np.concatenate([-sin, sin], axis=-1).astype(np.float32))


def kernel(x, c, w_ada, b_ada, norm_gain, w_in, sb_q_gain, sb_k_gain, ret_norm_gain, w_out):
    b, s, d = x.shape
    depth = w_ada.shape[0]
    heads = w_in.shape[2] // (N_GROUPS * HEAD_DIM)
    cos_t, sin_t = _rotary_tables(s)
    din, qd, kd, cd = _retention_tables(heads, RET_CHUNK)
    c_pad = jnp.pad(c, ((0, 8 - b), (0, 0)))
    rope_spec = pl.BlockSpec((PROJ_ROWS, HEAD_DIM), lambda bi, si, g: (si, 0))
    gain_spec = pl.BlockSpec((1, 1, HEAD_DIM), lambda bi, si, g: (g, 0, 0))
    for layer in range(depth):
        mod = _adaln(c_pad, w_ada[layer], b_ada[layer][None, :])
        mod3 = mod[:b].reshape(b, 3, d)
        h = _norm_mod(x, mod3, norm_gain[layer][None, :])
        w = w_in[layer].astype(BF16)
        qk_gain = jnp.stack([sb_q_gain[layer], sb_k_gain[layer]])[:, None, :]
        sb_qk = _in_proj(h, w, "qk_norm", 0, 1, (qk_gain,), (gain_spec,),
                         first_mult=float(1.0 / np.sqrt(HEAD_DIM)))
        vals = _in_proj(h, w, "identity", 2, 4)
        gates = _in_proj(h, w, "silu", 3, 4)
        ret_qk = _in_proj(h, w, "rotary", 4, 1, (cos_t, sin_t), (rope_spec, rope_spec),
                          second_mult=float(HEAD_DIM ** -0.5))
        oa = _sb_attention(sb_qk, vals, gates, heads)
        ob = _retention(ret_qk, vals, gates, heads, din, qd, kd, cd, ret_norm_gain[layer][None, :])
        x = _out_proj(oa, ob, w_out[layer].astype(BF16), x, mod3)
    return x
```

```python
import functools

import jax
import jax.numpy as jnp
import numpy as np
from jax import lax
from jax.experimental import pallas as pl
from jax.experimental.pallas import tpu as pltpu

HEAD_DIM = 128
ROPE_BASE = 10000.0
EPS = 1e-6
N_GROUPS = 8
F32 = jnp.float32
BF16 = jnp.bfloat16

ADALN_STEPS = 16
NORM_ROWS = 1024
NORM_CHUNK = 32
NORM_UNROLL = 8
PROJ_ROWS = 2048
PROJ_UNIT_ROWS = 1024
PROJ_COLS = 256
SB_Q = 128
SB_K = 256
SB_TAIL = 128
SB_STEP_ROWS = 256
SB_HEADS_PER_STEP = 4
RET_CHUNK = 256
OUT_ROWS = 512
OUT_COLS = 256
VMEM_LIMIT = 56 * 1024 * 1024
LOG_WEIGHT_FLOOR = -120.0
LOG2E = 1.4426950408889634


def _silu(v):
    return v / (1.0 + jnp.exp(-v))


def _adaln_kernel(c_ref, w_ref, b_ref, o_ref):
    s = _silu(c_ref[...])
    o_ref[...] = jnp.dot(s, w_ref[...], preferred_element_type=F32,
                         precision=lax.Precision.HIGHEST) + b_ref[...]


def _adaln(c_pad, w, b):
    rows, d = c_pad.shape
    n = w.shape[1]
    tn = n // ADALN_STEPS
    return pl.pallas_call(
        _adaln_kernel,
        grid=(n // tn,),
        in_specs=[pl.BlockSpec((rows, d), lambda j: (0, 0)),
                  pl.BlockSpec((d, tn), lambda j: (0, j)),
                  pl.BlockSpec((1, tn), lambda j: (0, j))],
        out_specs=pl.BlockSpec((rows, tn), lambda j: (0, j)),
        out_shape=jax.ShapeDtypeStruct((rows, n), F32),
        compiler_params=pltpu.CompilerParams(
            dimension_semantics=("arbitrary",), vmem_limit_bytes=VMEM_LIMIT),
        name="adaln",
    )(c_pad, w, b)


def _norm_kernel(x_ref, mod_ref, ng_ref, h_ref):
    shift = mod_ref[0, 0:1, :]
    gain = ng_ref[...] * (1.0 + mod_ref[0, 1:2, :])

    def body(r, _):
        rows = pl.ds(pl.multiple_of(r * NORM_CHUNK, NORM_CHUNK), NORM_CHUNK)
        x = x_ref[0, rows, :]
        ms = jnp.mean(x * x, axis=-1, keepdims=True)
        h_ref[0, rows, :] = (x * lax.rsqrt(ms + EPS) * gain + shift).astype(BF16)
        return 0

    lax.fori_loop(0, x_ref.shape[1] // NORM_CHUNK, body, 0, unroll=NORM_UNROLL)


def _norm_mod(x, mod3, norm_gain):
    b, s, d = x.shape
    tm = NORM_ROWS
    return pl.pallas_call(
        _norm_kernel,
        grid=(b, s // tm),
        in_specs=[pl.BlockSpec((1, tm, d), lambda bi, si: (bi, si, 0)),
                  pl.BlockSpec((1, 3, d), lambda bi, si: (bi, 0, 0)),
                  pl.BlockSpec((1, d), lambda bi, si: (0, 0))],
        out_specs=pl.BlockSpec((1, tm, d), lambda bi, si: (bi, si, 0)),
        out_shape=jax.ShapeDtypeStruct((b, s, d), BF16),
        compiler_params=pltpu.CompilerParams(
            dimension_semantics=("parallel", "parallel"), vmem_limit_bytes=VMEM_LIMIT),
        name="norm_mod",
    )(x, mod3, norm_gain)


def _proj_kernel(h_ref, w_ref, *refs, kind, heads, first_mult, second_mult):
    o_ref = refs[-1]
    g = pl.program_id(2)
    mult = jnp.where(g == 0, first_mult, second_mult).astype(F32)
    if kind == "qk_norm":
        gain = refs[0][0] * mult

        def epilogue(a, rows):
            ms = jnp.mean(a * a, axis=-1, keepdims=True)
            return a * lax.rsqrt(ms + EPS) * gain
    elif kind == "rotary":
        def epilogue(a, rows):
            return (a * (refs[0][rows, :] * mult)
                    + pltpu.roll(a, HEAD_DIM // 2, 1) * (refs[1][rows, :] * mult))
    elif kind == "silu":
        epilogue = lambda a, rows: _silu(a)
    else:
        epilogue = lambda a, rows: a

    heads_per_chunk = PROJ_COLS // HEAD_DIM
    for r in range(h_ref.shape[1] // PROJ_UNIT_ROWS):
        rows = slice(r * PROJ_UNIT_ROWS, (r + 1) * PROJ_UNIT_ROWS)
        h = h_ref[0, rows, :]
        for c in range(heads // heads_per_chunk):
            acc = jnp.dot(h, w_ref[:, c * PROJ_COLS:(c + 1) * PROJ_COLS],
                          preferred_element_type=F32)
            for j in range(heads_per_chunk):
                a = acc[:, j * HEAD_DIM:(j + 1) * HEAD_DIM]
                o_ref[0, c * heads_per_chunk + j, rows, :] = epilogue(a, rows).astype(BF16)


def _in_proj(h, w_bf16, kind, first_group, group_stride, extra=(), extra_specs=(),
             first_mult=1.0, second_mult=1.0):
    b, s, d = h.shape
    tn = w_bf16.shape[1] // N_GROUPS
    heads = tn // HEAD_DIM
    tm = PROJ_ROWS
    return pl.pallas_call(
        functools.partial(_proj_kernel, kind=kind, heads=heads,
                          first_mult=first_mult, second_mult=second_mult),
        grid=(b, s // tm, 2),
        in_specs=[
            pl.BlockSpec((1, tm, d), lambda bi, si, g: (bi, si, 0)),
            pl.BlockSpec((d, tn), lambda bi, si, g: (0, first_group + group_stride * g)),
            *extra_specs,
        ],
        out_specs=pl.BlockSpec((1, heads, tm, HEAD_DIM), lambda bi, si, g: (bi, g, si, 0)),
        out_shape=jax.ShapeDtypeStruct((b, 2 * heads, s, HEAD_DIM), BF16),
        compiler_params=pltpu.CompilerParams(
            dimension_semantics=("parallel", "parallel", "arbitrary"),
            vmem_limit_bytes=VMEM_LIMIT),
        name="in_proj_" + kind,
    )(h, w_bf16, *extra)


def _sb_scores(q, k_blk):
    z = lax.dot_general(q, k_blk, (((1,), (1,)), ((), ())), preferred_element_type=F32)
    sp = jnp.maximum(z, 0.0) + jnp.log(1.0 + jnp.exp2(jnp.abs(z) * (-LOG2E)))
    return sp, z - sp


def _sb_later(sp, neg_upper):
    return jnp.dot(sp.astype(BF16), neg_upper, preferred_element_type=F32)


def _sb_edge_window(q, k_ref, v_ref, hh, q0, neg_upper):
    row = lax.broadcasted_iota(jnp.int32, (SB_Q, SB_K), 0)
    col = lax.broadcasted_iota(jnp.int32, (SB_Q, SB_K), 1)
    d_start = jnp.maximum(q0 - SB_Q, 0)
    p_start = jnp.maximum(d_start - SB_K, 0)
    mask_d = col < row + (q0 - d_start)
    mask_p = col < (d_start - p_start)
    d_start = pl.multiple_of(d_start, SB_Q)
    p_start = pl.multiple_of(p_start, SB_Q)

    sp_d, logb_d = _sb_scores(q, k_ref[0, hh, pl.ds(d_start, SB_K), :])
    sp_p, logb_p = _sb_scores(q, k_ref[0, hh, pl.ds(p_start, SB_K), :])
    sp_d = jnp.where(mask_d, sp_d, 0.0)
    sp_p = jnp.where(mask_p, sp_p, 0.0)
    w_d = jnp.where(mask_d, jnp.exp(logb_d + _sb_later(sp_d, neg_upper)), 0.0)
    carry = -jnp.sum(sp_d, axis=1, keepdims=True)
    w_p = jnp.where(mask_p, jnp.exp(logb_p + _sb_later(sp_p, neg_upper) + carry), 0.0)
    acc = (jnp.dot(w_d.astype(BF16), v_ref[0, hh, pl.ds(d_start, SB_K), :], preferred_element_type=F32)
           + jnp.dot(w_p.astype(BF16), v_ref[0, hh, pl.ds(p_start, SB_K), :], preferred_element_type=F32))
    carry = carry - jnp.sum(sp_p, axis=1, keepdims=True)
    return carry, acc, p_start


def _sb_interior_windows(q_ref, k_ref, v_ref, step_start, neg_upper, heads_per_step):
    subs = SB_STEP_ROWS // SB_Q
    win = 2 * SB_K
    span = win + (subs - 1) * SB_Q
    w_start = pl.multiple_of(step_start - (win - SB_Q), SB_Q)
    zs = []
    for hh in range(heads_per_step):
        z = lax.dot_general(q_ref[0, hh], k_ref[0, hh, pl.ds(w_start, span), :],
                            (((1,), (1,)), ((), ())), preferred_element_type=F32)
        zs += [z[sub * SB_Q:(sub + 1) * SB_Q, sub * SB_Q:sub * SB_Q + win] for sub in range(subs)]
    z = jnp.concatenate(zs, axis=0)
    n = z.shape[0]
    sp = jnp.maximum(z, 0.0) + jnp.log(1.0 + jnp.exp2(jnp.abs(z) * (-LOG2E)))
    logb = z - sp
    row = lax.broadcasted_iota(jnp.int32, (n, SB_Q), 0) & (SB_Q - 1)
    col = lax.broadcasted_iota(jnp.int32, (n, SB_Q), 1)
    past = col < row
    diag = lambda t: jnp.concatenate([t[:, :SB_K - SB_Q], jnp.where(past, t[:, SB_K - SB_Q:], 0.0)],
                                     axis=1)
    sp_p, sp_d = sp[:, :SB_K], diag(sp[:, SB_K:])
    w_d = diag(jnp.exp(logb[:, SB_K:] + _sb_later(sp_d, neg_upper)))
    carry = -jnp.sum(sp_d, axis=1, keepdims=True)
    w_p = jnp.exp(logb[:, :SB_K] + _sb_later(sp_p, neg_upper) + carry)
    carry = carry - jnp.sum(sp_p, axis=1, keepdims=True)
    w = jnp.concatenate([w_p, w_d], axis=1).astype(BF16)
    carries, accs = [], []
    for hh in range(heads_per_step):
        for sub in range(subs):
            c = hh * subs + sub
            v_win = v_ref[0, hh, pl.ds(pl.multiple_of(w_start + sub * SB_Q, SB_Q), win), :]
            accs.append(jnp.dot(w[c * SB_Q:(c + 1) * SB_Q], v_win, preferred_element_type=F32))
            carries.append(carry[c * SB_Q:(c + 1) * SB_Q])
    return carries, accs, w_start


def _sb_kernel(q_ref, k_ref, v_ref, g_ref, o_ref, *, heads_per_step):
    i = pl.program_id(2)
    urow = lax.broadcasted_iota(jnp.int32, (SB_K, SB_K), 0)
    ucol = lax.broadcasted_iota(jnp.int32, (SB_K, SB_K), 1)
    neg_upper = jnp.where(urow > ucol, -1.0, 0.0).astype(BF16)
    neg_upper_tail = neg_upper[:SB_TAIL, :SB_TAIL]
    subs = SB_STEP_ROWS // SB_Q

    def run(edge):
        step_start = i * SB_STEP_ROWS
        if not edge:
            carries, accs, w_start = _sb_interior_windows(
                q_ref, k_ref, v_ref, step_start, neg_upper, heads_per_step)
        chains = []
        for hh in range(heads_per_step):
            for sub in range(subs):
                rows = slice(sub * SB_Q, (sub + 1) * SB_Q)
                q = q_ref[0, hh, rows, :]
                if edge:
                    carry, acc, p_start = _sb_edge_window(
                        q, k_ref, v_ref, hh, step_start + sub * SB_Q, neg_upper)
                else:
                    carry, acc = carries[hh * subs + sub], accs[hh * subs + sub]
                    p_start = w_start + sub * SB_Q
                chains.append((hh, rows, q, carry, acc, p_start,
                               jnp.max(carry) > LOG_WEIGHT_FLOOR))

        for hh, rows, q, carry, acc, p_start, live in chains:
            tiles_left = p_start // SB_TAIL

            def cond(state, tiles_left=tiles_left):
                jj, live, _, _ = state
                return (jj < tiles_left) & live

            def body(state, hh=hh, q=q, p_start=p_start):
                jj, _, carry, acc = state
                start = pl.multiple_of(p_start - (jj + 1) * SB_TAIL, SB_TAIL)
                sp, logb = _sb_scores(q, k_ref[0, hh, pl.ds(start, SB_TAIL), :])
                w = jnp.exp(logb + _sb_later(sp, neg_upper_tail) + carry)
                acc = acc + jnp.dot(w.astype(BF16), v_ref[0, hh, pl.ds(start, SB_TAIL), :],
                                    preferred_element_type=F32)
                carry = carry - jnp.sum(sp, axis=1, keepdims=True)
                return jj + 1, jnp.max(carry) > LOG_WEIGHT_FLOOR, carry, acc

            _, _, _, acc = lax.while_loop(cond, body, (jnp.int32(0), live, carry, acc))
            o_ref[0, hh, rows, :] = (acc * g_ref[0, hh, rows, :].astype(F32)).astype(BF16)

    first_interior = (SB_Q + SB_K + SB_STEP_ROWS - 1) // SB_STEP_ROWS
    pl.when(i >= first_interior)(lambda: run(False))
    pl.when(i < first_interior)(lambda: run(True))


def _sb_attention(qk, v, gate, heads):
    b, _, s, d = qk.shape
    hb = SB_HEADS_PER_STEP
    groups = heads // hb
    t = SB_STEP_ROWS
    return pl.pallas_call(
        functools.partial(_sb_kernel, heads_per_step=hb),
        grid=(b, groups, s // t),
        in_specs=[
            pl.BlockSpec((1, hb, t, d), lambda bi, g, i: (bi, g, i, 0)),
            pl.BlockSpec((1, hb, s, d), lambda bi, g, i: (bi, groups + g, 0, 0)),
            pl.BlockSpec((1, hb, s, d), lambda bi, g, i: (bi, g, 0, 0)),
            pl.BlockSpec((1, hb, t, d), lambda bi, g, i: (bi, g, i, 0)),
        ],
        out_specs=pl.BlockSpec((1, hb, t, d), lambda bi, g, i: (bi, g, i, 0)),
        out_shape=jax.ShapeDtypeStruct((b, heads, s, d), BF16),
        compiler_params=pltpu.CompilerParams(
            dimension_semantics=("parallel", "parallel", "arbitrary"),
            vmem_limit_bytes=VMEM_LIMIT),
        name="sb_attn",
    )(qk, qk, v, gate)


def _ret_kernel(q_ref, k_ref, v_ref, g_ref, din_ref, qd_ref, kd_ref, cd_ref, gain_ref,
                o_ref, state_scr, *, heads):
    ci = pl.program_id(1)

    @pl.when(ci == 0)
    def _():
        state_scr[...] = jnp.zeros_like(state_scr)

    for h in range(heads):
        qc = q_ref[0, h]
        kc = k_ref[0, h]
        vc = v_ref[0, h]
        state = state_scr[h]

        scores = lax.dot_general(qc, kc, (((1,), (1,)), ((), ())), preferred_element_type=F32)
        inner = jnp.dot((scores * din_ref[h]).astype(BF16), vc, preferred_element_type=F32)
        q_dec = (qc.astype(F32) * qd_ref[h]).astype(BF16)
        cross = jnp.dot(q_dec, state.astype(BF16), preferred_element_type=F32)
        k_dec = (kc.astype(F32) * kd_ref[h]).astype(BF16)
        kv = lax.dot_general(k_dec, vc, (((0,), (0,)), ((), ())), preferred_element_type=F32)
        state_scr[h] = state * cd_ref[h] + kv

        o = inner + cross
        mu = jnp.mean(o, axis=-1, keepdims=True)
        cen = o - mu
        var = jnp.mean(cen * cen, axis=-1, keepdims=True)
        y = cen * lax.rsqrt(var + EPS) * gain_ref[:, h * HEAD_DIM:(h + 1) * HEAD_DIM]
        o_ref[0, h] = (y * g_ref[0, h].astype(F32)).astype(BF16)


def _retention(qk, v, gate, heads, din, qd, kd, cd, gain):
    b, _, s, d = qk.shape
    c = RET_CHUNK
    blk = lambda group: pl.BlockSpec((1, heads, c, d), lambda bi, ci: (bi, group, ci, 0))
    const = lambda arr: pl.BlockSpec(arr.shape, lambda bi, ci: (0,) * arr.ndim)
    return pl.pallas_call(
        functools.partial(_ret_kernel, heads=heads),
        grid=(b, s // c),
        in_specs=[blk(0), blk(1), blk(1), blk(1),
                  const(din), const(qd), const(kd), const(cd), const(gain)],
        out_specs=pl.BlockSpec((1, heads, c, d), lambda bi, ci: (bi, 0, ci, 0)),
        out_shape=jax.ShapeDtypeStruct((b, heads, s, d), BF16),
        scratch_shapes=[pltpu.VMEM((heads, d, d), F32)],
        compiler_params=pltpu.CompilerParams(
            dimension_semantics=("parallel", "arbitrary"),
            vmem_limit_bytes=VMEM_LIMIT),
        name="retention",
    )(qk, qk, v, gate, din, qd, kd, cd, gain)


def _retention_tables(heads, c):
    log_gamma = np.log1p(-np.exp2(-5.0 - np.arange(heads, dtype=np.float64)))
    idx = np.arange(c, dtype=np.float64)
    rel = idx[:, None] - idx[None, :]
    din = np.where(rel[None] >= 0, np.exp(np.maximum(rel, 0.0)[None] * log_gamma[:, None, None]), 0.0)
    qd = np.exp((idx[None, :] + 1.0) * log_gamma[:, None])[..., None]
    kd = np.exp((c - 1.0 - idx[None, :]) * log_gamma[:, None])[..., None]
    cd = np.exp(c * log_gamma)[:, None, None]
    bc = lambda t: np.ascontiguousarray(
        np.broadcast_to(t, t.shape[:-1] + (HEAD_DIM,)), dtype=np.float32)
    return din.astype(np.float32), bc(qd), bc(kd), bc(cd)


def _out_kernel(a_ref, r_ref, w_ref, x_ref, mod_ref, o_ref, *, heads):
    parts = [a_ref[0, j] for j in range(heads)] + [r_ref[0, j] for j in range(heads)]
    mix = jnp.concatenate(parts, axis=-1)
    for c in range(w_ref.shape[1] // OUT_COLS):
        cols = slice(c * OUT_COLS, (c + 1) * OUT_COLS)
        y = jnp.dot(mix, w_ref[:, cols], preferred_element_type=F32)
        o_ref[0, :, cols] = x_ref[0, :, cols] + mod_ref[0, 2:3, cols] * y


def _out_proj(oa, ob, w_bf16, x, mod3):
    b, s, d = x.shape
    heads = oa.shape[1]
    tm = OUT_ROWS
    return pl.pallas_call(
        functools.partial(_out_kernel, heads=heads),
        grid=(b, s // tm),
        in_specs=[
            pl.BlockSpec((1, heads, tm, HEAD_DIM), lambda bi, si: (bi, 0, si, 0)),
            pl.BlockSpec((1, heads, tm, HEAD_DIM), lambda bi, si: (bi, 0, si, 0)),
            pl.BlockSpec(w_bf16.shape, lambda bi, si: (0, 0)),
            pl.BlockSpec((1, tm, d), lambda bi, si: (bi, si, 0)),
            pl.BlockSpec((1, 3, d), lambda bi, si: (bi, 0, 0)),
        ],
        out_specs=pl.BlockSpec((1, tm, d), lambda bi, si: (bi, si, 0)),
        out_shape=jax.ShapeDtypeStruct((b, s, d), F32),
        compiler_params=pltpu.CompilerParams(
            dimension_semantics=("parallel", "parallel"),
            vmem_limit_bytes=VMEM_LIMIT),
        name="out_proj",
    )(oa, ob, w_bf16, x, mod3)


def _rotary_tables(s):
    half = HEAD_DIM // 2
    inv_freq = ROPE_BASE ** (-np.arange(half, dtype=np.float64) / half)
    ang = np.arange(s, dtype=np.float64)[:, None] * inv_freq[None, :]
    cos, sin = np.cos(ang), np.sin(ang)
    return (np.concatenate([cos, cos], axis=-1).astype(np.float32),
            np.concatenate([-sin, sin], axis=-1).astype(np.float32))


def kernel(x, c, w_ada, b_ada, norm_gain, w_in, sb_q_gain, sb_k_gain, ret_norm_gain, w_out):
    b, s, d = x.shape
    depth = w_ada.shape[0]
    heads = w_in.shape[2] // (N_GROUPS * HEAD_DIM)
    cos_t, sin_t = _rotary_tables(s)
    din, qd, kd, cd = _retention_tables(heads, RET_CHUNK)
    c_pad = jnp.pad(c, ((0, 8 - b), (0, 0)))
    rope_spec = pl.BlockSpec((PROJ_ROWS, HEAD_DIM), lambda bi, si, g: (si, 0))
    gain_spec = pl.BlockSpec((1, 1, HEAD_DIM), lambda bi, si, g: (g, 0, 0))
    for layer in range(depth):
        mod = _adaln(c_pad, w_ada[layer], b_ada[layer][None, :])
        mod3 = mod[:b].reshape(b, 3, d)
        h = _norm_mod(x, mod3, norm_gain[layer][None, :])
        w = w_in[layer].astype(BF16)
        qk_gain = jnp.stack([sb_q_gain[layer], sb_k_gain[layer]])[:, None, :]
        sb_qk = _in_proj(h, w, "qk_norm", 0, 1, (qk_gain,), (gain_spec,),
                         first_mult=float(1.0 / np.sqrt(HEAD_DIM)))
        vals = _in_proj(h, w, "identity", 2, 4)
        gates = _in_proj(h, w, "silu", 3, 4)
        ret_qk = _in_proj(h, w, "rotary", 4, 1, (cos_t, sin_t), (rope_spec, rope_spec),
                          second_mult=float(HEAD_DIM ** -0.5))
        oa = _sb_attention(sb_qk, vals, gates, heads)
        ob = _retention(ret_qk, vals, gates, heads, din, qd, kd, cd, ret_norm_gain[layer][None, :])
        x = _out_proj(oa, ob, w_out[layer].astype(BF16), x, mod3)
    return x
```

```python
import functools

import jax
import jax.numpy as jnp
import numpy as np
from jax import lax
from jax.experimental import pallas as pl
from jax.experimental.pallas import tpu as pltpu

HEAD_DIM = 128
ROPE_BASE = 10000.0
EPS = 1e-6
N_GROUPS = 8
F32 = jnp.float32
BF16 = jnp.bfloat16

ADALN_STEPS = 16
NORM_ROWS = 1024
NORM_CHUNK = 32
NORM_UNROLL = 8
PROJ_ROWS = 2048
PROJ_UNIT_ROWS = 1024
PROJ_COLS = 256
SB_Q = 128
SB_K = 256
SB_TAIL = 128
SB_STEP_ROWS = 256
SB_HEADS_PER_STEP = 4
RET_CHUNK = 256
OUT_ROWS = 512
OUT_COLS = 256
VMEM_LIMIT = 56 * 1024 * 1024
LOG_WEIGHT_FLOOR = -120.0
LOG2E = 1.4426950408889634


def _silu(v):
    return v / (1.0 + jnp.exp(-v))


def _adaln_kernel(c_ref, w_ref, b_ref, o_ref):
    s = _silu(c_ref[...])
    o_ref[...] = jnp.dot(s, w_ref[...], preferred_element_type=F32,
                         precision=lax.Precision.HIGHEST) + b_ref[...]


def _adaln(c_pad, w, b):
    rows, d = c_pad.shape
    n = w.shape[1]
    tn = n // ADALN_STEPS
    return pl.pallas_call(
        _adaln_kernel,
        grid=(n // tn,),
        in_specs=[pl.BlockSpec((rows, d), lambda j: (0, 0)),
                  pl.BlockSpec((d, tn), lambda j: (0, j)),
                  pl.BlockSpec((1, tn), lambda j: (0, j))],
        out_specs=pl.BlockSpec((rows, tn), lambda j: (0, j)),
        out_shape=jax.ShapeDtypeStruct((rows, n), F32),
        compiler_params=pltpu.CompilerParams(
            dimension_semantics=("arbitrary",), vmem_limit_bytes=VMEM_LIMIT),
        name="adaln",
    )(c_pad, w, b)


def _norm_kernel(x_ref, mod_ref, ng_ref, h_ref):
    shift = mod_ref[0, 0:1, :]
    gain = ng_ref[...] * (1.0 + mod_ref[0, 1:2, :])

    def body(r, _):
        rows = pl.ds(pl.multiple_of(r * NORM_CHUNK, NORM_CHUNK), NORM_CHUNK)
        x = x_ref[0, rows, :]
        ms = jnp.mean(x * x, axis=-1, keepdims=True)
        h_ref[0, rows, :] = (x * lax.rsqrt(ms + EPS) * gain + shift).astype(BF16)
        return 0

    lax.fori_loop(0, x_ref.shape[1] // NORM_CHUNK, body, 0, unroll=NORM_UNROLL)


def _norm_mod(x, mod3, norm_gain):
    b, s, d = x.shape
    tm = NORM_ROWS
    return pl.pallas_call(
        _norm_kernel,
        grid=(b, s // tm),
        in_specs=[pl.BlockSpec((1, tm, d), lambda bi, si: (bi, si, 0)),
                  pl.BlockSpec((1, 3, d), lambda bi, si: (bi, 0, 0)),
                  pl.BlockSpec((1, d), lambda bi, si: (0, 0))],
        out_specs=pl.BlockSpec((1, tm, d), lambda bi, si: (bi, si, 0)),
        out_shape=jax.ShapeDtypeStruct((b, s, d), BF16),
        compiler_params=pltpu.CompilerParams(
            dimension_semantics=("parallel", "parallel"), vmem_limit_bytes=VMEM_LIMIT),
        name="norm_mod",
    )(x, mod3, norm_gain)


def _proj_kernel(h_ref, w_ref, *refs, kind, heads, first_mult, second_mult):
    o_ref = refs[-1]
    g = pl.program_id(2)
    mult = jnp.where(g == 0, first_mult, second_mult).astype(F32)
    if kind == "qk_norm":
        gain = refs[0][0] * mult

        def epilogue(a, rows):
            ms = jnp.mean(a * a, axis=-1, keepdims=True)
            return a * lax.rsqrt(ms + EPS) * gain
    elif kind == "rotary":
        def epilogue(a, rows):
            return (a * (refs[0][rows, :] * mult)
                    + pltpu.roll(a, HEAD_DIM // 2, 1) * (refs[1][rows, :] * mult))
    elif kind == "silu":
        epilogue = lambda a, rows: _silu(a)
    else:
        epilogue = lambda a, rows: a

    heads_per_chunk = PROJ_COLS // HEAD_DIM
    for r in range(h_ref.shape[1] // PROJ_UNIT_ROWS):
        rows = slice(r * PROJ_UNIT_ROWS, (r + 1) * PROJ_UNIT_ROWS)
        h = h_ref[0, rows, :]
        for c in range(heads // heads_per_chunk):
            acc = jnp.dot(h, w_ref[:, c * PROJ_COLS:(c + 1) * PROJ_COLS],
                          preferred_element_type=F32)
            for j in range(heads_per_chunk):
                a = acc[:, j * HEAD_DIM:(j + 1) * HEAD_DIM]
                o_ref[0, c * heads_per_chunk + j, rows, :] = epilogue(a, rows).astype(BF16)


def _in_proj(h, w_bf16, kind, first_group, group_stride, extra=(), extra_specs=(),
             first_mult=1.0, second_mult=1.0):
    b, s, d = h.shape
    tn = w_bf16.shape[1] // N_GROUPS
    heads = tn // HEAD_DIM
    tm = PROJ_ROWS
    return pl.pallas_call(
        functools.partial(_proj_kernel, kind=kind, heads=heads,
                          first_mult=first_mult, second_mult=second_mult),
        grid=(b, s // tm, 2),
        in_specs=[
            pl.BlockSpec((1, tm, d), lambda bi, si, g: (bi, si, 0)),
            pl.BlockSpec((d, tn), lambda bi, si, g: (0, first_group + group_stride * g)),
            *extra_specs,
        ],
        out_specs=pl.BlockSpec((1, heads, tm, HEAD_DIM), lambda bi, si, g: (bi, g, si, 0)),
        out_shape=jax.ShapeDtypeStruct((b, 2 * heads, s, HEAD_DIM), BF16),
        compiler_params=pltpu.CompilerParams(
            dimension_semantics=("parallel", "parallel", "arbitrary"),
            vmem_limit_bytes=VMEM_LIMIT),
        name="in_proj_" + kind,
    )(h, w_bf16, *extra)


def _sb_scores(q, k_blk):
    z = lax.dot_general(q, k_blk, (((1,), (1,)), ((), ())), preferred_element_type=F32)
    sp = jnp.maximum(z, 0.0) + jnp.log(1.0 + jnp.exp2(jnp.abs(z) * (-LOG2E)))
    return sp, z - sp


def _sb_later(sp, neg_upper):
    return jnp.dot(sp.astype(BF16), neg_upper, preferred_element_type=F32)


def _sb_edge_window(q, k_ref, v_ref, hh, q0, neg_upper):
    row = lax.broadcasted_iota(jnp.int32, (SB_Q, SB_K), 0)
    col = lax.broadcasted_iota(jnp.int32, (SB_Q, SB_K), 1)
    d_start = jnp.maximum(q0 - SB_Q, 0)
    p_start = jnp.maximum(d_start - SB_K, 0)
    mask_d = col < row + (q0 - d_start)
    mask_p = col < (d_start - p_start)
    d_start = pl.multiple_of(d_start, SB_Q)
    p_start = pl.multiple_of(p_start, SB_Q)

    sp_d, logb_d = _sb_scores(q, k_ref[0, hh, pl.ds(d_start, SB_K), :])
    sp_p, logb_p = _sb_scores(q, k_ref[0, hh, pl.ds(p_start, SB_K), :])
    sp_d = jnp.where(mask_d, sp_d, 0.0)
    sp_p = jnp.where(mask_p, sp_p, 0.0)
    w_d = jnp.where(mask_d, jnp.exp(logb_d + _sb_later(sp_d, neg_upper)), 0.0)
    carry = -jnp.sum(sp_d, axis=1, keepdims=True)
    w_p = jnp.where(mask_p, jnp.exp(logb_p + _sb_later(sp_p, neg_upper) + carry), 0.0)
    acc = (jnp.dot(w_d.astype(BF16), v_ref[0, hh, pl.ds(d_start, SB_K), :], preferred_element_type=F32)
           + jnp.dot(w_p.astype(BF16), v_ref[0, hh, pl.ds(p_start, SB_K), :], preferred_element_type=F32))
    carry = carry - jnp.sum(sp_p, axis=1, keepdims=True)
    return carry, acc, p_start


def _sb_interior_windows(q_ref, k_ref, v_ref, step_start, neg_upper, heads_per_step):
    subs = SB_STEP_ROWS // SB_Q
    win = 2 * SB_K
    span = win + (subs - 1) * SB_Q
    w_start = pl.multiple_of(step_start - (win - SB_Q), SB_Q)
    zs = []
    for hh in range(heads_per_step):
        z = lax.dot_general(q_ref[0, hh], k_ref[0, hh, pl.ds(w_start, span), :],
                            (((1,), (1,)), ((), ())), preferred_element_type=F32)
        zs += [z[sub * SB_Q:(sub + 1) * SB_Q, sub * SB_Q:sub * SB_Q + win] for sub in range(subs)]
    z = jnp.concatenate(zs, axis=0).astype(BF16)
    n = z.shape[0]
    l = jnp.log(1.0 + jnp.exp2(jnp.abs(z) * (-LOG2E)))
    sp = jnp.maximum(z, 0.0) + l
    logb = jnp.minimum(z, 0.0) - l
    row = lax.broadcasted_iota(jnp.int32, (n, SB_Q), 0) & (SB_Q - 1)
    col = lax.broadcasted_iota(jnp.int32, (n, SB_Q), 1)
    past = jnp.where(col < row, 1.0, 0.0).astype(BF16)
    diag = lambda t: jnp.concatenate([t[:, :SB_K - SB_Q], t[:, SB_K - SB_Q:] * past], axis=1)
    sp_p, sp_d = sp[:, :SB_K], diag(sp[:, SB_K:])
    w_d = diag(jnp.exp(logb[:, SB_K:] + _sb_later(sp_d, neg_upper).astype(BF16)))
    carry = -jnp.sum(sp_d.astype(F32), axis=1, keepdims=True)
    w_p = jnp.exp(logb[:, :SB_K] + (_sb_later(sp_p, neg_upper) + carry).astype(BF16))
    carry = carry - jnp.sum(sp_p.astype(F32), axis=1, keepdims=True)
    w = jnp.concatenate([w_p, w_d], axis=1)
    carries, accs = [], []
    for hh in range(heads_per_step):
        for sub in range(subs):
            c = hh * subs + sub
            v_win = v_ref[0, hh, pl.ds(pl.multiple_of(w_start + sub * SB_Q, SB_Q), win), :]
            accs.append(jnp.dot(w[c * SB_Q:(c + 1) * SB_Q], v_win, preferred_element_type=F32))
            carries.append(carry[c * SB_Q:(c + 1) * SB_Q])
    return carries, accs, w_start


def _sb_kernel(q_ref, k_ref, v_ref, g_ref, o_ref, *, heads_per_step):
    i = pl.program_id(2)
    urow = lax.broadcasted_iota(jnp.int32, (SB_K, SB_K), 0)
    ucol = lax.broadcasted_iota(jnp.int32, (SB_K, SB_K), 1)
    neg_upper = jnp.where(urow > ucol, -1.0, 0.0).astype(BF16)
    neg_upper_tail = neg_upper[:SB_TAIL, :SB_TAIL]
    subs = SB_STEP_ROWS // SB_Q

    def run(edge):
        step_start = i * SB_STEP_ROWS
        if not edge:
            carries, accs, w_start = _sb_interior_windows(
                q_ref, k_ref, v_ref, step_start, neg_upper, heads_per_step)
        chains = []
        for hh in range(heads_per_step):
            for sub in range(subs):
                rows = slice(sub * SB_Q, (sub + 1) * SB_Q)
                q = q_ref[0, hh, rows, :]
                if edge:
                    carry, acc, p_start = _sb_edge_window(
                        q, k_ref, v_ref, hh, step_start + sub * SB_Q, neg_upper)
                else:
                    carry, acc = carries[hh * subs + sub], accs[hh * subs + sub]
                    p_start = w_start + sub * SB_Q
                chains.append((hh, rows, q, carry, acc, p_start,
                               jnp.max(carry) > LOG_WEIGHT_FLOOR))

        for hh, rows, q, carry, acc, p_start, live in chains:
            tiles_left = p_start // SB_TAIL

            def cond(state, tiles_left=tiles_left):
                jj, live, _, _ = state
                return (jj < tiles_left) & live

            def body(state, hh=hh, q=q, p_start=p_start):
                jj, _, carry, acc = state
                start = pl.multiple_of(p_start - (jj + 1) * SB_TAIL, SB_TAIL)
                sp, logb = _sb_scores(q, k_ref[0, hh, pl.ds(start, SB_TAIL), :])
                w = jnp.exp(logb + _sb_later(sp, neg_upper_tail) + carry)
                acc = acc + jnp.dot(w.astype(BF16), v_ref[0, hh, pl.ds(start, SB_TAIL), :],
                                    preferred_element_type=F32)
                carry = carry - jnp.sum(sp, axis=1, keepdims=True)
                return jj + 1, jnp.max(carry) > LOG_WEIGHT_FLOOR, carry, acc

            _, _, _, acc = lax.while_loop(cond, body, (jnp.int32(0), live, carry, acc))
            o_ref[0, hh, rows, :] = (acc * g_ref[0, hh, rows, :].astype(F32)).astype(BF16)

    first_interior = (SB_Q + SB_K + SB_STEP_ROWS - 1) // SB_STEP_ROWS
    pl.when(i >= first_interior)(lambda: run(False))
    pl.when(i < first_interior)(lambda: run(True))


def _sb_attention(qk, v, gate, heads):
    b, _, s, d = qk.shape
    hb = SB_HEADS_PER_STEP
    groups = heads // hb
    t = SB_STEP_ROWS
    return pl.pallas_call(
        functools.partial(_sb_kernel, heads_per_step=hb),
        grid=(b, groups, s // t),
        in_specs=[
            pl.BlockSpec((1, hb, t, d), lambda bi, g, i: (bi, g, i, 0)),
            pl.BlockSpec((1, hb, s, d), lambda bi, g, i: (bi, groups + g, 0, 0)),
            pl.BlockSpec((1, hb, s, d), lambda bi, g, i: (bi, g, 0, 0)),
            pl.BlockSpec((1, hb, t, d), lambda bi, g, i: (bi, g, i, 0)),
        ],
        out_specs=pl.BlockSpec((1, hb, t, d), lambda bi, g, i: (bi, g, i, 0)),
        out_shape=jax.ShapeDtypeStruct((b, heads, s, d), BF16),
        compiler_params=pltpu.CompilerParams(
            dimension_semantics=("parallel", "parallel", "arbitrary"),
            vmem_limit_bytes=VMEM_LIMIT),
        name="sb_attn",
    )(qk, qk, v, gate)


def _ret_kernel(q_ref, k_ref, v_ref, g_ref, din_ref, qd_ref, kd_ref, cd_ref, gain_ref,
                o_ref, state_scr, *, heads):
    ci = pl.program_id(1)

    @pl.when(ci == 0)
    def _():
        state_scr[...] = jnp.zeros_like(state_scr)

    for h in range(heads):
        qc = q_ref[0, h]
        kc = k_ref[0, h]
        vc = v_ref[0, h]
        state = state_scr[h]

        scores = lax.dot_general(qc, kc, (((1,), (1,)), ((), ())), preferred_element_type=F32)
        inner = jnp.dot((scores * din_ref[h]).astype(BF16), vc, preferred_element_type=F32)
        q_dec = (qc.astype(F32) * qd_ref[h]).astype(BF16)
        cross = jnp.dot(q_dec, state.astype(BF16), preferred_element_type=F32)
        k_dec = (kc.astype(F32) * kd_ref[h]).astype(BF16)
        kv = lax.dot_general(k_dec, vc, (((0,), (0,)), ((), ())), preferred_element_type=F32)
        state_scr[h] = state * cd_ref[h] + kv

        o = inner + cross
        mu = jnp.mean(o, axis=-1, keepdims=True)
        cen = o - mu
        var = jnp.mean(cen * cen, axis=-1, keepdims=True)
        y = cen * lax.rsqrt(var + EPS) * gain_ref[:, h * HEAD_DIM:(h + 1) * HEAD_DIM]
        o_ref[0, h] = (y * g_ref[0, h].astype(F32)).astype(BF16)


def _retention(qk, v, gate, heads, din, qd, kd, cd, gain):
    b, _, s, d = qk.shape
    c = RET_CHUNK
    blk = lambda group: pl.BlockSpec((1, heads, c, d), lambda bi, ci: (bi, group, ci, 0))
    const = lambda arr: pl.BlockSpec(arr.shape, lambda bi, ci: (0,) * arr.ndim)
    return pl.pallas_call(
        functools.partial(_ret_kernel, heads=heads),
        grid=(b, s // c),
        in_specs=[blk(0), blk(1), blk(1), blk(1),
                  const(din), const(qd), const(kd), const(cd), const(gain)],
        out_specs=pl.BlockSpec((1, heads, c, d), lambda bi, ci: (bi, 0, ci, 0)),
        out_shape=jax.ShapeDtypeStruct((b, heads, s, d), BF16),
        scratch_shapes=[pltpu.VMEM((heads, d, d), F32)],
        compiler_params=pltpu.CompilerParams(
            dimension_semantics=("parallel", "arbitrary"),
            vmem_limit_bytes=VMEM_LIMIT),
        name="retention",
    )(qk, qk, v, gate, din, qd, kd, cd, gain)


def _retention_tables(heads, c):
    log_gamma = np.log1p(-np.exp2(-5.0 - np.arange(heads, dtype=np.float64)))
    idx = np.arange(c, dtype=np.float64)
    rel = idx[:, None] - idx[None, :]
    din = np.where(rel[None] >= 0, np.exp(np.maximum(rel, 0.0)[None] * log_gamma[:, None, None]), 0.0)
    qd = np.exp((idx[None, :] + 1.0) * log_gamma[:, None])[..., None]
    kd = np.exp((c - 1.0 - idx[None, :]) * log_gamma[:, None])[..., None]
    cd = np.exp(c * log_gamma)[:, None, None]
    bc = lambda t: np.ascontiguousarray(
        np.broadcast_to(t, t.shape[:-1] + (HEAD_DIM,)), dtype=np.float32)
    return din.astype(np.float32), bc(qd), bc(kd), bc(cd)


def _out_kernel(a_ref, r_ref, w_ref, x_ref, mod_ref, o_ref, *, heads):
    parts = [a_ref[0, j] for j in range(heads)] + [r_ref[0, j] for j in range(heads)]
    mix = jnp.concatenate(parts, axis=-1)
    for c in range(w_ref.shape[1] // OUT_COLS):
        cols = slice(c * OUT_COLS, (c + 1) * OUT_COLS)
        y = jnp.dot(mix, w_ref[:, cols], preferred_element_type=F32)
        o_ref[0, :, cols] = x_ref[0, :, cols] + mod_ref[0, 2:3, cols] * y


def _out_proj(oa, ob, w_bf16, x, mod3):
    b, s, d = x.shape
    heads = oa.shape[1]
    tm = OUT_ROWS
    return pl.pallas_call(
        functools.partial(_out_kernel, heads=heads),
        grid=(b, s // tm),
        in_specs=[
            pl.BlockSpec((1, heads, tm, HEAD_DIM), lambda bi, si: (bi, 0, si, 0)),
            pl.BlockSpec((1, heads, tm, HEAD_DIM), lambda bi, si: (bi, 0, si, 0)),
            pl.BlockSpec(w_bf16.shape, lambda bi, si: (0, 0)),
            pl.BlockSpec((1, tm, d), lambda bi, si: (bi, si, 0)),
            pl.BlockSpec((1, 3, d), lambda bi, si: (bi, 0, 0)),
        ],
        out_specs=pl.BlockSpec((1, tm, d), lambda bi, si: (bi, si, 0)),
        out_shape=jax.ShapeDtypeStruct((b, s, d), F32),
        compiler_params=pltpu.CompilerParams(
            dimension_semantics=("parallel", "parallel"),
            vmem_limit_bytes=VMEM_LIMIT),
        name="out_proj",
    )(oa, ob, w_bf16, x, mod3)


def _rotary_tables(s):
    half = HEAD_DIM // 2
    inv_freq = ROPE_BASE ** (-np.arange(half, dtype=np.float64) / half)
    ang = np.arange(s, dtype=np.float64)[:, None] * inv_freq[None, :]
    cos, sin = np.cos(ang), np.sin(ang)
    return (np.concatenate([cos, cos], axis=-1).astype(np.float32),
            np.concatenate([-sin, sin], axis=-1).astype(np.float32))


def kernel(x, c, w_ada, b_ada, norm_gain, w_in, sb_q_gain, sb_k_gain, ret_norm_gain, w_out):
    b, s, d = x.shape
    depth = w_ada.shape[0]
    heads = w_in.shape[2] // (N_GROUPS * HEAD_DIM)
    cos_t, sin_t = _rotary_tables(s)
    din, qd, kd, cd = _retention_tables(heads, RET_CHUNK)
    c_pad = jnp.pad(c, ((0, 8 - b), (0, 0)))
    rope_spec = pl.BlockSpec((PROJ_ROWS, HEAD_DIM), lambda bi, si, g: (si, 0))
    gain_spec = pl.BlockSpec((1, 1, HEAD_DIM), lambda bi, si, g: (g, 0, 0))
    for layer in range(depth):
        mod = _adaln(c_pad, w_ada[layer], b_ada[layer][None, :])
        mod3 = mod[:b].reshape(b, 3, d)
        h = _norm_mod(x, mod3, norm_gain[layer][None, :])
        w = w_in[layer].astype(BF16)
        qk_gain = jnp.stack([sb_q_gain[layer], sb_k_gain[layer]])[:, None, :]
        sb_qk = _in_proj(h, w, "qk_norm", 0, 1, (qk_gain,), (gain_spec,),
                         first_mult=float(1.0 / np.sqrt(HEAD_DIM)))
        vals = _in_proj(h, w, "identity", 2, 4)
        gates = _in_proj(h, w, "silu", 3, 4)
        ret_qk = _in_proj(h, w, "rotary", 4, 1, (cos_t, sin_t), (rope_spec, rope_spec),
                          second_mult=float(HEAD_DIM ** -0.5))
        oa = _sb_attention(sb_qk, vals, gates, heads)
        ob = _retention(ret_qk, vals, gates, heads, din, qd, kd, cd, ret_norm_gain[layer][None, :])
        x = _out_proj(oa, ob, w_out[layer].astype(BF16), x, mod3)
    return x
```

```python
import functools

import jax
import jax.numpy as jnp
import numpy as np
from jax import lax
from jax.experimental import pallas as pl
from jax.experimental.pallas import tpu as pltpu

HEAD_DIM = 128
ROPE_BASE = 10000.0
EPS = 1e-6
N_GROUPS = 8
F32 = jnp.float32
BF16 = jnp.bfloat16

ADALN_STEPS = 16
NORM_ROWS = 1024
NORM_CHUNK = 32
NORM_UNROLL = 8
PROJ_ROWS = 2048
PROJ_UNIT_ROWS = 1024
PROJ_COLS = 256
SB_Q = 128
SB_K = 256
SB_TAIL = 128
SB_STEP_ROWS = 256
SB_HEADS_PER_STEP = 4
RET_CHUNK = 256
OUT_ROWS = 512
OUT_COLS = 256
VMEM_LIMIT = 56 * 1024 * 1024
LOG_WEIGHT_FLOOR = -120.0
LOG2E = 1.4426950408889634


def _silu(v):
    return v / (1.0 + jnp.exp(-v))


def _adaln_kernel(c_ref, w_ref, b_ref, o_ref):
    s = _silu(c_ref[...])
    o_ref[...] = jnp.dot(s, w_ref[...], preferred_element_type=F32,
                         precision=lax.Precision.HIGHEST) + b_ref[...]


def _adaln(c_pad, w, b):
    rows, d = c_pad.shape
    n = w.shape[1]
    tn = n // ADALN_STEPS
    return pl.pallas_call(
        _adaln_kernel,
        grid=(n // tn,),
        in_specs=[pl.BlockSpec((rows, d), lambda j: (0, 0)),
                  pl.BlockSpec((d, tn), lambda j: (0, j)),
                  pl.BlockSpec((1, tn), lambda j: (0, j))],
        out_specs=pl.BlockSpec((rows, tn), lambda j: (0, j)),
        out_shape=jax.ShapeDtypeStruct((rows, n), F32),
        compiler_params=pltpu.CompilerParams(
            dimension_semantics=("arbitrary",), vmem_limit_bytes=VMEM_LIMIT),
        name="adaln",
    )(c_pad, w, b)


def _norm_kernel(x_ref, mod_ref, ng_ref, h_ref):
    shift = mod_ref[0, 0:1, :]
    gain = ng_ref[...] * (1.0 + mod_ref[0, 1:2, :])

    def body(r, _):
        rows = pl.ds(pl.multiple_of(r * NORM_CHUNK, NORM_CHUNK), NORM_CHUNK)
        x = x_ref[0, rows, :]
        ms = jnp.mean(x * x, axis=-1, keepdims=True)
        h_ref[0, rows, :] = (x * lax.rsqrt(ms + EPS) * gain + shift).astype(BF16)
        return 0

    lax.fori_loop(0, x_ref.shape[1] // NORM_CHUNK, body, 0, unroll=NORM_UNROLL)


def _norm_mod(x, mod3, norm_gain):
    b, s, d = x.shape
    tm = NORM_ROWS
    return pl.pallas_call(
        _norm_kernel,
        grid=(b, s // tm),
        in_specs=[pl.BlockSpec((1, tm, d), lambda bi, si: (bi, si, 0)),
                  pl.BlockSpec((1, 3, d), lambda bi, si: (bi, 0, 0)),
                  pl.BlockSpec((1, d), lambda bi, si: (0, 0))],
        out_specs=pl.BlockSpec((1, tm, d), lambda bi, si: (bi, si, 0)),
        out_shape=jax.ShapeDtypeStruct((b, s, d), BF16),
        compiler_params=pltpu.CompilerParams(
            dimension_semantics=("parallel", "parallel"), vmem_limit_bytes=VMEM_LIMIT),
        name="norm_mod",
    )(x, mod3, norm_gain)


def _proj_kernel(h_ref, w_ref, *refs, kind, heads, first_mult, second_mult):
    o_ref = refs[-1]
    g = pl.program_id(2)
    mult = jnp.where(g == 0, first_mult, second_mult).astype(F32)
    if kind == "qk_norm":
        gain = refs[0][0] * mult

        def epilogue(a, rows):
            ms = jnp.mean(a * a, axis=-1, keepdims=True)
            return a * lax.rsqrt(ms + EPS) * gain
    elif kind == "rotary":
        def epilogue(a, rows):
            return (a * (refs[0][rows, :] * mult)
                    + pltpu.roll(a, HEAD_DIM // 2, 1) * (refs[1][rows, :] * mult))
    elif kind == "silu":
        epilogue = lambda a, rows: _silu(a)
    else:
        epilogue = lambda a, rows: a

    heads_per_chunk = PROJ_COLS // HEAD_DIM
    for r in range(h_ref.shape[1] // PROJ_UNIT_ROWS):
        rows = slice(r * PROJ_UNIT_ROWS, (r + 1) * PROJ_UNIT_ROWS)
        h = h_ref[0, rows, :]
        for c in range(heads // heads_per_chunk):
            acc = jnp.dot(h, w_ref[:, c * PROJ_COLS:(c + 1) * PROJ_COLS],
                          preferred_element_type=F32)
            for j in range(heads_per_chunk):
                a = acc[:, j * HEAD_DIM:(j + 1) * HEAD_DIM]
                o_ref[0, c * heads_per_chunk + j, rows, :] = epilogue(a, rows).astype(BF16)


def _in_proj(h, w_bf16, kind, first_group, group_stride, extra=(), extra_specs=(),
             first_mult=1.0, second_mult=1.0):
    b, s, d = h.shape
    tn = w_bf16.shape[1] // N_GROUPS
    heads = tn // HEAD_DIM
    tm = PROJ_ROWS
    return pl.pallas_call(
        functools.partial(_proj_kernel, kind=kind, heads=heads,
                          first_mult=first_mult, second_mult=second_mult),
        grid=(b, s // tm, 2),
        in_specs=[
            pl.BlockSpec((1, tm, d), lambda bi, si, g: (bi, si, 0)),
            pl.BlockSpec((d, tn), lambda bi, si, g: (0, first_group + group_stride * g)),
            *extra_specs,
        ],
        out_specs=pl.BlockSpec((1, heads, tm, HEAD_DIM), lambda bi, si, g: (bi, g, si, 0)),
        out_shape=jax.ShapeDtypeStruct((b, 2 * heads, s, HEAD_DIM), BF16),
        compiler_params=pltpu.CompilerParams(
            dimension_semantics=("parallel", "parallel", "arbitrary"),
            vmem_limit_bytes=VMEM_LIMIT),
        name="in_proj_" + kind,
    )(h, w_bf16, *extra)


def _sb_scores(q, k_blk):
    z = lax.dot_general(q, k_blk, (((1,), (1,)), ((), ())), preferred_element_type=F32)
    sp = jnp.maximum(z, 0.0) + jnp.log(1.0 + jnp.exp2(jnp.abs(z) * (-LOG2E)))
    return sp, z - sp


def _sb_later(sp, neg_upper):
    return jnp.dot(sp.astype(BF16), neg_upper, preferred_element_type=F32)


def _sb_edge_window(q, k_ref, v_ref, hh, q0, neg_upper):
    row = lax.broadcasted_iota(jnp.int32, (SB_Q, SB_K), 0)
    col = lax.broadcasted_iota(jnp.int32, (SB_Q, SB_K), 1)
    d_start = jnp.maximum(q0 - SB_Q, 0)
    p_start = jnp.maximum(d_start - SB_K, 0)
    mask_d = col < row + (q0 - d_start)
    mask_p = col < (d_start - p_start)
    d_start = pl.multiple_of(d_start, SB_Q)
    p_start = pl.multiple_of(p_start, SB_Q)

    sp_d, logb_d = _sb_scores(q, k_ref[0, hh, pl.ds(d_start, SB_K), :])
    sp_p, logb_p = _sb_scores(q, k_ref[0, hh, pl.ds(p_start, SB_K), :])
    sp_d = jnp.where(mask_d, sp_d, 0.0)
    sp_p = jnp.where(mask_p, sp_p, 0.0)
    w_d = jnp.where(mask_d, jnp.exp(logb_d + _sb_later(sp_d, neg_upper)), 0.0)
    carry = -jnp.sum(sp_d, axis=1, keepdims=True)
    w_p = jnp.where(mask_p, jnp.exp(logb_p + _sb_later(sp_p, neg_upper) + carry), 0.0)
    acc = (jnp.dot(w_d.astype(BF16), v_ref[0, hh, pl.ds(d_start, SB_K), :], preferred_element_type=F32)
           + jnp.dot(w_p.astype(BF16), v_ref[0, hh, pl.ds(p_start, SB_K), :], preferred_element_type=F32))
    carry = carry - jnp.sum(sp_p, axis=1, keepdims=True)
    return carry, acc, p_start


def _sb_interior_windows(q_ref, k_ref, v_ref, step_start, neg_upper, heads_per_step):
    subs = SB_STEP_ROWS // SB_Q
    win = 2 * SB_K
    span = win + (subs - 1) * SB_Q
    w_start = pl.multiple_of(step_start - (win - SB_Q), SB_Q)
    zs = []
    for hh in range(heads_per_step):
        z = lax.dot_general(q_ref[0, hh], k_ref[0, hh, pl.ds(w_start, span), :],
                            (((1,), (1,)), ((), ())), preferred_element_type=F32)
        zs += [z[sub * SB_Q:(sub + 1) * SB_Q, sub * SB_Q:sub * SB_Q + win] for sub in range(subs)]
    z = jnp.concatenate(zs, axis=0).astype(BF16)
    n = z.shape[0]
    l = jnp.log(1.0 + jnp.exp(-jnp.abs(z)))
    sp = jnp.maximum(z, 0.0) + l
    logb = jnp.minimum(z, 0.0) - l
    row = lax.broadcasted_iota(jnp.int32, (n, SB_Q), 0) & (SB_Q - 1)
    col = lax.broadcasted_iota(jnp.int32, (n, SB_Q), 1)
    past = jnp.where(col < row, 1.0, 0.0).astype(BF16)
    diag = lambda t: jnp.concatenate([t[:, :SB_K - SB_Q], t[:, SB_K - SB_Q:] * past], axis=1)
    sp_p, sp_d = sp[:, :SB_K], diag(sp[:, SB_K:])
    w_d = diag(jnp.exp(logb[:, SB_K:] + _sb_later(sp_d, neg_upper).astype(BF16)))
    carry = -jnp.sum(sp_d.astype(F32), axis=1, keepdims=True)
    w_p = jnp.exp(logb[:, :SB_K] + (_sb_later(sp_p, neg_upper) + carry).astype(BF16))
    carry = carry - jnp.sum(sp_p.astype(F32), axis=1, keepdims=True)
    w = jnp.concatenate([w_p, w_d], axis=1)
    carries, accs = [], []
    for hh in range(heads_per_step):
        for sub in range(subs):
            c = hh * subs + sub
            v_win = v_ref[0, hh, pl.ds(pl.multiple_of(w_start + sub * SB_Q, SB_Q), win), :]
            accs.append(jnp.dot(w[c * SB_Q:(c + 1) * SB_Q], v_win, preferred_element_type=F32))
            carries.append(carry[c * SB_Q:(c + 1) * SB_Q])
    return carries, accs, w_start


def _sb_kernel(q_ref, k_ref, v_ref, g_ref, o_ref, *, heads_per_step):
    i = pl.program_id(2)
    urow = lax.broadcasted_iota(jnp.int32, (SB_K, SB_K), 0)
    ucol = lax.broadcasted_iota(jnp.int32, (SB_K, SB_K), 1)
    neg_upper = jnp.where(urow > ucol, -1.0, 0.0).astype(BF16)
    neg_upper_tail = neg_upper[:SB_TAIL, :SB_TAIL]
    subs = SB_STEP_ROWS // SB_Q

    def run(edge):
        step_start = i * SB_STEP_ROWS
        if not edge:
            carries, accs, w_start = _sb_interior_windows(
                q_ref, k_ref, v_ref, step_start, neg_upper, heads_per_step)
        chains = []
        for hh in range(heads_per_step):
            for sub in range(subs):
                rows = slice(sub * SB_Q, (sub + 1) * SB_Q)
                q = q_ref[0, hh, rows, :]
                if edge:
                    carry, acc, p_start = _sb_edge_window(
                        q, k_ref, v_ref, hh, step_start + sub * SB_Q, neg_upper)
                else:
                    carry, acc = carries[hh * subs + sub], accs[hh * subs + sub]
                    p_start = w_start + sub * SB_Q
                chains.append((hh, rows, q, carry, acc, p_start,
                               jnp.max(carry) > LOG_WEIGHT_FLOOR))

        for hh, rows, q, carry, acc, p_start, live in chains:
            tiles_left = p_start // SB_TAIL

            def cond(state, tiles_left=tiles_left):
                jj, live, _, _ = state
                return (jj < tiles_left) & live

            def body(state, hh=hh, q=q, p_start=p_start):
                jj, _, carry, acc = state
                start = pl.multiple_of(p_start - (jj + 1) * SB_TAIL, SB_TAIL)
                sp, logb = _sb_scores(q, k_ref[0, hh, pl.ds(start, SB_TAIL), :])
                w = jnp.exp(logb + _sb_later(sp, neg_upper_tail) + carry)
                acc = acc + jnp.dot(w.astype(BF16), v_ref[0, hh, pl.ds(start, SB_TAIL), :],
                                    preferred_element_type=F32)
                carry = carry - jnp.sum(sp, axis=1, keepdims=True)
                return jj + 1, jnp.max(carry) > LOG_WEIGHT_FLOOR, carry, acc

            _, _, _, acc = lax.while_loop(cond, body, (jnp.int32(0), live, carry, acc))
            o_ref[0, hh, rows, :] = (acc * g_ref[0, hh, rows, :].astype(F32)).astype(BF16)

    first_interior = (SB_Q + SB_K + SB_STEP_ROWS - 1) // SB_STEP_ROWS
    pl.when(i >= first_interior)(lambda: run(False))
    pl.when(i < first_interior)(lambda: run(True))


def _sb_attention(qk, v, gate, heads):
    b, _, s, d = qk.shape
    hb = SB_HEADS_PER_STEP
    groups = heads // hb
    t = SB_STEP_ROWS
    return pl.pallas_call(
        functools.partial(_sb_kernel, heads_per_step=hb),
        grid=(b, groups, s // t),
        in_specs=[
            pl.BlockSpec((1, hb, t, d), lambda bi, g, i: (bi, g, i, 0)),
            pl.BlockSpec((1, hb, s, d), lambda bi, g, i: (bi, groups + g, 0, 0)),
            pl.BlockSpec((1, hb, s, d), lambda bi, g, i: (bi, g, 0, 0)),
            pl.BlockSpec((1, hb, t, d), lambda bi, g, i: (bi, g, i, 0)),
        ],
        out_specs=pl.BlockSpec((1, hb, t, d), lambda bi, g, i: (bi, g, i, 0)),
        out_shape=jax.ShapeDtypeStruct((b, heads, s, d), BF16),
        compiler_params=pltpu.CompilerParams(
            dimension_semantics=("parallel", "parallel", "arbitrary"),
            vmem_limit_bytes=VMEM_LIMIT),
        name="sb_attn",
    )(qk, qk, v, gate)


def _ret_kernel(q_ref, k_ref, v_ref, g_ref, din_ref, qd_ref, kd_ref, cd_ref, gain_ref,
                o_ref, state_scr, *, heads):
    ci = pl.program_id(1)

    @pl.when(ci == 0)
    def _():
        state_scr[...] = jnp.zeros_like(state_scr)

    for h in range(heads):
        qc = q_ref[0, h]
        kc = k_ref[0, h]
        vc = v_ref[0, h]
        state = state_scr[h]

        scores = lax.dot_general(qc, kc, (((1,), (1,)), ((), ())), preferred_element_type=F32)
        inner = jnp.dot((scores * din_ref[h]).astype(BF16), vc, preferred_element_type=F32)
        q_dec = (qc.astype(F32) * qd_ref[h]).astype(BF16)
        cross = jnp.dot(q_dec, state.astype(BF16), preferred_element_type=F32)
        k_dec = (kc.astype(F32) * kd_ref[h]).astype(BF16)
        kv = lax.dot_general(k_dec, vc, (((0,), (0,)), ((), ())), preferred_element_type=F32)
        state_scr[h] = state * cd_ref[h] + kv

        o = inner + cross
        mu = jnp.mean(o, axis=-1, keepdims=True)
        cen = o - mu
        var = jnp.mean(cen * cen, axis=-1, keepdims=True)
        y = cen * lax.rsqrt(var + EPS) * gain_ref[:, h * HEAD_DIM:(h + 1) * HEAD_DIM]
        o_ref[0, h] = (y * g_ref[0, h].astype(F32)).astype(BF16)


def _retention(qk, v, gate, heads, din, qd, kd, cd, gain):
    b, _, s, d = qk.shape
    c = RET_CHUNK
    blk = lambda group: pl.BlockSpec((1, heads, c, d), lambda bi, ci: (bi, group, ci, 0))
    const = lambda arr: pl.BlockSpec(arr.shape, lambda bi, ci: (0,) * arr.ndim)
    return pl.pallas_call(
        functools.partial(_ret_kernel, heads=heads),
        grid=(b, s // c),
        in_specs=[blk(0), blk(1), blk(1), blk(1),
                  const(din), const(qd), const(kd), const(cd), const(gain)],
        out_specs=pl.BlockSpec((1, heads, c, d), lambda bi, ci: (bi, 0, ci, 0)),
        out_shape=jax.ShapeDtypeStruct((b, heads, s, d), BF16),
        scratch_shapes=[pltpu.VMEM((heads, d, d), F32)],
        compiler_params=pltpu.CompilerParams(
            dimension_semantics=("parallel", "arbitrary"),
            vmem_limit_bytes=VMEM_LIMIT),
        name="retention",
    )(qk, qk, v, gate, din, qd, kd, cd, gain)


def _retention_tables(heads, c):
    log_gamma = np.log1p(-np.exp2(-5.0 - np.arange(heads, dtype=np.float64)))
    idx = np.arange(c, dtype=np.float64)
    rel = idx[:, None] - idx[None, :]
    din = np.where(rel[None] >= 0, np.exp(np.maximum(rel, 0.0)[None] * log_gamma[:, None, None]), 0.0)
    qd = np.exp((idx[None, :] + 1.0) * log_gamma[:, None])[..., None]
    kd = np.exp((c - 1.0 - idx[None, :]) * log_gamma[:, None])[..., None]
    cd = np.exp(c * log_gamma)[:, None, None]
    bc = lambda t: np.ascontiguousarray(
        np.broadcast_to(t, t.shape[:-1] + (HEAD_DIM,)), dtype=np.float32)
    return din.astype(np.float32), bc(qd), bc(kd), bc(cd)


def _out_kernel(a_ref, r_ref, w_ref, x_ref, mod_ref, o_ref, *, heads):
    parts = [a_ref[0, j] for j in range(heads)] + [r_ref[0, j] for j in range(heads)]
    mix = jnp.concatenate(parts, axis=-1)
    for c in range(w_ref.shape[1] // OUT_COLS):
        cols = slice(c * OUT_COLS, (c + 1) * OUT_COLS)
        y = jnp.dot(mix, w_ref[:, cols], preferred_element_type=F32)
        o_ref[0, :, cols] = x_ref[0, :, cols] + mod_ref[0, 2:3, cols] * y


def _out_proj(oa, ob, w_bf16, x, mod3):
    b, s, d = x.shape
    heads = oa.shape[1]
    tm = OUT_ROWS
    return pl.pallas_call(
        functools.partial(_out_kernel, heads=heads),
        grid=(b, s // tm),
        in_specs=[
            pl.BlockSpec((1, heads, tm, HEAD_DIM), lambda bi, si: (bi, 0, si, 0)),
            pl.BlockSpec((1, heads, tm, HEAD_DIM), lambda bi, si: (bi, 0, si, 0)),
            pl.BlockSpec(w_bf16.shape, lambda bi, si: (0, 0)),
            pl.BlockSpec((1, tm, d), lambda bi, si: (bi, si, 0)),
            pl.BlockSpec((1, 3, d), lambda bi, si: (bi, 0, 0)),
        ],
        out_specs=pl.BlockSpec((1, tm, d), lambda bi, si: (bi, si, 0)),
        out_shape=jax.ShapeDtypeStruct((b, s, d), F32),
        compiler_params=pltpu.CompilerParams(
            dimension_semantics=("parallel", "parallel"),
            vmem_limit_bytes=VMEM_LIMIT),
        name="out_proj",
    )(oa, ob, w_bf16, x, mod3)


def _rotary_tables(s):
    half = HEAD_DIM // 2
    inv_freq = ROPE_BASE ** (-np.arange(half, dtype=np.float64) / half)
    ang = np.arange(s, dtype=np.float64)[:, None] * inv_freq[None, :]
    cos, sin = np.cos(ang), np.sin(ang)
    return (np.concatenate([cos, cos], axis=-1).astype(np.float32),
            np.concatenate([-sin, sin], axis=-1).astype(np.float32))


def kernel(x, c, w_ada, b_ada, norm_gain, w_in, sb_q_gain, sb_k_gain, ret_norm_gain, w_out):
    b, s, d = x.shape
    depth = w_ada.shape[0]
    heads = w_in.shape[2] // (N_GROUPS * HEAD_DIM)
    cos_t, sin_t = _rotary_tables(s)
    din, qd, kd, cd = _retention_tables(heads, RET_CHUNK)
    c_pad = jnp.pad(c, ((0, 8 - b), (0, 0)))
    rope_spec = pl.BlockSpec((PROJ_ROWS, HEAD_DIM), lambda bi, si, g: (si, 0))
    gain_spec = pl.BlockSpec((1, 1, HEAD_DIM), lambda bi, si, g: (g, 0, 0))
    for layer in range(depth):
        mod = _adaln(c_pad, w_ada[layer], b_ada[layer][None, :])
        mod3 = mod[:b].reshape(b, 3, d)
        h = _norm_mod(x, mod3, norm_gain[layer][None, :])
        w = w_in[layer].astype(BF16)
        qk_gain = jnp.stack([sb_q_gain[layer], sb_k_gain[layer]])[:, None, :]
        sb_qk = _in_proj(h, w, "qk_norm", 0, 1, (qk_gain,), (gain_spec,),
                         first_mult=float(1.0 / np.sqrt(HEAD_DIM)))
        vals = _in_proj(h, w, "identity", 2, 4)
        gates = _in_proj(h, w, "silu", 3, 4)
        ret_qk = _in_proj(h, w, "rotary", 4, 1, (cos_t, sin_t), (rope_spec, rope_spec),
                          second_mult=float(HEAD_DIM ** -0.5))
        oa = _sb_attention(sb_qk, vals, gates, heads)
        ob = _retention(ret_qk, vals, gates, heads, din, qd, kd, cd, ret_norm_gain[layer][None, :])
        x = _out_proj(oa, ob, w_out[layer].astype(BF16), x, mod3)
    return x
```

```python
import functools

import jax
import jax.numpy as jnp
import numpy as np
from jax import lax
from jax.experimental import pallas as pl
from jax.experimental.pallas import tpu as pltpu

HEAD_DIM = 128
ROPE_BASE = 10000.0
EPS = 1e-6
N_GROUPS = 8
F32 = jnp.float32
BF16 = jnp.bfloat16

ADALN_STEPS = 16
NORM_ROWS = 1024
NORM_UNIT_ROWS = 512
NORM_CHUNK = 32
PROJ_ROWS = 2048
PROJ_UNIT_ROWS = 1024
PROJ_COLS = 256
SB_Q = 128
SB_K = 256
SB_TAIL = 128
SB_STEP_ROWS = 256
SB_HEADS_PER_STEP = 4
RET_CHUNK = 256
OUT_ROWS = 512
OUT_COLS = 256
VMEM_LIMIT = 56 * 1024 * 1024
LOG_WEIGHT_FLOOR = -120.0
LOG2E = 1.4426950408889634


def _silu(v):
    return v / (1.0 + jnp.exp(-v))


def _adaln_kernel(c_ref, w_ref, b_ref, o_ref):
    s = _silu(c_ref[...])
    o_ref[...] = jnp.dot(s, w_ref[...], preferred_element_type=F32,
                         precision=lax.Precision.HIGHEST) + b_ref[...]


def _adaln(c_pad, w, b):
    rows, d = c_pad.shape
    n = w.shape[1]
    tn = n // ADALN_STEPS
    return pl.pallas_call(
        _adaln_kernel,
        grid=(n // tn,),
        in_specs=[pl.BlockSpec((rows, d), lambda j: (0, 0)),
                  pl.BlockSpec((d, tn), lambda j: (0, j)),
                  pl.BlockSpec((1, tn), lambda j: (0, j))],
        out_specs=pl.BlockSpec((rows, tn), lambda j: (0, j)),
        out_shape=jax.ShapeDtypeStruct((rows, n), F32),
        compiler_params=pltpu.CompilerParams(
            dimension_semantics=("arbitrary",), vmem_limit_bytes=VMEM_LIMIT),
        name="adaln",
    )(c_pad, w, b)


def _norm_qk_kernel(x_ref, mod_ref, ng_ref, w_ref, gain_ref, h_ref, o_ref, *, heads, q_mult):
    shift = mod_ref[0, 0:1, :]
    gain_h = ng_ref[...] * (1.0 + mod_ref[0, 1:2, :])
    heads_per_chunk = PROJ_COLS // HEAD_DIM
    for r in range(x_ref.shape[1] // NORM_UNIT_ROWS):
        for ch in range(NORM_UNIT_ROWS // NORM_CHUNK):
            rows = slice(r * NORM_UNIT_ROWS + ch * NORM_CHUNK,
                         r * NORM_UNIT_ROWS + (ch + 1) * NORM_CHUNK)
            x = x_ref[0, rows, :]
            ms = jnp.mean(x * x, axis=-1, keepdims=True)
            h_ref[0, rows, :] = (x * lax.rsqrt(ms + EPS) * gain_h + shift).astype(BF16)
        rows = slice(r * NORM_UNIT_ROWS, (r + 1) * NORM_UNIT_ROWS)
        h = h_ref[0, rows, :]
        for c in range(2 * heads // heads_per_chunk):
            acc = jnp.dot(h, w_ref[:, c * PROJ_COLS:(c + 1) * PROJ_COLS],
                          preferred_element_type=F32)
            for j in range(heads_per_chunk):
                head = c * heads_per_chunk + j
                gain = gain_ref[head // heads] * (q_mult if head < heads else 1.0)
                a = acc[:, j * HEAD_DIM:(j + 1) * HEAD_DIM]
                ms = jnp.mean(a * a, axis=-1, keepdims=True)
                o_ref[0, head, rows, :] = (a * lax.rsqrt(ms + EPS) * gain).astype(BF16)


def _norm_qk(x, mod3, norm_gain, w_bf16, qk_gain, heads, q_mult):
    b, s, d = x.shape
    tm = NORM_ROWS
    cols = 2 * heads * HEAD_DIM
    return pl.pallas_call(
        functools.partial(_norm_qk_kernel, heads=heads, q_mult=q_mult),
        grid=(b, s // tm),
        in_specs=[pl.BlockSpec((1, tm, d), lambda bi, si: (bi, si, 0)),
                  pl.BlockSpec((1, 3, d), lambda bi, si: (bi, 0, 0)),
                  pl.BlockSpec((1, d), lambda bi, si: (0, 0)),
                  pl.BlockSpec((d, cols), lambda bi, si: (0, 0), pipeline_mode=pl.Buffered(1)),
                  pl.BlockSpec(qk_gain.shape, lambda bi, si: (0, 0, 0))],
        out_specs=[pl.BlockSpec((1, tm, d), lambda bi, si: (bi, si, 0)),
                   pl.BlockSpec((1, 2 * heads, tm, HEAD_DIM), lambda bi, si: (bi, 0, si, 0))],
        out_shape=[jax.ShapeDtypeStruct((b, s, d), BF16),
                   jax.ShapeDtypeStruct((b, 2 * heads, s, HEAD_DIM), BF16)],
        compiler_params=pltpu.CompilerParams(
            dimension_semantics=("parallel", "parallel"), vmem_limit_bytes=VMEM_LIMIT),
        name="norm_qk",
    )(x, mod3, norm_gain, w_bf16, qk_gain)


def _proj_kernel(h_ref, w_ref, *refs, kind, heads, first_mult, second_mult):
    o_ref = refs[-1]
    g = pl.program_id(2)
    mult = jnp.where(g == 0, first_mult, second_mult).astype(F32)
    if kind == "rotary":
        def epilogue(a, rows):
            return (a * (refs[0][rows, :] * mult)
                    + pltpu.roll(a, HEAD_DIM // 2, 1) * (refs[1][rows, :] * mult))
    elif kind == "silu":
        epilogue = lambda a, rows: _silu(a)
    else:
        epilogue = lambda a, rows: a

    heads_per_chunk = PROJ_COLS // HEAD_DIM
    for r in range(h_ref.shape[1] // PROJ_UNIT_ROWS):
        rows = slice(r * PROJ_UNIT_ROWS, (r + 1) * PROJ_UNIT_ROWS)
        h = h_ref[0, rows, :]
        for c in range(heads // heads_per_chunk):
            acc = jnp.dot(h, w_ref[:, c * PROJ_COLS:(c + 1) * PROJ_COLS],
                          preferred_element_type=F32)
            for j in range(heads_per_chunk):
                a = acc[:, j * HEAD_DIM:(j + 1) * HEAD_DIM]
                o_ref[0, c * heads_per_chunk + j, rows, :] = epilogue(a, rows).astype(BF16)


def _in_proj(h, w_bf16, kind, first_group, group_stride, extra=(), extra_specs=(),
             first_mult=1.0, second_mult=1.0):
    b, s, d = h.shape
    tn = w_bf16.shape[1] // N_GROUPS
    heads = tn // HEAD_DIM
    tm = PROJ_ROWS
    return pl.pallas_call(
        functools.partial(_proj_kernel, kind=kind, heads=heads,
                          first_mult=first_mult, second_mult=second_mult),
        grid=(b, s // tm, 2),
        in_specs=[
            pl.BlockSpec((1, tm, d), lambda bi, si, g: (bi, si, 0)),
            pl.BlockSpec((d, tn), lambda bi, si, g: (0, first_group + group_stride * g)),
            *extra_specs,
        ],
        out_specs=pl.BlockSpec((1, heads, tm, HEAD_DIM), lambda bi, si, g: (bi, g, si, 0)),
        out_shape=jax.ShapeDtypeStruct((b, 2 * heads, s, HEAD_DIM), BF16),
        compiler_params=pltpu.CompilerParams(
            dimension_semantics=("parallel", "parallel", "arbitrary"),
            vmem_limit_bytes=VMEM_LIMIT),
        name="in_proj_" + kind,
    )(h, w_bf16, *extra)


def _sb_scores(q, k_blk):
    z = lax.dot_general(q, k_blk, (((1,), (1,)), ((), ())), preferred_element_type=F32)
    sp = jnp.maximum(z, 0.0) + jnp.log(1.0 + jnp.exp2(jnp.abs(z) * (-LOG2E)))
    return sp, z - sp


def _sb_later(sp, neg_upper):
    return jnp.dot(sp.astype(BF16), neg_upper, preferred_element_type=F32)


def _sb_edge_window(q, k_ref, v_ref, hh, q0, neg_upper):
    row = lax.broadcasted_iota(jnp.int32, (SB_Q, SB_K), 0)
    col = lax.broadcasted_iota(jnp.int32, (SB_Q, SB_K), 1)
    d_start = jnp.maximum(q0 - SB_Q, 0)
    p_start = jnp.maximum(d_start - SB_K, 0)
    mask_d = col < row + (q0 - d_start)
    mask_p = col < (d_start - p_start)
    d_start = pl.multiple_of(d_start, SB_Q)
    p_start = pl.multiple_of(p_start, SB_Q)

    sp_d, logb_d = _sb_scores(q, k_ref[0, hh, pl.ds(d_start, SB_K), :])
    sp_p, logb_p = _sb_scores(q, k_ref[0, hh, pl.ds(p_start, SB_K), :])
    sp_d = jnp.where(mask_d, sp_d, 0.0)
    sp_p = jnp.where(mask_p, sp_p, 0.0)
    w_d = jnp.where(mask_d, jnp.exp(logb_d + _sb_later(sp_d, neg_upper)), 0.0)
    carry = -jnp.sum(sp_d, axis=1, keepdims=True)
    w_p = jnp.where(mask_p, jnp.exp(logb_p + _sb_later(sp_p, neg_upper) + carry), 0.0)
    acc = (jnp.dot(w_d.astype(BF16), v_ref[0, hh, pl.ds(d_start, SB_K), :], preferred_element_type=F32)
           + jnp.dot(w_p.astype(BF16), v_ref[0, hh, pl.ds(p_start, SB_K), :], preferred_element_type=F32))
    carry = carry - jnp.sum(sp_p, axis=1, keepdims=True)
    return carry, acc, p_start


def _sb_interior_windows(q_ref, k_ref, v_ref, step_start, neg_upper, heads_per_step):
    subs = SB_STEP_ROWS // SB_Q
    win = 2 * SB_K
    span = win + (subs - 1) * SB_Q
    w_start = pl.multiple_of(step_start - (win - SB_Q), SB_Q)
    zs = []
    for hh in range(heads_per_step):
        z = lax.dot_general(q_ref[0, hh], k_ref[0, hh, pl.ds(w_start, span), :],
                            (((1,), (1,)), ((), ())), preferred_element_type=F32)
        zs += [z[sub * SB_Q:(sub + 1) * SB_Q, sub * SB_Q:sub * SB_Q + win] for sub in range(subs)]
    z = jnp.concatenate(zs, axis=0).astype(BF16)
    n = z.shape[0]
    l = jnp.log(1.0 + jnp.exp(-jnp.abs(z)))
    sp = jnp.maximum(z, 0.0) + l
    logb = jnp.minimum(z, 0.0) - l
    row = lax.broadcasted_iota(jnp.int32, (n, SB_Q), 0) & (SB_Q - 1)
    col = lax.broadcasted_iota(jnp.int32, (n, SB_Q), 1)
    past = jnp.where(col < row, 1.0, 0.0).astype(BF16)
    diag = lambda t: jnp.concatenate([t[:, :SB_K - SB_Q], t[:, SB_K - SB_Q:] * past], axis=1)
    sp_p, sp_d = sp[:, :SB_K], diag(sp[:, SB_K:])
    w_d = diag(jnp.exp(logb[:, SB_K:] + _sb_later(sp_d, neg_upper).astype(BF16)))
    carry = -jnp.sum(sp_d.astype(F32), axis=1, keepdims=True)
    w_p = jnp.exp(logb[:, :SB_K] + (_sb_later(sp_p, neg_upper) + carry).astype(BF16))
    carry = carry - jnp.sum(sp_p.astype(F32), axis=1, keepdims=True)
    w = jnp.concatenate([w_p, w_d], axis=1)
    carries, accs = [], []
    for hh in range(heads_per_step):
        for sub in range(subs):
            c = hh * subs + sub
            v_win = v_ref[0, hh, pl.ds(pl.multiple_of(w_start + sub * SB_Q, SB_Q), win), :]
            accs.append(jnp.dot(w[c * SB_Q:(c + 1) * SB_Q], v_win, preferred_element_type=F32))
            carries.append(carry[c * SB_Q:(c + 1) * SB_Q])
    return carries, accs, w_start


def _sb_kernel(q_ref, k_ref, v_ref, g_ref, o_ref, *, heads_per_step):
    i = pl.program_id(2)
    urow = lax.broadcasted_iota(jnp.int32, (SB_K, SB_K), 0)
    ucol = lax.broadcasted_iota(jnp.int32, (SB_K, SB_K), 1)
    neg_upper = jnp.where(urow > ucol, -1.0, 0.0).astype(BF16)
    neg_upper_tail = neg_upper[:SB_TAIL, :SB_TAIL]
    subs = SB_STEP_ROWS // SB_Q

    def run(edge):
        step_start = i * SB_STEP_ROWS
        if not edge:
            carries, accs, w_start = _sb_interior_windows(
                q_ref, k_ref, v_ref, step_start, neg_upper, heads_per_step)
        chains = []
        for hh in range(heads_per_step):
            for sub in range(subs):
                rows = slice(sub * SB_Q, (sub + 1) * SB_Q)
                q = q_ref[0, hh, rows, :]
                if edge:
                    carry, acc, p_start = _sb_edge_window(
                        q, k_ref, v_ref, hh, step_start + sub * SB_Q, neg_upper)
                else:
                    carry, acc = carries[hh * subs + sub], accs[hh * subs + sub]
                    p_start = w_start + sub * SB_Q
                chains.append((hh, rows, q, carry, acc, p_start,
                               jnp.max(carry) > LOG_WEIGHT_FLOOR))

        for hh, rows, q, carry, acc, p_start, live in chains:
            tiles_left = p_start // SB_TAIL

            def cond(state, tiles_left=tiles_left):
                jj, live, _, _ = state
                return (jj < tiles_left) & live

            def body(state, hh=hh, q=q, p_start=p_start):
                jj, _, carry, acc = state
                start = pl.multiple_of(p_start - (jj + 1) * SB_TAIL, SB_TAIL)
                sp, logb = _sb_scores(q, k_ref[0, hh, pl.ds(start, SB_TAIL), :])
                w = jnp.exp(logb + _sb_later(sp, neg_upper_tail) + carry)
                acc = acc + jnp.dot(w.astype(BF16), v_ref[0, hh, pl.ds(start, SB_TAIL), :],
                                    preferred_element_type=F32)
                carry = carry - jnp.sum(sp, axis=1, keepdims=True)
                return jj + 1, jnp.max(carry) > LOG_WEIGHT_FLOOR, carry, acc

            _, _, _, acc = lax.while_loop(cond, body, (jnp.int32(0), live, carry, acc))
            o_ref[0, hh, rows, :] = (acc * g_ref[0, hh, rows, :].astype(F32)).astype(BF16)

    first_interior = (SB_Q + SB_K + SB_STEP_ROWS - 1) // SB_STEP_ROWS
    pl.when(i >= first_interior)(lambda: run(False))
    pl.when(i < first_interior)(lambda: run(True))


def _sb_attention(qk, v, gate, heads):
    b, _, s, d = qk.shape
    hb = SB_HEADS_PER_STEP
    groups = heads // hb
    t = SB_STEP_ROWS
    return pl.pallas_call(
        functools.partial(_sb_kernel, heads_per_step=hb),
        grid=(b, groups, s // t),
        in_specs=[
            pl.BlockSpec((1, hb, t, d), lambda bi, g, i: (bi, g, i, 0)),
            pl.BlockSpec((1, hb, s, d), lambda bi, g, i: (bi, groups + g, 0, 0)),
            pl.BlockSpec((1, hb, s, d), lambda bi, g, i: (bi, g, 0, 0)),
            pl.BlockSpec((1, hb, t, d), lambda bi, g, i: (bi, g, i, 0)),
        ],
        out_specs=pl.BlockSpec((1, hb, t, d), lambda bi, g, i: (bi, g, i, 0)),
        out_shape=jax.ShapeDtypeStruct((b, heads, s, d), BF16),
        compiler_params=pltpu.CompilerParams(
            dimension_semantics=("parallel", "parallel", "arbitrary"),
            vmem_limit_bytes=VMEM_LIMIT),
        name="sb_attn",
    )(qk, qk, v, gate)


def _ret_kernel(q_ref, k_ref, v_ref, g_ref, din_ref, qd_ref, kd_ref, cd_ref, gain_ref,
                o_ref, state_scr, *, heads):
    ci = pl.program_id(1)

    @pl.when(ci == 0)
    def _():
        state_scr[...] = jnp.zeros_like(state_scr)

    for h in range(heads):
        qc = q_ref[0, h]
        kc = k_ref[0, h]
        vc = v_ref[0, h]
        state = state_scr[h]

        scores = lax.dot_general(qc, kc, (((1,), (1,)), ((), ())), preferred_element_type=F32)
        inner = jnp.dot((scores * din_ref[h]).astype(BF16), vc, preferred_element_type=F32)
        q_dec = (qc.astype(F32) * qd_ref[h]).astype(BF16)
        cross = jnp.dot(q_dec, state.astype(BF16), preferred_element_type=F32)
        k_dec = (kc.astype(F32) * kd_ref[h]).astype(BF16)
        kv = lax.dot_general(k_dec, vc, (((0,), (0,)), ((), ())), preferred_element_type=F32)
        state_scr[h] = state * cd_ref[h] + kv

        o = inner + cross
        mu = jnp.mean(o, axis=-1, keepdims=True)
        cen = o - mu
        var = jnp.mean(cen * cen, axis=-1, keepdims=True)
        y = cen * lax.rsqrt(var + EPS) * gain_ref[:, h * HEAD_DIM:(h + 1) * HEAD_DIM]
        o_ref[0, h] = (y * g_ref[0, h].astype(F32)).astype(BF16)


def _retention(qk, v, gate, heads, din, qd, kd, cd, gain):
    b, _, s, d = qk.shape
    c = RET_CHUNK
    blk = lambda group: pl.BlockSpec((1, heads, c, d), lambda bi, ci: (bi, group, ci, 0))
    const = lambda arr: pl.BlockSpec(arr.shape, lambda bi, ci: (0,) * arr.ndim)
    return pl.pallas_call(
        functools.partial(_ret_kernel, heads=heads),
        grid=(b, s // c),
        in_specs=[blk(0), blk(1), blk(1), blk(1),
                  const(din), const(qd), const(kd), const(cd), const(gain)],
        out_specs=pl.BlockSpec((1, heads, c, d), lambda bi, ci: (bi, 0, ci, 0)),
        out_shape=jax.ShapeDtypeStruct((b, heads, s, d), BF16),
        scratch_shapes=[pltpu.VMEM((heads, d, d), F32)],
        compiler_params=pltpu.CompilerParams(
            dimension_semantics=("parallel", "arbitrary"),
            vmem_limit_bytes=VMEM_LIMIT),
        name="retention",
    )(qk, qk, v, gate, din, qd, kd, cd, gain)


def _retention_tables(heads, c):
    log_gamma = np.log1p(-np.exp2(-5.0 - np.arange(heads, dtype=np.float64)))
    idx = np.arange(c, dtype=np.float64)
    rel = idx[:, None] - idx[None, :]
    din = np.where(rel[None] >= 0, np.exp(np.maximum(rel, 0.0)[None] * log_gamma[:, None, None]), 0.0)
    qd = np.exp((idx[None, :] + 1.0) * log_gamma[:, None])[..., None]
    kd = np.exp((c - 1.0 - idx[None, :]) * log_gamma[:, None])[..., None]
    cd = np.exp(c * log_gamma)[:, None, None]
    bc = lambda t: np.ascontiguousarray(
        np.broadcast_to(t, t.shape[:-1] + (HEAD_DIM,)), dtype=np.float32)
    return din.astype(np.float32), bc(qd), bc(kd), bc(cd)


def _out_kernel(a_ref, r_ref, w_ref, x_ref, mod_ref, o_ref, *, heads):
    parts = [a_ref[0, j] for j in range(heads)] + [r_ref[0, j] for j in range(heads)]
    mix = jnp.concatenate(parts, axis=-1)
    for c in range(w_ref.shape[1] // OUT_COLS):
        cols = slice(c * OUT_COLS, (c + 1) * OUT_COLS)
        y = jnp.dot(mix, w_ref[:, cols], preferred_element_type=F32)
        o_ref[0, :, cols] = x_ref[0, :, cols] + mod_ref[0, 2:3, cols] * y


def _out_proj(oa, ob, w_bf16, x, mod3):
    b, s, d = x.shape
    heads = oa.shape[1]
    tm = OUT_ROWS
    return pl.pallas_call(
        functools.partial(_out_kernel, heads=heads),
        grid=(b, s // tm),
        in_specs=[
            pl.BlockSpec((1, heads, tm, HEAD_DIM), lambda bi, si: (bi, 0, si, 0)),
            pl.BlockSpec((1, heads, tm, HEAD_DIM), lambda bi, si: (bi, 0, si, 0)),
            pl.BlockSpec(w_bf16.shape, lambda bi, si: (0, 0)),
            pl.BlockSpec((1, tm, d), lambda bi, si: (bi, si, 0)),
            pl.BlockSpec((1, 3, d), lambda bi, si: (bi, 0, 0)),
        ],
        out_specs=pl.BlockSpec((1, tm, d), lambda bi, si: (bi, si, 0)),
        out_shape=jax.ShapeDtypeStruct((b, s, d), F32),
        compiler_params=pltpu.CompilerParams(
            dimension_semantics=("parallel", "parallel"),
            vmem_limit_bytes=VMEM_LIMIT),
        name="out_proj",
    )(oa, ob, w_bf16, x, mod3)


def _rotary_tables(s):
    half = HEAD_DIM // 2
    inv_freq = ROPE_BASE ** (-np.arange(half, dtype=np.float64) / half)
    ang = np.arange(s, dtype=np.float64)[:, None] * inv_freq[None, :]
    cos, sin = np.cos(ang), np.sin(ang)
    return (np.concatenate([cos, cos], axis=-1).astype(np.float32),
            np.concatenate([-sin, sin], axis=-1).astype(np.float32))


def kernel(x, c, w_ada, b_ada, norm_gain, w_in, sb_q_gain, sb_k_gain, ret_norm_gain, w_out):
    b, s, d = x.shape
    depth = w_ada.shape[0]
    heads = w_in.shape[2] // (N_GROUPS * HEAD_DIM)
    cos_t, sin_t = _rotary_tables(s)
    din, qd, kd, cd = _retention_tables(heads, RET_CHUNK)
    c_pad = jnp.pad(c, ((0, 8 - b), (0, 0)))
    rope_spec = pl.BlockSpec((PROJ_ROWS, HEAD_DIM), lambda bi, si, g: (si, 0))
    for layer in range(depth):
        mod = _adaln(c_pad, w_ada[layer], b_ada[layer][None, :])
        mod3 = mod[:b].reshape(b, 3, d)
        w = w_in[layer].astype(BF16)
        qk_gain = jnp.stack([sb_q_gain[layer], sb_k_gain[layer]])[:, None, :]
        h, sb_qk = _norm_qk(x, mod3, norm_gain[layer][None, :], w, qk_gain, heads,
                            q_mult=float(1.0 / np.sqrt(HEAD_DIM)))
        vals = _in_proj(h, w, "identity", 2, 4)
        gates = _in_proj(h, w, "silu", 3, 4)
        ret_qk = _in_proj(h, w, "rotary", 4, 1, (cos_t, sin_t), (rope_spec, rope_spec),
                          second_mult=float(HEAD_DIM ** -0.5))
        oa = _sb_attention(sb_qk, vals, gates, heads)
        ob = _retention(ret_qk, vals, gates, heads, din, qd, kd, cd, ret_norm_gain[layer][None, :])
        x = _out_proj(oa, ob, w_out[layer].astype(BF16), x, mod3)
    return x
```

```python
import functools

import jax
import jax.numpy as jnp
import numpy as np
from jax import lax
from jax.experimental import pallas as pl
from jax.experimental.pallas import tpu as pltpu

HEAD_DIM = 128
ROPE_BASE = 10000.0
EPS = 1e-6
N_GROUPS = 8
F32 = jnp.float32
BF16 = jnp.bfloat16

ADALN_STEPS = 16
NORM_ROWS = 1024
NORM_UNIT_ROWS = 512
NORM_CHUNK = 32
PROJ_ROWS = 2048
PROJ_UNIT_ROWS = 1024
PROJ_COLS = 256
SB_Q = 128
SB_K = 256
SB_TAIL = 128
SB_QK_ROWS = 256
SB_STEP_ROWS = 512
SB_HEADS_PER_STEP = 2
RET_CHUNK = 256
OUT_ROWS = 512
OUT_COLS = 256
VMEM_LIMIT = 56 * 1024 * 1024
LOG_WEIGHT_FLOOR = -120.0
LOG2E = 1.4426950408889634


def _silu(v):
    return v / (1.0 + jnp.exp(-v))


def _adaln_kernel(c_ref, w_ref, b_ref, o_ref):
    s = _silu(c_ref[...])
    o_ref[...] = jnp.dot(s, w_ref[...], preferred_element_type=F32,
                         precision=lax.Precision.HIGHEST) + b_ref[...]


def _adaln(c_pad, w, b):
    rows, d = c_pad.shape
    n = w.shape[1]
    tn = n // ADALN_STEPS
    return pl.pallas_call(
        _adaln_kernel,
        grid=(n // tn,),
        in_specs=[pl.BlockSpec((rows, d), lambda j: (0, 0)),
                  pl.BlockSpec((d, tn), lambda j: (0, j)),
                  pl.BlockSpec((1, tn), lambda j: (0, j))],
        out_specs=pl.BlockSpec((rows, tn), lambda j: (0, j)),
        out_shape=jax.ShapeDtypeStruct((rows, n), F32),
        compiler_params=pltpu.CompilerParams(
            dimension_semantics=("arbitrary",), vmem_limit_bytes=VMEM_LIMIT),
        name="adaln",
    )(c_pad, w, b)


def _norm_qk_kernel(x_ref, mod_ref, ng_ref, w_ref, gain_ref, h_ref, o_ref, *, heads, q_mult):
    shift = mod_ref[0, 0:1, :]
    gain_h = ng_ref[...] * (1.0 + mod_ref[0, 1:2, :])
    heads_per_chunk = PROJ_COLS // HEAD_DIM
    for r in range(x_ref.shape[1] // NORM_UNIT_ROWS):
        for ch in range(NORM_UNIT_ROWS // NORM_CHUNK):
            rows = slice(r * NORM_UNIT_ROWS + ch * NORM_CHUNK,
                         r * NORM_UNIT_ROWS + (ch + 1) * NORM_CHUNK)
            x = x_ref[0, rows, :]
            ms = jnp.mean(x * x, axis=-1, keepdims=True)
            h_ref[0, rows, :] = (x * lax.rsqrt(ms + EPS) * gain_h + shift).astype(BF16)
        rows = slice(r * NORM_UNIT_ROWS, (r + 1) * NORM_UNIT_ROWS)
        h = h_ref[0, rows, :]
        for c in range(2 * heads // heads_per_chunk):
            acc = jnp.dot(h, w_ref[:, c * PROJ_COLS:(c + 1) * PROJ_COLS],
                          preferred_element_type=F32)
            for j in range(heads_per_chunk):
                head = c * heads_per_chunk + j
                gain = gain_ref[head // heads] * (q_mult if head < heads else 1.0)
                a = acc[:, j * HEAD_DIM:(j + 1) * HEAD_DIM]
                ms = jnp.mean(a * a, axis=-1, keepdims=True)
                o_ref[0, head, rows, :] = (a * lax.rsqrt(ms + EPS) * gain).astype(BF16)


def _norm_qk(x, mod3, norm_gain, w_bf16, qk_gain, heads, q_mult):
    b, s, d = x.shape
    tm = NORM_ROWS
    cols = 2 * heads * HEAD_DIM
    return pl.pallas_call(
        functools.partial(_norm_qk_kernel, heads=heads, q_mult=q_mult),
        grid=(b, s // tm),
        in_specs=[pl.BlockSpec((1, tm, d), lambda bi, si: (bi, si, 0)),
                  pl.BlockSpec((1, 3, d), lambda bi, si: (bi, 0, 0)),
                  pl.BlockSpec((1, d), lambda bi, si: (0, 0)),
                  pl.BlockSpec((d, cols), lambda bi, si: (0, 0), pipeline_mode=pl.Buffered(1)),
                  pl.BlockSpec(qk_gain.shape, lambda bi, si: (0, 0, 0))],
        out_specs=[pl.BlockSpec((1, tm, d), lambda bi, si: (bi, si, 0)),
                   pl.BlockSpec((1, 2 * heads, tm, HEAD_DIM), lambda bi, si: (bi, 0, si, 0))],
        out_shape=[jax.ShapeDtypeStruct((b, s, d), BF16),
                   jax.ShapeDtypeStruct((b, 2 * heads, s, HEAD_DIM), BF16)],
        compiler_params=pltpu.CompilerParams(
            dimension_semantics=("parallel", "parallel"), vmem_limit_bytes=VMEM_LIMIT),
        name="norm_qk",
    )(x, mod3, norm_gain, w_bf16, qk_gain)


def _proj_kernel(h_ref, w_ref, *refs, kind, heads, first_mult, second_mult):
    o_ref = refs[-1]
    g = pl.program_id(2)
    mult = jnp.where(g == 0, first_mult, second_mult).astype(F32)
    if kind == "rotary":
        def epilogue(a, rows):
            return (a * (refs[0][rows, :] * mult)
                    + pltpu.roll(a, HEAD_DIM // 2, 1) * (refs[1][rows, :] * mult))
    elif kind == "silu":
        epilogue = lambda a, rows: _silu(a)
    else:
        epilogue = lambda a, rows: a

    heads_per_chunk = PROJ_COLS // HEAD_DIM
    for r in range(h_ref.shape[1] // PROJ_UNIT_ROWS):
        rows = slice(r * PROJ_UNIT_ROWS, (r + 1) * PROJ_UNIT_ROWS)
        h = h_ref[0, rows, :]
        for c in range(heads // heads_per_chunk):
            acc = jnp.dot(h, w_ref[:, c * PROJ_COLS:(c + 1) * PROJ_COLS],
                          preferred_element_type=F32)
            for j in range(heads_per_chunk):
                a = acc[:, j * HEAD_DIM:(j + 1) * HEAD_DIM]
                o_ref[0, c * heads_per_chunk + j, rows, :] = epilogue(a, rows).astype(BF16)


def _in_proj(h, w_bf16, kind, first_group, group_stride, extra=(), extra_specs=(),
             first_mult=1.0, second_mult=1.0):
    b, s, d = h.shape
    tn = w_bf16.shape[1] // N_GROUPS
    heads = tn // HEAD_DIM
    tm = PROJ_ROWS
    return pl.pallas_call(
        functools.partial(_proj_kernel, kind=kind, heads=heads,
                          first_mult=first_mult, second_mult=second_mult),
        grid=(b, s // tm, 2),
        in_specs=[
            pl.BlockSpec((1, tm, d), lambda bi, si, g: (bi, si, 0)),
            pl.BlockSpec((d, tn), lambda bi, si, g: (0, first_group + group_stride * g)),
            *extra_specs,
        ],
        out_specs=pl.BlockSpec((1, heads, tm, HEAD_DIM), lambda bi, si, g: (bi, g, si, 0)),
        out_shape=jax.ShapeDtypeStruct((b, 2 * heads, s, HEAD_DIM), BF16),
        compiler_params=pltpu.CompilerParams(
            dimension_semantics=("parallel", "parallel", "arbitrary"),
            vmem_limit_bytes=VMEM_LIMIT),
        name="in_proj_" + kind,
    )(h, w_bf16, *extra)


def _sb_scores(q, k_blk):
    z = lax.dot_general(q, k_blk, (((1,), (1,)), ((), ())), preferred_element_type=F32)
    sp = jnp.maximum(z, 0.0) + jnp.log(1.0 + jnp.exp2(jnp.abs(z) * (-LOG2E)))
    return sp, z - sp


def _sb_later(sp, neg_upper):
    return jnp.dot(sp.astype(BF16), neg_upper, preferred_element_type=F32)


def _sb_edge_window(q, k_ref, v_ref, hh, q0, neg_upper):
    row = lax.broadcasted_iota(jnp.int32, (SB_Q, SB_K), 0)
    col = lax.broadcasted_iota(jnp.int32, (SB_Q, SB_K), 1)
    d_start = jnp.maximum(q0 - SB_Q, 0)
    p_start = jnp.maximum(d_start - SB_K, 0)
    mask_d = col < row + (q0 - d_start)
    mask_p = col < (d_start - p_start)
    d_start = pl.multiple_of(d_start, SB_Q)
    p_start = pl.multiple_of(p_start, SB_Q)

    sp_d, logb_d = _sb_scores(q, k_ref[0, hh, pl.ds(d_start, SB_K), :])
    sp_p, logb_p = _sb_scores(q, k_ref[0, hh, pl.ds(p_start, SB_K), :])
    sp_d = jnp.where(mask_d, sp_d, 0.0)
    sp_p = jnp.where(mask_p, sp_p, 0.0)
    w_d = jnp.where(mask_d, jnp.exp(logb_d + _sb_later(sp_d, neg_upper)), 0.0)
    carry = -jnp.sum(sp_d, axis=1, keepdims=True)
    w_p = jnp.where(mask_p, jnp.exp(logb_p + _sb_later(sp_p, neg_upper) + carry), 0.0)
    acc = (jnp.dot(w_d.astype(BF16), v_ref[0, hh, pl.ds(d_start, SB_K), :], preferred_element_type=F32)
           + jnp.dot(w_p.astype(BF16), v_ref[0, hh, pl.ds(p_start, SB_K), :], preferred_element_type=F32))
    carry = carry - jnp.sum(sp_p, axis=1, keepdims=True)
    return carry, acc, p_start


def _sb_interior_windows(q_ref, k_ref, v_ref, step_start, neg_upper, heads_per_step):
    subs = SB_STEP_ROWS // SB_Q
    win = 2 * SB_K
    per_mm = SB_QK_ROWS // SB_Q
    span = win + (per_mm - 1) * SB_Q
    w_start = pl.multiple_of(step_start - (win - SB_Q), SB_Q)
    zs = []
    for hh in range(heads_per_step):
        for m in range(SB_STEP_ROWS // SB_QK_ROWS):
            k_start = pl.multiple_of(w_start + m * SB_QK_ROWS, SB_Q)
            z = lax.dot_general(q_ref[0, hh, m * SB_QK_ROWS:(m + 1) * SB_QK_ROWS, :],
                                k_ref[0, hh, pl.ds(k_start, span), :],
                                (((1,), (1,)), ((), ())), preferred_element_type=F32)
            zs += [z[u * SB_Q:(u + 1) * SB_Q, u * SB_Q:u * SB_Q + win] for u in range(per_mm)]
    z = jnp.concatenate(zs, axis=0).astype(BF16)
    n = z.shape[0]
    l = jnp.log(1.0 + jnp.exp(-jnp.abs(z)))
    sp = jnp.maximum(z, 0.0) + l
    logb = jnp.minimum(z, 0.0) - l
    row = lax.broadcasted_iota(jnp.int32, (n, SB_Q), 0) & (SB_Q - 1)
    col = lax.broadcasted_iota(jnp.int32, (n, SB_Q), 1)
    past = jnp.where(col < row, 1.0, 0.0).astype(BF16)
    diag = lambda t: jnp.concatenate([t[:, :SB_K - SB_Q], t[:, SB_K - SB_Q:] * past], axis=1)
    sp_p, sp_d = sp[:, :SB_K], diag(sp[:, SB_K:])
    w_d = diag(jnp.exp(logb[:, SB_K:] + _sb_later(sp_d, neg_upper).astype(BF16)))
    carry = -jnp.sum(sp_d.astype(F32), axis=1, keepdims=True)
    w_p = jnp.exp(logb[:, :SB_K] + (_sb_later(sp_p, neg_upper) + carry).astype(BF16))
    carry = carry - jnp.sum(sp_p.astype(F32), axis=1, keepdims=True)
    w = jnp.concatenate([w_p, w_d], axis=1)
    carries, accs = [], []
    for hh in range(heads_per_step):
        for sub in range(subs):
            c = hh * subs + sub
            v_win = v_ref[0, hh, pl.ds(pl.multiple_of(w_start + sub * SB_Q, SB_Q), win), :]
            accs.append(jnp.dot(w[c * SB_Q:(c + 1) * SB_Q], v_win, preferred_element_type=F32))
            carries.append(carry[c * SB_Q:(c + 1) * SB_Q])
    return carries, accs, w_start


def _sb_kernel(q_ref, k_ref, v_ref, g_ref, o_ref, *, heads_per_step):
    i = pl.program_id(2)
    urow = lax.broadcasted_iota(jnp.int32, (SB_K, SB_K), 0)
    ucol = lax.broadcasted_iota(jnp.int32, (SB_K, SB_K), 1)
    neg_upper = jnp.where(urow > ucol, -1.0, 0.0).astype(BF16)
    neg_upper_tail = neg_upper[:SB_TAIL, :SB_TAIL]
    subs = SB_STEP_ROWS // SB_Q

    def run(edge):
        step_start = i * SB_STEP_ROWS
        if not edge:
            carries, accs, w_start = _sb_interior_windows(
                q_ref, k_ref, v_ref, step_start, neg_upper, heads_per_step)
        chains = []
        for hh in range(heads_per_step):
            for sub in range(subs):
                rows = slice(sub * SB_Q, (sub + 1) * SB_Q)
                q = q_ref[0, hh, rows, :]
                if edge:
                    carry, acc, p_start = _sb_edge_window(
                        q, k_ref, v_ref, hh, step_start + sub * SB_Q, neg_upper)
                else:
                    carry, acc = carries[hh * subs + sub], accs[hh * subs + sub]
                    p_start = w_start + sub * SB_Q
                o_ref[0, hh, rows, :] = (acc * g_ref[0, hh, rows, :].astype(F32)).astype(BF16)
                live = (jnp.max(carry) > LOG_WEIGHT_FLOOR) & (p_start > 0)
                chains.append((hh, rows, q, carry, acc, p_start, live))

        def older_tiles(hh, rows, q, carry, acc, p_start, live):
            tiles_left = p_start // SB_TAIL

            def cond(state):
                jj, live, _, _ = state
                return (jj < tiles_left) & live

            def body(state):
                jj, _, carry, acc = state
                start = pl.multiple_of(p_start - (jj + 1) * SB_TAIL, SB_TAIL)
                sp, logb = _sb_scores(q, k_ref[0, hh, pl.ds(start, SB_TAIL), :])
                w = jnp.exp(logb + _sb_later(sp, neg_upper_tail) + carry)
                acc = acc + jnp.dot(w.astype(BF16), v_ref[0, hh, pl.ds(start, SB_TAIL), :],
                                    preferred_element_type=F32)
                carry = carry - jnp.sum(sp, axis=1, keepdims=True)
                return jj + 1, jnp.max(carry) > LOG_WEIGHT_FLOOR, carry, acc

            _, _, _, acc = lax.while_loop(cond, body, (jnp.int32(0), live, carry, acc))
            o_ref[0, hh, rows, :] = (acc * g_ref[0, hh, rows, :].astype(F32)).astype(BF16)

        any_live = functools.reduce(jnp.logical_or, [ch[-1] for ch in chains])

        @pl.when(any_live)
        def _():
            for chain in chains:
                pl.when(chain[-1])(functools.partial(older_tiles, *chain))

    first_interior = (SB_Q + SB_K + SB_STEP_ROWS - 1) // SB_STEP_ROWS
    pl.when(i >= first_interior)(lambda: run(False))
    pl.when(i < first_interior)(lambda: run(True))


def _sb_attention(qk, v, gate, heads):
    b, _, s, d = qk.shape
    hb = SB_HEADS_PER_STEP
    groups = heads // hb
    t = SB_STEP_ROWS
    return pl.pallas_call(
        functools.partial(_sb_kernel, heads_per_step=hb),
        grid=(b, groups, s // t),
        in_specs=[
            pl.BlockSpec((1, hb, t, d), lambda bi, g, i: (bi, g, i, 0)),
            pl.BlockSpec((1, hb, s, d), lambda bi, g, i: (bi, groups + g, 0, 0)),
            pl.BlockSpec((1, hb, s, d), lambda bi, g, i: (bi, g, 0, 0)),
            pl.BlockSpec((1, hb, t, d), lambda bi, g, i: (bi, g, i, 0)),
        ],
        out_specs=pl.BlockSpec((1, hb, t, d), lambda bi, g, i: (bi, g, i, 0)),
        out_shape=jax.ShapeDtypeStruct((b, heads, s, d), BF16),
        compiler_params=pltpu.CompilerParams(
            dimension_semantics=("parallel", "parallel", "arbitrary"),
            vmem_limit_bytes=VMEM_LIMIT),
        name="sb_attn",
    )(qk, qk, v, gate)


def _ret_kernel(q_ref, k_ref, v_ref, g_ref, din_ref, qd_ref, kd_ref, cd_ref, gain_ref,
                o_ref, state_scr, *, heads):
    ci = pl.program_id(1)

    @pl.when(ci == 0)
    def _():
        state_scr[...] = jnp.zeros_like(state_scr)

    for h in range(heads):
        qc = q_ref[0, h]
        kc = k_ref[0, h]
        vc = v_ref[0, h]
        state = state_scr[h]

        scores = lax.dot_general(qc, kc, (((1,), (1,)), ((), ())), preferred_element_type=F32)
        inner = jnp.dot((scores * din_ref[h]).astype(BF16), vc, preferred_element_type=F32)
        q_dec = (qc.astype(F32) * qd_ref[h]).astype(BF16)
        cross = jnp.dot(q_dec, state.astype(BF16), preferred_element_type=F32)
        k_dec = (kc.astype(F32) * kd_ref[h]).astype(BF16)
        kv = lax.dot_general(k_dec, vc, (((0,), (0,)), ((), ())), preferred_element_type=F32)
        state_scr[h] = state * cd_ref[h] + kv

        o = inner + cross
        mu = jnp.mean(o, axis=-1, keepdims=True)
        cen = o - mu
        var = jnp.mean(cen * cen, axis=-1, keepdims=True)
        y = cen * lax.rsqrt(var + EPS) * gain_ref[:, h * HEAD_DIM:(h + 1) * HEAD_DIM]
        o_ref[0, h] = (y * g_ref[0, h].astype(F32)).astype(BF16)


def _retention(qk, v, gate, heads, din, qd, kd, cd, gain):
    b, _, s, d = qk.shape
    c = RET_CHUNK
    blk = lambda group: pl.BlockSpec((1, heads, c, d), lambda bi, ci: (bi, group, ci, 0))
    const = lambda arr: pl.BlockSpec(arr.shape, lambda bi, ci: (0,) * arr.ndim)
    return pl.pallas_call(
        functools.partial(_ret_kernel, heads=heads),
        grid=(b, s // c),
        in_specs=[blk(0), blk(1), blk(1), blk(1),
                  const(din), const(qd), const(kd), const(cd), const(gain)],
        out_specs=pl.BlockSpec((1, heads, c, d), lambda bi, ci: (bi, 0, ci, 0)),
        out_shape=jax.ShapeDtypeStruct((b, heads, s, d), BF16),
        scratch_shapes=[pltpu.VMEM((heads, d, d), F32)],
        compiler_params=pltpu.CompilerParams(
            dimension_semantics=("parallel", "arbitrary"),
            vmem_limit_bytes=VMEM_LIMIT),
        name="retention",
    )(qk, qk, v, gate, din, qd, kd, cd, gain)


def _retention_tables(heads, c):
    log_gamma = np.log1p(-np.exp2(-5.0 - np.arange(heads, dtype=np.float64)))
    idx = np.arange(c, dtype=np.float64)
    rel = idx[:, None] - idx[None, :]
    din = np.where(rel[None] >= 0, np.exp(np.maximum(rel, 0.0)[None] * log_gamma[:, None, None]), 0.0)
    qd = np.exp((idx[None, :] + 1.0) * log_gamma[:, None])[..., None]
    kd = np.exp((c - 1.0 - idx[None, :]) * log_gamma[:, None])[..., None]
    cd = np.exp(c * log_gamma)[:, None, None]
    bc = lambda t: np.ascontiguousarray(
        np.broadcast_to(t, t.shape[:-1] + (HEAD_DIM,)), dtype=np.float32)
    return din.astype(np.float32), bc(qd), bc(kd), bc(cd)


def _out_kernel(a_ref, r_ref, w_ref, x_ref, mod_ref, o_ref, *, heads):
    parts = [a_ref[0, j] for j in range(heads)] + [r_ref[0, j] for j in range(heads)]
    mix = jnp.concatenate(parts, axis=-1)
    for c in range(w_ref.shape[1] // OUT_COLS):
        cols = slice(c * OUT_COLS, (c + 1) * OUT_COLS)
        y = jnp.dot(mix, w_ref[:, cols], preferred_element_type=F32)
        o_ref[0, :, cols] = x_ref[0, :, cols] + mod_ref[0, 2:3, cols] * y


def _out_proj(oa, ob, w_bf16, x, mod3):
    b, s, d = x.shape
    heads = oa.shape[1]
    tm = OUT_ROWS
    return pl.pallas_call(
        functools.partial(_out_kernel, heads=heads),
        grid=(b, s // tm),
        in_specs=[
            pl.BlockSpec((1, heads, tm, HEAD_DIM), lambda bi, si: (bi, 0, si, 0)),
            pl.BlockSpec((1, heads, tm, HEAD_DIM), lambda bi, si: (bi, 0, si, 0)),
            pl.BlockSpec(w_bf16.shape, lambda bi, si: (0, 0)),
            pl.BlockSpec((1, tm, d), lambda bi, si: (bi, si, 0)),
            pl.BlockSpec((1, 3, d), lambda bi, si: (bi, 0, 0)),
        ],
        out_specs=pl.BlockSpec((1, tm, d), lambda bi, si: (bi, si, 0)),
        out_shape=jax.ShapeDtypeStruct((b, s, d), F32),
        compiler_params=pltpu.CompilerParams(
            dimension_semantics=("parallel", "parallel"),
            vmem_limit_bytes=VMEM_LIMIT),
        name="out_proj",
    )(oa, ob, w_bf16, x, mod3)


def _rotary_tables(s):
    half = HEAD_DIM // 2
    inv_freq = ROPE_BASE ** (-np.arange(half, dtype=np.float64) / half)
    ang = np.arange(s, dtype=np.float64)[:, None] * inv_freq[None, :]
    cos, sin = np.cos(ang), np.sin(ang)
    return (np.concatenate([cos, cos], axis=-1).astype(np.float32),
            np.concatenate([-sin, sin], axis=-1).astype(np.float32))


def kernel(x, c, w_ada, b_ada, norm_gain, w_in, sb_q_gain, sb_k_gain, ret_norm_gain, w_out):
    b, s, d = x.shape
    depth = w_ada.shape[0]
    heads = w_in.shape[2] // (N_GROUPS * HEAD_DIM)
    cos_t, sin_t = _rotary_tables(s)
    din, qd, kd, cd = _retention_tables(heads, RET_CHUNK)
    c_pad = jnp.pad(c, ((0, 8 - b), (0, 0)))
    rope_spec = pl.BlockSpec((PROJ_ROWS, HEAD_DIM), lambda bi, si, g: (si, 0))
    for layer in range(depth):
        mod = _adaln(c_pad, w_ada[layer], b_ada[layer][None, :])
        mod3 = mod[:b].reshape(b, 3, d)
        w = w_in[layer].astype(BF16)
        qk_gain = jnp.stack([sb_q_gain[layer], sb_k_gain[layer]])[:, None, :]
        h, sb_qk = _norm_qk(x, mod3, norm_gain[layer][None, :], w, qk_gain, heads,
                            q_mult=float(1.0 / np.sqrt(HEAD_DIM)))
        vals = _in_proj(h, w, "identity", 2, 4)
        gates = _in_proj(h, w, "silu", 3, 4)
        ret_qk = _in_proj(h, w, "rotary", 4, 1, (cos_t, sin_t), (rope_spec, rope_spec),
                          second_mult=float(HEAD_DIM ** -0.5))
        oa = _sb_attention(sb_qk, vals, gates, heads)
        ob = _retention(ret_qk, vals, gates, heads, din, qd, kd, cd, ret_norm_gain[layer][None, :])
        x = _out_proj(oa, ob, w_out[layer].astype(BF16), x, mod3)
    return x
```

```python
import functools

import jax
import jax.numpy as jnp
import numpy as np
from jax import lax
from jax.experimental import pallas as pl
from jax.experimental.pallas import tpu as pltpu

HEAD_DIM = 128
ROPE_BASE = 10000.0
EPS = 1e-6
N_GROUPS = 8
F32 = jnp.float32
BF16 = jnp.bfloat16

ADALN_ROWS = 16
ADALN_STEPS = 16
NORM_ROWS = 1024
NORM_UNIT_ROWS = 512
NORM_CHUNK = 32
PROJ_ROWS = 2048
PROJ_UNIT_ROWS = 1024
PROJ_COLS = 256
SB_Q = 128
SB_K = 256
SB_TAIL = 128
SB_QK_ROWS = 256
SB_STEP_ROWS = 1024
SB_HEADS_PER_STEP = 2
RET_CHUNK = 256
OUT_ROWS = 512
OUT_COLS = 256
VMEM_LIMIT = 56 * 1024 * 1024
LOG_WEIGHT_FLOOR = -120.0
LOG2E = 1.4426950408889634


def _silu(v):
    half = 0.5 * v
    return half + half * jnp.tanh(half)


def _split_bf16(t):
    hi = t.astype(BF16)
    return hi, (t - hi.astype(F32)).astype(BF16)


def _adaln_kernel(c_ref, w_ref, b_ref, o_ref):
    s_hi, s_lo = _split_bf16(_silu(c_ref[...]))
    w_hi, w_lo = _split_bf16(w_ref[...])
    dot = functools.partial(jnp.dot, preferred_element_type=F32)
    o_ref[...] = dot(s_hi, w_hi) + (dot(s_hi, w_lo) + dot(s_lo, w_hi)) + b_ref[...]


def _adaln(c_pad, w, b):
    rows, d = c_pad.shape
    n = w.shape[1]
    tn = n // ADALN_STEPS
    return pl.pallas_call(
        _adaln_kernel,
        grid=(n // tn,),
        in_specs=[pl.BlockSpec((rows, d), lambda j: (0, 0)),
                  pl.BlockSpec((d, tn), lambda j: (0, j)),
                  pl.BlockSpec((1, tn), lambda j: (0, j))],
        out_specs=pl.BlockSpec((rows, tn), lambda j: (0, j)),
        out_shape=jax.ShapeDtypeStruct((rows, n), F32),
        compiler_params=pltpu.CompilerParams(
            dimension_semantics=("arbitrary",), vmem_limit_bytes=VMEM_LIMIT),
        name="adaln",
    )(c_pad, w, b)


def _norm_qk_kernel(x_ref, mod_ref, ng_ref, w_ref, gain_ref, h_ref, o_ref, *, heads, q_mult):
    shift = mod_ref[0, 0:1, :]
    gain_h = ng_ref[...] * (1.0 + mod_ref[0, 1:2, :])
    heads_per_chunk = PROJ_COLS // HEAD_DIM
    for r in range(x_ref.shape[1] // NORM_UNIT_ROWS):
        for ch in range(NORM_UNIT_ROWS // NORM_CHUNK):
            rows = slice(r * NORM_UNIT_ROWS + ch * NORM_CHUNK,
                         r * NORM_UNIT_ROWS + (ch + 1) * NORM_CHUNK)
            x = x_ref[0, rows, :]
            ms = jnp.mean(x * x, axis=-1, keepdims=True)
            h_ref[0, rows, :] = (x * lax.rsqrt(ms + EPS) * gain_h + shift).astype(BF16)
        rows = slice(r * NORM_UNIT_ROWS, (r + 1) * NORM_UNIT_ROWS)
        h = h_ref[0, rows, :]
        for c in range(2 * heads // heads_per_chunk):
            acc = jnp.dot(h, w_ref[:, c * PROJ_COLS:(c + 1) * PROJ_COLS],
                          preferred_element_type=F32)
            for j in range(heads_per_chunk):
                head = c * heads_per_chunk + j
                gain = gain_ref[head // heads] * (q_mult if head < heads else 1.0)
                a = acc[:, j * HEAD_DIM:(j + 1) * HEAD_DIM]
                ms = jnp.mean(a * a, axis=-1, keepdims=True)
                o_ref[0, head, rows, :] = (a * lax.rsqrt(ms + EPS) * gain).astype(BF16)


def _norm_qk(x, mod3, norm_gain, w_bf16, qk_gain, heads, q_mult):
    b, s, d = x.shape
    tm = NORM_ROWS
    cols = 2 * heads * HEAD_DIM
    return pl.pallas_call(
        functools.partial(_norm_qk_kernel, heads=heads, q_mult=q_mult),
        grid=(b, s // tm),
        in_specs=[pl.BlockSpec((1, tm, d), lambda bi, si: (bi, si, 0)),
                  pl.BlockSpec((1, 3, d), lambda bi, si: (bi, 0, 0)),
                  pl.BlockSpec((1, d), lambda bi, si: (0, 0)),
                  pl.BlockSpec((d, cols), lambda bi, si: (0, 0), pipeline_mode=pl.Buffered(1)),
                  pl.BlockSpec(qk_gain.shape, lambda bi, si: (0, 0, 0))],
        out_specs=[pl.BlockSpec((1, tm, d), lambda bi, si: (bi, si, 0)),
                   pl.BlockSpec((1, 2 * heads, tm, HEAD_DIM), lambda bi, si: (bi, 0, si, 0))],
        out_shape=[jax.ShapeDtypeStruct((b, s, d), BF16),
                   jax.ShapeDtypeStruct((b, 2 * heads, s, HEAD_DIM), BF16)],
        compiler_params=pltpu.CompilerParams(
            dimension_semantics=("parallel", "parallel"), vmem_limit_bytes=VMEM_LIMIT),
        name="norm_qk",
    )(x, mod3, norm_gain, w_bf16, qk_gain)


def _proj_kernel(h_ref, w_ref, *refs, kind, heads, first_mult, second_mult):
    o_ref = refs[-1]
    g = pl.program_id(2)
    mult = jnp.where(g == 0, first_mult, second_mult).astype(F32)
    if kind == "rotary":
        def epilogue(a, rows):
            return (a * (refs[0][rows, :] * mult)
                    + pltpu.roll(a, HEAD_DIM // 2, 1) * (refs[1][rows, :] * mult))
    elif kind == "silu":
        epilogue = lambda a, rows: _silu(a)
    else:
        epilogue = lambda a, rows: a

    heads_per_chunk = PROJ_COLS // HEAD_DIM
    for r in range(h_ref.shape[1] // PROJ_UNIT_ROWS):
        rows = slice(r * PROJ_UNIT_ROWS, (r + 1) * PROJ_UNIT_ROWS)
        h = h_ref[0, rows, :]
        for c in range(heads // heads_per_chunk):
            acc = jnp.dot(h, w_ref[:, c * PROJ_COLS:(c + 1) * PROJ_COLS],
                          preferred_element_type=F32)
            for j in range(heads_per_chunk):
                a = acc[:, j * HEAD_DIM:(j + 1) * HEAD_DIM]
                o_ref[0, c * heads_per_chunk + j, rows, :] = epilogue(a, rows).astype(BF16)


def _in_proj(h, w_bf16, kind, first_group, group_stride, extra=(), extra_specs=(),
             first_mult=1.0, second_mult=1.0):
    b, s, d = h.shape
    tn = w_bf16.shape[1] // N_GROUPS
    heads = tn // HEAD_DIM
    tm = PROJ_ROWS
    return pl.pallas_call(
        functools.partial(_proj_kernel, kind=kind, heads=heads,
                          first_mult=first_mult, second_mult=second_mult),
        grid=(b, s // tm, 2),
        in_specs=[
            pl.BlockSpec((1, tm, d), lambda bi, si, g: (bi, si, 0)),
            pl.BlockSpec((d, tn), lambda bi, si, g: (0, first_group + group_stride * g)),
            *extra_specs,
        ],
        out_specs=pl.BlockSpec((1, heads, tm, HEAD_DIM), lambda bi, si, g: (bi, g, si, 0)),
        out_shape=jax.ShapeDtypeStruct((b, 2 * heads, s, HEAD_DIM), BF16),
        compiler_params=pltpu.CompilerParams(
            dimension_semantics=("parallel", "parallel", "arbitrary"),
            vmem_limit_bytes=VMEM_LIMIT),
        name="in_proj_" + kind,
    )(h, w_bf16, *extra)


def _sb_scores(q, k_blk):
    z = lax.dot_general(q, k_blk, (((1,), (1,)), ((), ())), preferred_element_type=F32)
    sp = jnp.maximum(z, 0.0) + jnp.log(1.0 + jnp.exp2(jnp.abs(z) * (-LOG2E)))
    return sp, z - sp


def _sb_later(sp, neg_upper):
    return jnp.dot(sp.astype(BF16), neg_upper, preferred_element_type=F32)


def _sb_edge_window(q, k_ref, v_ref, hh, q0, neg_upper):
    row = lax.broadcasted_iota(jnp.int32, (SB_Q, SB_K), 0)
    col = lax.broadcasted_iota(jnp.int32, (SB_Q, SB_K), 1)
    d_start = jnp.maximum(q0 - SB_Q, 0)
    p_start = jnp.maximum(d_start - SB_K, 0)
    mask_d = col < row + (q0 - d_start)
    mask_p = col < (d_start - p_start)
    d_start = pl.multiple_of(d_start, SB_Q)
    p_start = pl.multiple_of(p_start, SB_Q)

    sp_d, logb_d = _sb_scores(q, k_ref[0, hh, pl.ds(d_start, SB_K), :])
    sp_p, logb_p = _sb_scores(q, k_ref[0, hh, pl.ds(p_start, SB_K), :])
    sp_d = jnp.where(mask_d, sp_d, 0.0)
    sp_p = jnp.where(mask_p, sp_p, 0.0)
    w_d = jnp.where(mask_d, jnp.exp(logb_d + _sb_later(sp_d, neg_upper)), 0.0)
    carry = -jnp.sum(sp_d, axis=1, keepdims=True)
    w_p = jnp.where(mask_p, jnp.exp(logb_p + _sb_later(sp_p, neg_upper) + carry), 0.0)
    acc = (jnp.dot(w_d.astype(BF16), v_ref[0, hh, pl.ds(d_start, SB_K), :], preferred_element_type=F32)
           + jnp.dot(w_p.astype(BF16), v_ref[0, hh, pl.ds(p_start, SB_K), :], preferred_element_type=F32))
    carry = carry - jnp.sum(sp_p, axis=1, keepdims=True)
    return carry, acc, p_start


def _sb_interior_windows(q_ref, k_ref, v_ref, step_start, neg_upper, heads_per_step, first_block):
    win = 2 * SB_K
    per_mm = SB_QK_ROWS // SB_Q
    span = win + (per_mm - 1) * SB_Q
    w_start = step_start - (win - SB_Q)
    zs, keys = [], []
    for hh in range(heads_per_step):
        for m in range(first_block, SB_STEP_ROWS // SB_QK_ROWS):
            k_start = pl.multiple_of(w_start + m * SB_QK_ROWS, SB_Q)
            z = lax.dot_general(q_ref[0, hh, m * SB_QK_ROWS:(m + 1) * SB_QK_ROWS, :],
                                k_ref[0, hh, pl.ds(k_start, span), :],
                                (((1,), (1,)), ((), ())), preferred_element_type=F32)
            zs += [z[u * SB_Q:(u + 1) * SB_Q, u * SB_Q:u * SB_Q + win] for u in range(per_mm)]
            keys += [(hh, m * per_mm + u) for u in range(per_mm)]
    z = jnp.concatenate(zs, axis=0).astype(BF16)
    n = z.shape[0]
    l = jnp.log(1.0 + jnp.exp(-jnp.abs(z)))
    sp = jnp.maximum(z, 0.0) + l
    logb = jnp.minimum(z, 0.0) - l
    row = lax.broadcasted_iota(jnp.int32, (n, SB_Q), 0) & (SB_Q - 1)
    col = lax.broadcasted_iota(jnp.int32, (n, SB_Q), 1)
    past = jnp.where(col < row, 1.0, 0.0).astype(BF16)
    diag = lambda t: jnp.concatenate([t[:, :SB_K - SB_Q], t[:, SB_K - SB_Q:] * past], axis=1)
    sp_p, sp_d = sp[:, :SB_K], diag(sp[:, SB_K:])
    w_d = diag(jnp.exp(logb[:, SB_K:] + _sb_later(sp_d, neg_upper).astype(BF16)))
    carry = -jnp.sum(sp_d.astype(F32), axis=1, keepdims=True)
    w_p = jnp.exp(logb[:, :SB_K] + (_sb_later(sp_p, neg_upper) + carry).astype(BF16))
    carry = carry - jnp.sum(sp_p.astype(F32), axis=1, keepdims=True)
    w = jnp.concatenate([w_p, w_d], axis=1)
    out = {}
    for c, (hh, sub) in enumerate(keys):
        p_start = pl.multiple_of(w_start + sub * SB_Q, SB_Q)
        acc = jnp.dot(w[c * SB_Q:(c + 1) * SB_Q], v_ref[0, hh, pl.ds(p_start, win), :],
                      preferred_element_type=F32)
        out[(hh, sub)] = (carry[c * SB_Q:(c + 1) * SB_Q], acc, p_start)
    return out


def _sb_kernel(q_ref, k_ref, v_ref, g_ref, o_ref, *, heads_per_step):
    i = pl.program_id(2)
    urow = lax.broadcasted_iota(jnp.int32, (SB_K, SB_K), 0)
    ucol = lax.broadcasted_iota(jnp.int32, (SB_K, SB_K), 1)
    neg_upper = jnp.where(urow > ucol, -1.0, 0.0).astype(BF16)
    neg_upper_tail = neg_upper[:SB_TAIL, :SB_TAIL]
    subs = SB_STEP_ROWS // SB_Q

    def run(edge):
        step_start = i * SB_STEP_ROWS
        edge_blocks = -(-(2 * SB_K - SB_Q) // SB_QK_ROWS) if edge else 0
        stacked = _sb_interior_windows(q_ref, k_ref, v_ref, step_start, neg_upper,
                                       heads_per_step, edge_blocks)
        chains = []
        for hh in range(heads_per_step):
            for sub in range(subs):
                rows = slice(sub * SB_Q, (sub + 1) * SB_Q)
                q = q_ref[0, hh, rows, :]
                if (hh, sub) in stacked:
                    carry, acc, p_start = stacked[(hh, sub)]
                else:
                    carry, acc, p_start = _sb_edge_window(
                        q, k_ref, v_ref, hh, step_start + sub * SB_Q, neg_upper)
                o_ref[0, hh, rows, :] = (acc * g_ref[0, hh, rows, :].astype(F32)).astype(BF16)
                live = (jnp.max(carry) > LOG_WEIGHT_FLOOR) & (p_start > 0)
                chains.append((hh, rows, q, carry, acc, p_start, live))

        def older_tiles(hh, rows, q, carry, acc, p_start, live):
            tiles_left = p_start // SB_TAIL

            def cond(state):
                jj, live, _, _ = state
                return (jj < tiles_left) & live

            def body(state):
                jj, _, carry, acc = state
                start = pl.multiple_of(p_start - (jj + 1) * SB_TAIL, SB_TAIL)
                sp, logb = _sb_scores(q, k_ref[0, hh, pl.ds(start, SB_TAIL), :])
                w = jnp.exp(logb + _sb_later(sp, neg_upper_tail) + carry)
                acc = acc + jnp.dot(w.astype(BF16), v_ref[0, hh, pl.ds(start, SB_TAIL), :],
                                    preferred_element_type=F32)
                carry = carry - jnp.sum(sp, axis=1, keepdims=True)
                return jj + 1, jnp.max(carry) > LOG_WEIGHT_FLOOR, carry, acc

            _, _, _, acc = lax.while_loop(cond, body, (jnp.int32(0), live, carry, acc))
            o_ref[0, hh, rows, :] = (acc * g_ref[0, hh, rows, :].astype(F32)).astype(BF16)

        any_live = functools.reduce(jnp.logical_or, [ch[-1] for ch in chains])

        @pl.when(any_live)
        def _():
            for chain in chains:
                pl.when(chain[-1])(functools.partial(older_tiles, *chain))

    first_interior = (SB_Q + SB_K + SB_STEP_ROWS - 1) // SB_STEP_ROWS
    pl.when(i >= first_interior)(lambda: run(False))
    pl.when(i < first_interior)(lambda: run(True))


def _sb_attention(qk, v, gate, heads):
    b, _, s, d = qk.shape
    hb = SB_HEADS_PER_STEP
    groups = heads // hb
    t = SB_STEP_ROWS
    return pl.pallas_call(
        functools.partial(_sb_kernel, heads_per_step=hb),
        grid=(b, groups, s // t),
        in_specs=[
            pl.BlockSpec((1, hb, t, d), lambda bi, g, i: (bi, g, i, 0)),
            pl.BlockSpec((1, hb, s, d), lambda bi, g, i: (bi, groups + g, 0, 0)),
            pl.BlockSpec((1, hb, s, d), lambda bi, g, i: (bi, g, 0, 0)),
            pl.BlockSpec((1, hb, t, d), lambda bi, g, i: (bi, g, i, 0)),
        ],
        out_specs=pl.BlockSpec((1, hb, t, d), lambda bi, g, i: (bi, g, i, 0)),
        out_shape=jax.ShapeDtypeStruct((b, heads, s, d), BF16),
        compiler_params=pltpu.CompilerParams(
            dimension_semantics=("parallel", "parallel", "arbitrary"),
            vmem_limit_bytes=VMEM_LIMIT),
        name="sb_attn",
    )(qk, qk, v, gate)


def _ret_kernel(q_ref, k_ref, v_ref, g_ref, din_ref, qd_ref, kd_ref, cd_ref, gain_ref,
                o_ref, state_scr, *, heads):
    ci = pl.program_id(1)

    @pl.when(ci == 0)
    def _():
        state_scr[...] = jnp.zeros_like(state_scr)

    for h in range(heads):
        qc = q_ref[0, h]
        kc = k_ref[0, h]
        vc = v_ref[0, h]
        state = state_scr[h]

        scores = lax.dot_general(qc, kc, (((1,), (1,)), ((), ())), preferred_element_type=F32)
        inner = jnp.dot((scores * din_ref[h]).astype(BF16), vc, preferred_element_type=F32)
        q_dec = (qc.astype(F32) * qd_ref[h]).astype(BF16)
        cross = jnp.dot(q_dec, state.astype(BF16), preferred_element_type=F32)
        k_dec = (kc.astype(F32) * kd_ref[h]).astype(BF16)
        kv = lax.dot_general(k_dec, vc, (((0,), (0,)), ((), ())), preferred_element_type=F32)
        state_scr[h] = state * cd_ref[h] + kv

        o = inner + cross
        mu = jnp.mean(o, axis=-1, keepdims=True)
        cen = o - mu
        var = jnp.mean(cen * cen, axis=-1, keepdims=True)
        y = cen * lax.rsqrt(var + EPS) * gain_ref[:, h * HEAD_DIM:(h + 1) * HEAD_DIM]
        o_ref[0, h] = (y * g_ref[0, h].astype(F32)).astype(BF16)


def _retention(qk, v, gate, heads, din, qd, kd, cd, gain):
    b, _, s, d = qk.shape
    c = RET_CHUNK
    blk = lambda group: pl.BlockSpec((1, heads, c, d), lambda bi, ci: (bi, group, ci, 0))
    const = lambda arr: pl.BlockSpec(arr.shape, lambda bi, ci: (0,) * arr.ndim)
    return pl.pallas_call(
        functools.partial(_ret_kernel, heads=heads),
        grid=(b, s // c),
        in_specs=[blk(0), blk(1), blk(1), blk(1),
                  const(din), const(qd), const(kd), const(cd), const(gain)],
        out_specs=pl.BlockSpec((1, heads, c, d), lambda bi, ci: (bi, 0, ci, 0)),
        out_shape=jax.ShapeDtypeStruct((b, heads, s, d), BF16),
        scratch_shapes=[pltpu.VMEM((heads, d, d), F32)],
        compiler_params=pltpu.CompilerParams(
            dimension_semantics=("parallel", "arbitrary"),
            vmem_limit_bytes=VMEM_LIMIT),
        name="retention",
    )(qk, qk, v, gate, din, qd, kd, cd, gain)


def _retention_tables(heads, c):
    log_gamma = np.log1p(-np.exp2(-5.0 - np.arange(heads, dtype=np.float64)))
    idx = np.arange(c, dtype=np.float64)
    rel = idx[:, None] - idx[None, :]
    din = np.where(rel[None] >= 0, np.exp(np.maximum(rel, 0.0)[None] * log_gamma[:, None, None]), 0.0)
    qd = np.exp((idx[None, :] + 1.0) * log_gamma[:, None])[..., None]
    kd = np.exp((c - 1.0 - idx[None, :]) * log_gamma[:, None])[..., None]
    cd = np.exp(c * log_gamma)[:, None, None]
    bc = lambda t: np.ascontiguousarray(
        np.broadcast_to(t, t.shape[:-1] + (HEAD_DIM,)), dtype=np.float32)
    return din.astype(np.float32), bc(qd), bc(kd), bc(cd)


def _out_kernel(a_ref, r_ref, w_ref, x_ref, mod_ref, o_ref, *, heads):
    parts = [a_ref[0, j] for j in range(heads)] + [r_ref[0, j] for j in range(heads)]
    mix = jnp.concatenate(parts, axis=-1)
    for c in range(w_ref.shape[1] // OUT_COLS):
        cols = slice(c * OUT_COLS, (c + 1) * OUT_COLS)
        y = jnp.dot(mix, w_ref[:, cols], preferred_element_type=F32)
        o_ref[0, :, cols] = x_ref[0, :, cols] + mod_ref[0, 2:3, cols] * y


def _out_proj(oa, ob, w_bf16, x, mod3):
    b, s, d = x.shape
    heads = oa.shape[1]
    tm = OUT_ROWS
    return pl.pallas_call(
        functools.partial(_out_kernel, heads=heads),
        grid=(b, s // tm),
        in_specs=[
            pl.BlockSpec((1, heads, tm, HEAD_DIM), lambda bi, si: (bi, 0, si, 0)),
            pl.BlockSpec((1, heads, tm, HEAD_DIM), lambda bi, si: (bi, 0, si, 0)),
            pl.BlockSpec(w_bf16.shape, lambda bi, si: (0, 0)),
            pl.BlockSpec((1, tm, d), lambda bi, si: (bi, si, 0)),
            pl.BlockSpec((1, 3, d), lambda bi, si: (bi, 0, 0)),
        ],
        out_specs=pl.BlockSpec((1, tm, d), lambda bi, si: (bi, si, 0)),
        out_shape=jax.ShapeDtypeStruct((b, s, d), F32),
        compiler_params=pltpu.CompilerParams(
            dimension_semantics=("parallel", "parallel"),
            vmem_limit_bytes=VMEM_LIMIT),
        name="out_proj",
    )(oa, ob, w_bf16, x, mod3)


def _rotary_tables(s):
    half = HEAD_DIM // 2
    inv_freq = ROPE_BASE ** (-np.arange(half, dtype=np.float64) / half)
    ang = np.arange(s, dtype=np.float64)[:, None] * inv_freq[None, :]
    cos, sin = np.cos(ang), np.sin(ang)
    return (np.concatenate([cos, cos], axis=-1).astype(np.float32),
            np.concatenate([-sin, sin], axis=-1).astype(np.float32))


def kernel(x, c, w_ada, b_ada, norm_gain, w_in, sb_q_gain, sb_k_gain, ret_norm_gain, w_out):
    b, s, d = x.shape
    depth = w_ada.shape[0]
    heads = w_in.shape[2] // (N_GROUPS * HEAD_DIM)
    cos_t, sin_t = _rotary_tables(s)
    din, qd, kd, cd = _retention_tables(heads, RET_CHUNK)
    c_pad = jnp.pad(c, ((0, ADALN_ROWS - b), (0, 0)))
    rope_spec = pl.BlockSpec((PROJ_ROWS, HEAD_DIM), lambda bi, si, g: (si, 0))
    for layer in range(depth):
        mod = _adaln(c_pad, w_ada[layer], b_ada[layer][None, :])
        mod3 = mod[:b].reshape(b, 3, d)
        w = w_in[layer].astype(BF16)
        qk_gain = jnp.stack([sb_q_gain[layer], sb_k_gain[layer]])[:, None, :]
        h, sb_qk = _norm_qk(x, mod3, norm_gain[layer][None, :], w, qk_gain, heads,
                            q_mult=float(1.0 / np.sqrt(HEAD_DIM)))
        vals = _in_proj(h, w, "identity", 2, 4)
        gates = _in_proj(h, w, "silu", 3, 4)
        ret_qk = _in_proj(h, w, "rotary", 4, 1, (cos_t, sin_t), (rope_spec, rope_spec),
                          second_mult=float(HEAD_DIM ** -0.5))
        oa = _sb_attention(sb_qk, vals, gates, heads)
        ob = _retention(ret_qk, vals, gates, heads, din, qd, kd, cd, ret_norm_gain[layer][None, :])
        x = _out_proj(oa, ob, w_out[layer].astype(BF16), x, mod3)
    return x
```

```python
import functools

import jax
import jax.numpy as jnp
import numpy as np
from jax import lax
from jax.experimental import pallas as pl
from jax.experimental.pallas import tpu as pltpu

HEAD_DIM = 128
ROPE_BASE = 10000.0
EPS = 1e-6
N_GROUPS = 8
F32 = jnp.float32
BF16 = jnp.bfloat16

ADALN_ROWS = 16
ADALN_STEPS = 16
NORM_ROWS = 1024
NORM_UNIT_ROWS = 512
NORM_CHUNK = 32
PROJ_ROWS = 2048
PROJ_UNIT_ROWS = 1024
PROJ_COLS = 256
SB_Q = 128
SB_K = 256
SB_TAIL = 128
SB_QK_ROWS = 256
SB_STEP_ROWS = 2048
SB_HEADS_PER_STEP = 2
RET_CHUNK = 256
RET_STEP_ROWS = 512
OUT_ROWS = 1024
OUT_COLS = 256
VMEM_LIMIT = 56 * 1024 * 1024
LOG_WEIGHT_FLOOR = -120.0
LOG2E = 1.4426950408889634


def _silu(v):
    half = 0.5 * v
    return half + half * jnp.tanh(half)


def _split_bf16(t):
    hi = t.astype(BF16)
    return hi, (t - hi.astype(F32)).astype(BF16)


def _adaln_kernel(c_ref, w_ref, b_ref, o_ref):
    s_hi, s_lo = _split_bf16(_silu(c_ref[...]))
    w_hi, w_lo = _split_bf16(w_ref[...])
    dot = functools.partial(jnp.dot, preferred_element_type=F32)
    o_ref[...] = dot(s_hi, w_hi) + (dot(s_hi, w_lo) + dot(s_lo, w_hi)) + b_ref[...]


def _adaln(c_pad, w, b):
    rows, d = c_pad.shape
    n = w.shape[1]
    tn = n // ADALN_STEPS
    return pl.pallas_call(
        _adaln_kernel,
        grid=(n // tn,),
        in_specs=[pl.BlockSpec((rows, d), lambda j: (0, 0)),
                  pl.BlockSpec((d, tn), lambda j: (0, j)),
                  pl.BlockSpec((1, tn), lambda j: (0, j))],
        out_specs=pl.BlockSpec((rows, tn), lambda j: (0, j)),
        out_shape=jax.ShapeDtypeStruct((rows, n), F32),
        compiler_params=pltpu.CompilerParams(
            dimension_semantics=("arbitrary",), vmem_limit_bytes=VMEM_LIMIT),
        name="adaln",
    )(c_pad, w, b)


def _norm_qk_kernel(x_ref, mod_ref, ng_ref, w_ref, gain_ref, h_ref, o_ref, *, heads, q_mult):
    shift = mod_ref[0, 0:1, :]
    gain_h = ng_ref[...] * (1.0 + mod_ref[0, 1:2, :])
    heads_per_chunk = PROJ_COLS // HEAD_DIM
    for r in range(x_ref.shape[1] // NORM_UNIT_ROWS):
        for ch in range(NORM_UNIT_ROWS // NORM_CHUNK):
            rows = slice(r * NORM_UNIT_ROWS + ch * NORM_CHUNK,
                         r * NORM_UNIT_ROWS + (ch + 1) * NORM_CHUNK)
            x = x_ref[0, rows, :]
            ms = jnp.mean(x * x, axis=-1, keepdims=True)
            h_ref[0, rows, :] = (x * lax.rsqrt(ms + EPS) * gain_h + shift).astype(BF16)
        rows = slice(r * NORM_UNIT_ROWS, (r + 1) * NORM_UNIT_ROWS)
        h = h_ref[0, rows, :]
        for c in range(2 * heads // heads_per_chunk):
            acc = jnp.dot(h, w_ref[:, c * PROJ_COLS:(c + 1) * PROJ_COLS],
                          preferred_element_type=F32)
            for j in range(heads_per_chunk):
                head = c * heads_per_chunk + j
                gain = gain_ref[head // heads] * (q_mult if head < heads else 1.0)
                a = acc[:, j * HEAD_DIM:(j + 1) * HEAD_DIM]
                ms = jnp.mean(a * a, axis=-1, keepdims=True)
                o_ref[0, head, rows, :] = (a * lax.rsqrt(ms + EPS) * gain).astype(BF16)


def _norm_qk(x, mod3, norm_gain, w_bf16, qk_gain, heads, q_mult):
    b, s, d = x.shape
    tm = NORM_ROWS
    cols = 2 * heads * HEAD_DIM
    return pl.pallas_call(
        functools.partial(_norm_qk_kernel, heads=heads, q_mult=q_mult),
        grid=(b, s // tm),
        in_specs=[pl.BlockSpec((1, tm, d), lambda bi, si: (bi, si, 0)),
                  pl.BlockSpec((1, 3, d), lambda bi, si: (bi, 0, 0)),
                  pl.BlockSpec((1, d), lambda bi, si: (0, 0)),
                  pl.BlockSpec((d, cols), lambda bi, si: (0, 0), pipeline_mode=pl.Buffered(1)),
                  pl.BlockSpec(qk_gain.shape, lambda bi, si: (0, 0, 0))],
        out_specs=[pl.BlockSpec((1, tm, d), lambda bi, si: (bi, si, 0)),
                   pl.BlockSpec((1, 2 * heads, tm, HEAD_DIM), lambda bi, si: (bi, 0, si, 0))],
        out_shape=[jax.ShapeDtypeStruct((b, s, d), BF16),
                   jax.ShapeDtypeStruct((b, 2 * heads, s, HEAD_DIM), BF16)],
        compiler_params=pltpu.CompilerParams(
            dimension_semantics=("parallel", "parallel"), vmem_limit_bytes=VMEM_LIMIT),
        name="norm_qk",
    )(x, mod3, norm_gain, w_bf16, qk_gain)


def _proj_kernel(h_ref, w_ref, *refs, kind, heads, first_mult, second_mult):
    o_ref = refs[-1]
    g = pl.program_id(2)
    mult = jnp.where(g == 0, first_mult, second_mult).astype(F32)
    if kind == "rotary":
        def epilogue(a, rows):
            return (a * (refs[0][rows, :] * mult)
                    + pltpu.roll(a, HEAD_DIM // 2, 1) * (refs[1][rows, :] * mult))
    elif kind == "silu":
        epilogue = lambda a, rows: _silu(a)
    else:
        epilogue = lambda a, rows: a

    heads_per_chunk = PROJ_COLS // HEAD_DIM
    for r in range(h_ref.shape[1] // PROJ_UNIT_ROWS):
        rows = slice(r * PROJ_UNIT_ROWS, (r + 1) * PROJ_UNIT_ROWS)
        h = h_ref[0, rows, :]
        for c in range(heads // heads_per_chunk):
            acc = jnp.dot(h, w_ref[:, c * PROJ_COLS:(c + 1) * PROJ_COLS],
                          preferred_element_type=F32)
            for j in range(heads_per_chunk):
                a = acc[:, j * HEAD_DIM:(j + 1) * HEAD_DIM]
                o_ref[0, c * heads_per_chunk + j, rows, :] = epilogue(a, rows).astype(BF16)


def _in_proj(h, w_bf16, kind, first_group, group_stride, extra=(), extra_specs=(),
             first_mult=1.0, second_mult=1.0):
    b, s, d = h.shape
    tn = w_bf16.shape[1] // N_GROUPS
    heads = tn // HEAD_DIM
    tm = PROJ_ROWS
    return pl.pallas_call(
        functools.partial(_proj_kernel, kind=kind, heads=heads,
                          first_mult=first_mult, second_mult=second_mult),
        grid=(b, s // tm, 2),
        in_specs=[
            pl.BlockSpec((1, tm, d), lambda bi, si, g: (bi, si, 0)),
            pl.BlockSpec((d, tn), lambda bi, si, g: (0, first_group + group_stride * g)),
            *extra_specs,
        ],
        out_specs=pl.BlockSpec((1, heads, tm, HEAD_DIM), lambda bi, si, g: (bi, g, si, 0)),
        out_shape=jax.ShapeDtypeStruct((b, 2 * heads, s, HEAD_DIM), BF16),
        compiler_params=pltpu.CompilerParams(
            dimension_semantics=("parallel", "parallel", "arbitrary"),
            vmem_limit_bytes=VMEM_LIMIT),
        name="in_proj_" + kind,
    )(h, w_bf16, *extra)


def _sb_scores(q, k_blk):
    z = lax.dot_general(q, k_blk, (((1,), (1,)), ((), ())), preferred_element_type=F32)
    sp = jnp.maximum(z, 0.0) + jnp.log(1.0 + jnp.exp2(jnp.abs(z) * (-LOG2E)))
    return sp, z - sp


def _sb_later(sp, neg_upper):
    return jnp.dot(sp.astype(BF16), neg_upper, preferred_element_type=F32)


def _sb_edge_window(q, k_ref, v_ref, hh, q0, neg_upper):
    row = lax.broadcasted_iota(jnp.int32, (SB_Q, SB_K), 0)
    col = lax.broadcasted_iota(jnp.int32, (SB_Q, SB_K), 1)
    d_start = jnp.maximum(q0 - SB_Q, 0)
    p_start = jnp.maximum(d_start - SB_K, 0)
    mask_d = col < row + (q0 - d_start)
    mask_p = col < (d_start - p_start)
    d_start = pl.multiple_of(d_start, SB_Q)
    p_start = pl.multiple_of(p_start, SB_Q)

    sp_d, logb_d = _sb_scores(q, k_ref[0, hh, pl.ds(d_start, SB_K), :])
    sp_p, logb_p = _sb_scores(q, k_ref[0, hh, pl.ds(p_start, SB_K), :])
    sp_d = jnp.where(mask_d, sp_d, 0.0)
    sp_p = jnp.where(mask_p, sp_p, 0.0)
    w_d = jnp.where(mask_d, jnp.exp(logb_d + _sb_later(sp_d, neg_upper)), 0.0)
    carry = -jnp.sum(sp_d, axis=1, keepdims=True)
    w_p = jnp.where(mask_p, jnp.exp(logb_p + _sb_later(sp_p, neg_upper) + carry), 0.0)
    acc = (jnp.dot(w_d.astype(BF16), v_ref[0, hh, pl.ds(d_start, SB_K), :], preferred_element_type=F32)
           + jnp.dot(w_p.astype(BF16), v_ref[0, hh, pl.ds(p_start, SB_K), :], preferred_element_type=F32))
    carry = carry - jnp.sum(sp_p, axis=1, keepdims=True)
    return carry, acc, p_start


def _sb_interior_windows(q_ref, k_ref, v_ref, step_start, neg_upper, heads_per_step, first_block):
    win = 2 * SB_K
    per_mm = SB_QK_ROWS // SB_Q
    span = win + (per_mm - 1) * SB_Q
    w_start = step_start - (win - SB_Q)
    zs, keys = [], []
    for hh in range(heads_per_step):
        for m in range(first_block, SB_STEP_ROWS // SB_QK_ROWS):
            k_start = pl.multiple_of(w_start + m * SB_QK_ROWS, SB_Q)
            z = lax.dot_general(q_ref[0, hh, m * SB_QK_ROWS:(m + 1) * SB_QK_ROWS, :],
                                k_ref[0, hh, pl.ds(k_start, span), :],
                                (((1,), (1,)), ((), ())), preferred_element_type=F32)
            zs += [z[u * SB_Q:(u + 1) * SB_Q, u * SB_Q:u * SB_Q + win] for u in range(per_mm)]
            keys += [(hh, m * per_mm + u) for u in range(per_mm)]
    z = jnp.concatenate(zs, axis=0).astype(BF16)
    n = z.shape[0]
    l = jnp.log(1.0 + jnp.exp(-jnp.abs(z)))
    sp = jnp.maximum(z, 0.0) + l
    logb = jnp.minimum(z, 0.0) - l
    row = lax.broadcasted_iota(jnp.int32, (n, SB_Q), 0) & (SB_Q - 1)
    col = lax.broadcasted_iota(jnp.int32, (n, SB_Q), 1)
    past = jnp.where(col < row, 1.0, 0.0).astype(BF16)
    diag = lambda t: jnp.concatenate([t[:, :SB_K - SB_Q], t[:, SB_K - SB_Q:] * past], axis=1)
    sp_p, sp_d = sp[:, :SB_K], diag(sp[:, SB_K:])
    w_d = diag(jnp.exp(logb[:, SB_K:] + _sb_later(sp_d, neg_upper).astype(BF16)))
    carry = -jnp.sum(sp_d.astype(F32), axis=1, keepdims=True)
    w_p = jnp.exp(logb[:, :SB_K] + (_sb_later(sp_p, neg_upper) + carry).astype(BF16))
    carry = carry - jnp.sum(sp_p.astype(F32), axis=1, keepdims=True)
    w = jnp.concatenate([w_p, w_d], axis=1)
    out = {}
    for c, (hh, sub) in enumerate(keys):
        p_start = pl.multiple_of(w_start + sub * SB_Q, SB_Q)
        acc = jnp.dot(w[c * SB_Q:(c + 1) * SB_Q], v_ref[0, hh, pl.ds(p_start, win), :],
                      preferred_element_type=F32)
        out[(hh, sub)] = (carry[c * SB_Q:(c + 1) * SB_Q], acc, p_start)
    return out


def _sb_kernel(q_ref, k_ref, v_ref, g_ref, o_ref, *, heads_per_step):
    i = pl.program_id(2)
    urow = lax.broadcasted_iota(jnp.int32, (SB_K, SB_K), 0)
    ucol = lax.broadcasted_iota(jnp.int32, (SB_K, SB_K), 1)
    neg_upper = jnp.where(urow > ucol, -1.0, 0.0).astype(BF16)
    neg_upper_tail = neg_upper[:SB_TAIL, :SB_TAIL]
    subs = SB_STEP_ROWS // SB_Q

    def run(edge):
        step_start = i * SB_STEP_ROWS
        edge_blocks = -(-(2 * SB_K - SB_Q) // SB_QK_ROWS) if edge else 0
        stacked = _sb_interior_windows(q_ref, k_ref, v_ref, step_start, neg_upper,
                                       heads_per_step, edge_blocks)
        chains = []
        for hh in range(heads_per_step):
            for sub in range(subs):
                rows = slice(sub * SB_Q, (sub + 1) * SB_Q)
                q = q_ref[0, hh, rows, :]
                if (hh, sub) in stacked:
                    carry, acc, p_start = stacked[(hh, sub)]
                else:
                    carry, acc, p_start = _sb_edge_window(
                        q, k_ref, v_ref, hh, step_start + sub * SB_Q, neg_upper)
                o_ref[0, hh, rows, :] = (acc * g_ref[0, hh, rows, :].astype(F32)).astype(BF16)
                live = (jnp.max(carry) > LOG_WEIGHT_FLOOR) & (p_start > 0)
                chains.append((hh, rows, q, carry, acc, p_start, live))

        def older_tiles(hh, rows, q, carry, acc, p_start, live):
            tiles_left = p_start // SB_TAIL

            def cond(state):
                jj, live, _, _ = state
                return (jj < tiles_left) & live

            def body(state):
                jj, _, carry, acc = state
                start = pl.multiple_of(p_start - (jj + 1) * SB_TAIL, SB_TAIL)
                sp, logb = _sb_scores(q, k_ref[0, hh, pl.ds(start, SB_TAIL), :])
                w = jnp.exp(logb + _sb_later(sp, neg_upper_tail) + carry)
                acc = acc + jnp.dot(w.astype(BF16), v_ref[0, hh, pl.ds(start, SB_TAIL), :],
                                    preferred_element_type=F32)
                carry = carry - jnp.sum(sp, axis=1, keepdims=True)
                return jj + 1, jnp.max(carry) > LOG_WEIGHT_FLOOR, carry, acc

            _, _, _, acc = lax.while_loop(cond, body, (jnp.int32(0), live, carry, acc))
            o_ref[0, hh, rows, :] = (acc * g_ref[0, hh, rows, :].astype(F32)).astype(BF16)

        any_live = functools.reduce(jnp.logical_or, [ch[-1] for ch in chains])

        @pl.when(any_live)
        def _():
            for chain in chains:
                pl.when(chain[-1])(functools.partial(older_tiles, *chain))

    first_interior = (SB_Q + SB_K + SB_STEP_ROWS - 1) // SB_STEP_ROWS
    pl.when(i >= first_interior)(lambda: run(False))
    pl.when(i < first_interior)(lambda: run(True))


def _sb_attention(qk, v, gate, heads):
    b, _, s, d = qk.shape
    hb = SB_HEADS_PER_STEP
    groups = heads // hb
    t = SB_STEP_ROWS
    return pl.pallas_call(
        functools.partial(_sb_kernel, heads_per_step=hb),
        grid=(b, groups, s // t),
        in_specs=[
            pl.BlockSpec((1, hb, t, d), lambda bi, g, i: (bi, g, i, 0)),
            pl.BlockSpec((1, hb, s, d), lambda bi, g, i: (bi, groups + g, 0, 0)),
            pl.BlockSpec((1, hb, s, d), lambda bi, g, i: (bi, g, 0, 0)),
            pl.BlockSpec((1, hb, t, d), lambda bi, g, i: (bi, g, i, 0)),
        ],
        out_specs=pl.BlockSpec((1, hb, t, d), lambda bi, g, i: (bi, g, i, 0)),
        out_shape=jax.ShapeDtypeStruct((b, heads, s, d), BF16),
        compiler_params=pltpu.CompilerParams(
            dimension_semantics=("parallel", "parallel", "arbitrary"),
            vmem_limit_bytes=VMEM_LIMIT),
        name="sb_attn",
    )(qk, qk, v, gate)


def _ret_kernel(q_ref, k_ref, v_ref, g_ref, din_ref, qd_ref, kd_ref, cd_ref, gain_ref,
                o_ref, state_scr, *, heads):
    ci = pl.program_id(1)

    @pl.when(ci == 0)
    def _():
        state_scr[...] = jnp.zeros_like(state_scr)

    for h in range(heads):
        state = state_scr[h]
        for part in range(q_ref.shape[2] // RET_CHUNK):
            rows = slice(part * RET_CHUNK, (part + 1) * RET_CHUNK)
            qc = q_ref[0, h, rows, :]
            kc = k_ref[0, h, rows, :]
            vc = v_ref[0, h, rows, :]

            scores = lax.dot_general(qc, kc, (((1,), (1,)), ((), ())), preferred_element_type=F32)
            inner = jnp.dot((scores * din_ref[h]).astype(BF16), vc, preferred_element_type=F32)
            q_dec = (qc.astype(F32) * qd_ref[h]).astype(BF16)
            cross = jnp.dot(q_dec, state.astype(BF16), preferred_element_type=F32)
            k_dec = (kc.astype(F32) * kd_ref[h]).astype(BF16)
            kv = lax.dot_general(k_dec, vc, (((0,), (0,)), ((), ())), preferred_element_type=F32)
            state = state * cd_ref[h] + kv

            o = inner + cross
            mu = jnp.mean(o, axis=-1, keepdims=True)
            cen = o - mu
            var = jnp.mean(cen * cen, axis=-1, keepdims=True)
            y = cen * lax.rsqrt(var + EPS) * gain_ref[:, h * HEAD_DIM:(h + 1) * HEAD_DIM]
            o_ref[0, h, rows, :] = (y * g_ref[0, h, rows, :].astype(F32)).astype(BF16)
        state_scr[h] = state


def _retention(qk, v, gate, heads, din, qd, kd, cd, gain):
    b, _, s, d = qk.shape
    c = RET_STEP_ROWS
    blk = lambda group: pl.BlockSpec((1, heads, c, d), lambda bi, ci: (bi, group, ci, 0))
    const = lambda arr: pl.BlockSpec(arr.shape, lambda bi, ci: (0,) * arr.ndim)
    return pl.pallas_call(
        functools.partial(_ret_kernel, heads=heads),
        grid=(b, s // c),
        in_specs=[blk(0), blk(1), blk(1), blk(1),
                  const(din), const(qd), const(kd), const(cd), const(gain)],
        out_specs=pl.BlockSpec((1, heads, c, d), lambda bi, ci: (bi, 0, ci, 0)),
        out_shape=jax.ShapeDtypeStruct((b, heads, s, d), BF16),
        scratch_shapes=[pltpu.VMEM((heads, d, d), F32)],
        compiler_params=pltpu.CompilerParams(
            dimension_semantics=("parallel", "arbitrary"),
            vmem_limit_bytes=VMEM_LIMIT),
        name="retention",
    )(qk, qk, v, gate, din, qd, kd, cd, gain)


def _retention_tables(heads, c):
    log_gamma = np.log1p(-np.exp2(-5.0 - np.arange(heads, dtype=np.float64)))
    idx = np.arange(c, dtype=np.float64)
    rel = idx[:, None] - idx[None, :]
    din = np.where(rel[None] >= 0, np.exp(np.maximum(rel, 0.0)[None] * log_gamma[:, None, None]), 0.0)
    qd = np.exp((idx[None, :] + 1.0) * log_gamma[:, None])[..., None]
    kd = np.exp((c - 1.0 - idx[None, :]) * log_gamma[:, None])[..., None]
    cd = np.exp(c * log_gamma)[:, None, None]
    bc = lambda t: np.ascontiguousarray(
        np.broadcast_to(t, t.shape[:-1] + (HEAD_DIM,)), dtype=np.float32)
    return din.astype(np.float32), bc(qd), bc(kd), bc(cd)


def _out_kernel(a_ref, r_ref, w_ref, x_ref, mod_ref, o_ref, *, heads):
    parts = [a_ref[0, j] for j in range(heads)] + [r_ref[0, j] for j in range(heads)]
    mix = jnp.concatenate(parts, axis=-1)
    for c in range(w_ref.shape[1] // OUT_COLS):
        cols = slice(c * OUT_COLS, (c + 1) * OUT_COLS)
        y = jnp.dot(mix, w_ref[:, cols], preferred_element_type=F32)
        o_ref[0, :, cols] = x_ref[0, :, cols] + mod_ref[0, 2:3, cols] * y


def _out_proj(oa, ob, w_bf16, x, mod3):
    b, s, d = x.shape
    heads = oa.shape[1]
    tm = OUT_ROWS
    return pl.pallas_call(
        functools.partial(_out_kernel, heads=heads),
        grid=(b, s // tm),
        in_specs=[
            pl.BlockSpec((1, heads, tm, HEAD_DIM), lambda bi, si: (bi, 0, si, 0)),
            pl.BlockSpec((1, heads, tm, HEAD_DIM), lambda bi, si: (bi, 0, si, 0)),
            pl.BlockSpec(w_bf16.shape, lambda bi, si: (0, 0), pipeline_mode=pl.Buffered(1)),
            pl.BlockSpec((1, tm, d), lambda bi, si: (bi, si, 0)),
            pl.BlockSpec((1, 3, d), lambda bi, si: (bi, 0, 0)),
        ],
        out_specs=pl.BlockSpec((1, tm, d), lambda bi, si: (bi, si, 0)),
        out_shape=jax.ShapeDtypeStruct((b, s, d), F32),
        compiler_params=pltpu.CompilerParams(
            dimension_semantics=("parallel", "parallel"),
            vmem_limit_bytes=VMEM_LIMIT),
        name="out_proj",
    )(oa, ob, w_bf16, x, mod3)


def _rotary_tables(s):
    half = HEAD_DIM // 2
    inv_freq = ROPE_BASE ** (-np.arange(half, dtype=np.float64) / half)
    ang = np.arange(s, dtype=np.float64)[:, None] * inv_freq[None, :]
    cos, sin = np.cos(ang), np.sin(ang)
    return (np.concatenate([cos, cos], axis=-1).astype(np.float32),
            np.concatenate([-sin, sin], axis=-1).astype(np.float32))


def kernel(x, c, w_ada, b_ada, norm_gain, w_in, sb_q_gain, sb_k_gain, ret_norm_gain, w_out):
    b, s, d = x.shape
    depth = w_ada.shape[0]
    heads = w_in.shape[2] // (N_GROUPS * HEAD_DIM)
    cos_t, sin_t = _rotary_tables(s)
    din, qd, kd, cd = _retention_tables(heads, RET_CHUNK)
    c_pad = jnp.pad(c, ((0, ADALN_ROWS - b), (0, 0)))
    rope_spec = pl.BlockSpec((PROJ_ROWS, HEAD_DIM), lambda bi, si, g: (si, 0))
    for layer in range(depth):
        mod = _adaln(c_pad, w_ada[layer], b_ada[layer][None, :])
        mod3 = mod[:b].reshape(b, 3, d)
        w = w_in[layer].astype(BF16)
        qk_gain = jnp.stack([sb_q_gain[layer], sb_k_gain[layer]])[:, None, :]
        h, sb_qk = _norm_qk(x, mod3, norm_gain[layer][None, :], w, qk_gain, heads,
                            q_mult=float(1.0 / np.sqrt(HEAD_DIM)))
        vals = _in_proj(h, w, "identity", 2, 4)
        gates = _in_proj(h, w, "silu", 3, 4)
        ret_qk = _in_proj(h, w, "rotary", 4, 1, (cos_t, sin_t), (rope_spec, rope_spec),
                          second_mult=float(HEAD_DIM ** -0.5))
        oa = _sb_attention(sb_qk, vals, gates, heads)
        ob = _retention(ret_qk, vals, gates, heads, din, qd, kd, cd, ret_norm_gain[layer][None, :])
        x = _out_proj(oa, ob, w_out[layer].astype(BF16), x, mod3)
    return x
```

```python
import functools

import jax
import jax.numpy as jnp
import numpy as np
from jax import lax
from jax.experimental import pallas as pl
from jax.experimental.pallas import tpu as pltpu

HEAD_DIM = 128
ROPE_BASE = 10000.0
EPS = 1e-6
N_GROUPS = 8
F32 = jnp.float32
BF16 = jnp.bfloat16

ADALN_ROWS = 16
ADALN_STEPS = 8
NORM_ROWS = 1024
NORM_UNIT_ROWS = 512
NORM_CHUNK = 32
PROJ_ROWS = 2048
PROJ_UNIT_ROWS = 1024
PROJ_COLS = 256
SB_Q = 128
SB_K = 256
SB_TAIL = 128
SB_QK_ROWS = 256
SB_STEP_ROWS = 2048
SB_HEADS_PER_STEP = 2
RET_CHUNK = 256
RET_PIECES_PER_GAP = 4
OUT_ROWS = 1024
OUT_COLS = 256
VMEM_LIMIT = 56 * 1024 * 1024
LOG_WEIGHT_FLOOR = -120.0
LOG2E = 1.4426950408889634


def _silu(v):
    half = 0.5 * v
    return half + half * jnp.tanh(half)


def _split_bf16(t):
    hi = t.astype(BF16)
    return hi, (t - hi.astype(F32)).astype(BF16)


def _adaln_kernel(c_ref, w_ref, b_ref, o_ref):
    s_hi, s_lo = _split_bf16(_silu(c_ref[...]))
    w_hi, w_lo = _split_bf16(w_ref[...])
    dot = functools.partial(jnp.dot, preferred_element_type=F32)
    o_ref[...] = dot(s_hi, w_hi) + (dot(s_hi, w_lo) + dot(s_lo, w_hi)) + b_ref[...]


def _adaln(c_pad, w, b):
    rows, d = c_pad.shape
    n = w.shape[1]
    tn = n // ADALN_STEPS
    return pl.pallas_call(
        _adaln_kernel,
        grid=(n // tn,),
        in_specs=[pl.BlockSpec((rows, d), lambda j: (0, 0)),
                  pl.BlockSpec((d, tn), lambda j: (0, j)),
                  pl.BlockSpec((1, tn), lambda j: (0, j))],
        out_specs=pl.BlockSpec((rows, tn), lambda j: (0, j)),
        out_shape=jax.ShapeDtypeStruct((rows, n), F32),
        compiler_params=pltpu.CompilerParams(
            dimension_semantics=("arbitrary",), vmem_limit_bytes=VMEM_LIMIT),
        name="adaln",
    )(c_pad, w, b)


def _norm_qk_kernel(x_ref, mod_ref, ng_ref, w_ref, gain_ref, h_ref, o_ref, *, heads, q_mult):
    shift = mod_ref[0, 0:1, :]
    gain_h = ng_ref[...] * (1.0 + mod_ref[0, 1:2, :])
    heads_per_chunk = PROJ_COLS // HEAD_DIM
    for r in range(x_ref.shape[1] // NORM_UNIT_ROWS):
        for ch in range(NORM_UNIT_ROWS // NORM_CHUNK):
            rows = slice(r * NORM_UNIT_ROWS + ch * NORM_CHUNK,
                         r * NORM_UNIT_ROWS + (ch + 1) * NORM_CHUNK)
            x = x_ref[0, rows, :]
            ms = jnp.mean(x * x, axis=-1, keepdims=True)
            h_ref[0, rows, :] = (x * lax.rsqrt(ms + EPS) * gain_h + shift).astype(BF16)
        rows = slice(r * NORM_UNIT_ROWS, (r + 1) * NORM_UNIT_ROWS)
        h = h_ref[0, rows, :]
        for c in range(2 * heads // heads_per_chunk):
            w_chunk = w_ref[:, c * PROJ_COLS:(c + 1) * PROJ_COLS].astype(BF16)
            acc = jnp.dot(h, w_chunk, preferred_element_type=F32)
            for j in range(heads_per_chunk):
                head = c * heads_per_chunk + j
                gain = gain_ref[head // heads] * (q_mult if head < heads else 1.0)
                a = acc[:, j * HEAD_DIM:(j + 1) * HEAD_DIM]
                ms = jnp.mean(a * a, axis=-1, keepdims=True)
                o_ref[0, head, rows, :] = (a * lax.rsqrt(ms + EPS) * gain).astype(BF16)


def _norm_qk(x, mod3, norm_gain, w, qk_gain, heads, q_mult):
    b, s, d = x.shape
    tm = NORM_ROWS
    cols = 2 * heads * HEAD_DIM
    return pl.pallas_call(
        functools.partial(_norm_qk_kernel, heads=heads, q_mult=q_mult),
        grid=(b, s // tm),
        in_specs=[pl.BlockSpec((1, tm, d), lambda bi, si: (bi, si, 0)),
                  pl.BlockSpec((1, 3, d), lambda bi, si: (bi, 0, 0)),
                  pl.BlockSpec((1, d), lambda bi, si: (0, 0)),
                  pl.BlockSpec((d, cols), lambda bi, si: (0, 0), pipeline_mode=pl.Buffered(1)),
                  pl.BlockSpec(qk_gain.shape, lambda bi, si: (0, 0, 0))],
        out_specs=[pl.BlockSpec((1, tm, d), lambda bi, si: (bi, si, 0)),
                   pl.BlockSpec((1, 2 * heads, tm, HEAD_DIM), lambda bi, si: (bi, 0, si, 0))],
        out_shape=[jax.ShapeDtypeStruct((b, s, d), BF16),
                   jax.ShapeDtypeStruct((b, 2 * heads, s, HEAD_DIM), BF16)],
        compiler_params=pltpu.CompilerParams(
            dimension_semantics=("parallel", "parallel"), vmem_limit_bytes=VMEM_LIMIT),
        name="norm_qk",
    )(x, mod3, norm_gain, w, qk_gain)


def _proj_kernel(h_ref, w_ref, *refs, kind, heads, first_mult, second_mult):
    o_ref = refs[-1]
    g = pl.program_id(2)
    mult = jnp.where(g == 0, first_mult, second_mult).astype(F32)
    if kind == "rotary":
        def epilogue(a, rows):
            return (a * (refs[0][rows, :] * mult)
                    + pltpu.roll(a, HEAD_DIM // 2, 1) * (refs[1][rows, :] * mult))
    elif kind == "silu":
        epilogue = lambda a, rows: _silu(a)
    else:
        epilogue = lambda a, rows: a

    heads_per_chunk = PROJ_COLS // HEAD_DIM
    for c in range(heads // heads_per_chunk):
        w_chunk = w_ref[:, c * PROJ_COLS:(c + 1) * PROJ_COLS].astype(BF16)
        for r in range(h_ref.shape[1] // PROJ_UNIT_ROWS):
            rows = slice(r * PROJ_UNIT_ROWS, (r + 1) * PROJ_UNIT_ROWS)
            acc = jnp.dot(h_ref[0, rows, :], w_chunk, preferred_element_type=F32)
            for j in range(heads_per_chunk):
                a = acc[:, j * HEAD_DIM:(j + 1) * HEAD_DIM]
                o_ref[0, c * heads_per_chunk + j, rows, :] = epilogue(a, rows).astype(BF16)


def _in_proj(h, w, kind, first_group, group_stride, extra=(), extra_specs=(),
             first_mult=1.0, second_mult=1.0):
    b, s, d = h.shape
    tn = w.shape[1] // N_GROUPS
    heads = tn // HEAD_DIM
    tm = PROJ_ROWS
    return pl.pallas_call(
        functools.partial(_proj_kernel, kind=kind, heads=heads,
                          first_mult=first_mult, second_mult=second_mult),
        grid=(b, s // tm, 2),
        in_specs=[
            pl.BlockSpec((1, tm, d), lambda bi, si, g: (bi, si, 0)),
            pl.BlockSpec((d, tn), lambda bi, si, g: (0, first_group + group_stride * g)),
            *extra_specs,
        ],
        out_specs=pl.BlockSpec((1, heads, tm, HEAD_DIM), lambda bi, si, g: (bi, g, si, 0)),
        out_shape=jax.ShapeDtypeStruct((b, 2 * heads, s, HEAD_DIM), BF16),
        compiler_params=pltpu.CompilerParams(
            dimension_semantics=("parallel", "parallel", "arbitrary"),
            vmem_limit_bytes=VMEM_LIMIT),
        name="in_proj_" + kind,
    )(h, w, *extra)


def _sb_scores(q, k_blk):
    z = lax.dot_general(q, k_blk, (((1,), (1,)), ((), ())), preferred_element_type=F32)
    sp = jnp.maximum(z, 0.0) + jnp.log(1.0 + jnp.exp2(jnp.abs(z) * (-LOG2E)))
    return sp, z - sp


def _sb_later(sp, neg_upper):
    return jnp.dot(sp.astype(BF16), neg_upper, preferred_element_type=F32)


def _sb_edge_window(q, k_ref, v_ref, hh, q0, neg_upper):
    row = lax.broadcasted_iota(jnp.int32, (SB_Q, SB_K), 0)
    col = lax.broadcasted_iota(jnp.int32, (SB_Q, SB_K), 1)
    d_start = jnp.maximum(q0 - SB_Q, 0)
    p_start = jnp.maximum(d_start - SB_K, 0)
    mask_d = col < row + (q0 - d_start)
    mask_p = col < (d_start - p_start)
    d_start = pl.multiple_of(d_start, SB_Q)
    p_start = pl.multiple_of(p_start, SB_Q)

    sp_d, logb_d = _sb_scores(q, k_ref[0, hh, pl.ds(d_start, SB_K), :])
    sp_p, logb_p = _sb_scores(q, k_ref[0, hh, pl.ds(p_start, SB_K), :])
    sp_d = jnp.where(mask_d, sp_d, 0.0)
    sp_p = jnp.where(mask_p, sp_p, 0.0)
    w_d = jnp.where(mask_d, jnp.exp(logb_d + _sb_later(sp_d, neg_upper)), 0.0)
    carry = -jnp.sum(sp_d, axis=1, keepdims=True)
    w_p = jnp.where(mask_p, jnp.exp(logb_p + _sb_later(sp_p, neg_upper) + carry), 0.0)
    acc = (jnp.dot(w_d.astype(BF16), v_ref[0, hh, pl.ds(d_start, SB_K), :], preferred_element_type=F32)
           + jnp.dot(w_p.astype(BF16), v_ref[0, hh, pl.ds(p_start, SB_K), :], preferred_element_type=F32))
    carry = carry - jnp.sum(sp_p, axis=1, keepdims=True)
    return carry, acc, p_start


def _sb_interior_windows(q_ref, k_ref, v_ref, step_start, neg_upper, heads_per_step, first_block,
                         filler):
    win = 2 * SB_K
    per_mm = SB_QK_ROWS // SB_Q
    span = win + (per_mm - 1) * SB_Q
    w_start = step_start - (win - SB_Q)
    zs, keys = [], []
    for hh in range(heads_per_step):
        for m in range(first_block, SB_STEP_ROWS // SB_QK_ROWS):
            k_start = pl.multiple_of(w_start + m * SB_QK_ROWS, SB_Q)
            z = lax.dot_general(q_ref[0, hh, m * SB_QK_ROWS:(m + 1) * SB_QK_ROWS, :],
                                k_ref[0, hh, pl.ds(k_start, span), :],
                                (((1,), (1,)), ((), ())), preferred_element_type=F32)
            zs += [z[u * SB_Q:(u + 1) * SB_Q, u * SB_Q:u * SB_Q + win] for u in range(per_mm)]
            keys += [(hh, m * per_mm + u) for u in range(per_mm)]
    filler()
    z = jnp.concatenate(zs, axis=0).astype(BF16)
    n = z.shape[0]
    l = jnp.log(1.0 + jnp.exp(-jnp.abs(z)))
    sp = jnp.maximum(z, 0.0) + l
    logb = jnp.minimum(z, 0.0) - l
    filler()
    row = lax.broadcasted_iota(jnp.int32, (n, SB_Q), 0) & (SB_Q - 1)
    col = lax.broadcasted_iota(jnp.int32, (n, SB_Q), 1)
    past = jnp.where(col < row, 1.0, 0.0).astype(BF16)
    diag = lambda t: jnp.concatenate([t[:, :SB_K - SB_Q], t[:, SB_K - SB_Q:] * past], axis=1)
    sp_p, sp_d = sp[:, :SB_K], diag(sp[:, SB_K:])
    w_d = diag(jnp.exp(logb[:, SB_K:] + _sb_later(sp_d, neg_upper).astype(BF16)))
    filler()
    carry = -jnp.sum(sp_d.astype(F32), axis=1, keepdims=True)
    w_p = jnp.exp(logb[:, :SB_K] + (_sb_later(sp_p, neg_upper) + carry).astype(BF16))
    carry = carry - jnp.sum(sp_p.astype(F32), axis=1, keepdims=True)
    filler()
    w = jnp.concatenate([w_p, w_d], axis=1)
    out = {}
    for c, (hh, sub) in enumerate(keys):
        p_start = pl.multiple_of(w_start + sub * SB_Q, SB_Q)
        acc = jnp.dot(w[c * SB_Q:(c + 1) * SB_Q], v_ref[0, hh, pl.ds(p_start, win), :],
                      preferred_element_type=F32)
        out[(hh, sub)] = (carry[c * SB_Q:(c + 1) * SB_Q], acc, p_start)
    return out


def _mixer_kernel(q_ref, k_ref, v_ref, g_ref, rq_ref, rk_ref, rv_ref, rg_ref,
                  din_ref, qd_ref, kd_ref, cd_ref, gain_ref, o_ref, ro_ref, state_scr,
                  *, heads_per_step):
    i = pl.program_id(2)

    @pl.when(i == 0)
    def _():
        state_scr[...] = jnp.zeros_like(state_scr)

    urow = lax.broadcasted_iota(jnp.int32, (SB_K, SB_K), 0)
    ucol = lax.broadcasted_iota(jnp.int32, (SB_K, SB_K), 1)
    neg_upper = jnp.where(urow > ucol, -1.0, 0.0).astype(BF16)
    neg_upper_tail = neg_upper[:SB_TAIL, :SB_TAIL]
    subs = SB_STEP_ROWS // SB_Q

    def run(edge):
        step_start = i * SB_STEP_ROWS
        edge_blocks = -(-(2 * SB_K - SB_Q) // SB_QK_ROWS) if edge else 0
        pieces = _ret_pieces(rq_ref, rk_ref, rv_ref, rg_ref, din_ref, qd_ref, kd_ref, cd_ref,
                             gain_ref, ro_ref, state_scr, heads_per_step)

        def filler():
            for piece in [pieces.pop(0) for _ in range(min(RET_PIECES_PER_GAP, len(pieces)))]:
                piece()

        stacked = _sb_interior_windows(q_ref, k_ref, v_ref, step_start, neg_upper,
                                       heads_per_step, edge_blocks, filler)
        chains = []
        for hh in range(heads_per_step):
            for sub in range(subs):
                rows = slice(sub * SB_Q, (sub + 1) * SB_Q)
                q = q_ref[0, hh, rows, :]
                if (hh, sub) in stacked:
                    carry, acc, p_start = stacked[(hh, sub)]
                else:
                    carry, acc, p_start = _sb_edge_window(
                        q, k_ref, v_ref, hh, step_start + sub * SB_Q, neg_upper)
                o_ref[0, hh, rows, :] = (acc * g_ref[0, hh, rows, :].astype(F32)).astype(BF16)
                live = (jnp.max(carry) > LOG_WEIGHT_FLOOR) & (p_start > 0)
                chains.append((hh, rows, q, carry, acc, p_start, live))

        while pieces:
            filler()

        def older_tiles(hh, rows, q, carry, acc, p_start, live):
            tiles_left = p_start // SB_TAIL

            def cond(state):
                jj, live, _, _ = state
                return (jj < tiles_left) & live

            def body(state):
                jj, _, carry, acc = state
                start = pl.multiple_of(p_start - (jj + 1) * SB_TAIL, SB_TAIL)
                sp, logb = _sb_scores(q, k_ref[0, hh, pl.ds(start, SB_TAIL), :])
                w = jnp.exp(logb + _sb_later(sp, neg_upper_tail) + carry)
                acc = acc + jnp.dot(w.astype(BF16), v_ref[0, hh, pl.ds(start, SB_TAIL), :],
                                    preferred_element_type=F32)
                carry = carry - jnp.sum(sp, axis=1, keepdims=True)
                return jj + 1, jnp.max(carry) > LOG_WEIGHT_FLOOR, carry, acc

            _, _, _, acc = lax.while_loop(cond, body, (jnp.int32(0), live, carry, acc))
            o_ref[0, hh, rows, :] = (acc * g_ref[0, hh, rows, :].astype(F32)).astype(BF16)

        any_live = functools.reduce(jnp.logical_or, [ch[-1] for ch in chains])

        @pl.when(any_live)
        def _():
            for chain in chains:
                pl.when(chain[-1])(functools.partial(older_tiles, *chain))

    first_interior = (SB_Q + SB_K + SB_STEP_ROWS - 1) // SB_STEP_ROWS
    pl.when(i >= first_interior)(lambda: run(False))
    pl.when(i < first_interior)(lambda: run(True))


def _mixers(sb_qk, ret_qk, vals, gates, heads, din, qd, kd, cd, gain):
    b, _, s, d = sb_qk.shape
    hb = SB_HEADS_PER_STEP
    groups = heads // hb
    t = SB_STEP_ROWS
    step = lambda off: pl.BlockSpec((1, hb, t, d), lambda bi, g, i: (bi, off * groups + g, i, 0))
    whole = lambda off: pl.BlockSpec((1, hb, s, d), lambda bi, g, i: (bi, off * groups + g, 0, 0))
    table = lambda arr: pl.BlockSpec((hb,) + arr.shape[1:], lambda bi, g, i: (g, 0, 0))
    out_shape = jax.ShapeDtypeStruct((b, heads, s, d), BF16)
    return pl.pallas_call(
        functools.partial(_mixer_kernel, heads_per_step=hb),
        grid=(b, groups, s // t),
        in_specs=[step(0), whole(1), whole(0), step(0),
                  step(0), step(1), step(1), step(1),
                  table(din), table(qd), table(kd), table(cd),
                  pl.BlockSpec((1, hb * d), lambda bi, g, i: (0, g))],
        out_specs=[step(0), step(0)],
        out_shape=[out_shape, out_shape],
        scratch_shapes=[pltpu.VMEM((hb, d, d), F32)],
        compiler_params=pltpu.CompilerParams(
            dimension_semantics=("parallel", "parallel", "arbitrary"),
            vmem_limit_bytes=VMEM_LIMIT),
        name="mixers",
    )(sb_qk, sb_qk, vals, gates, ret_qk, ret_qk, vals, gates, din, qd, kd, cd, gain)


def _ret_pieces(q_ref, k_ref, v_ref, g_ref, din_ref, qd_ref, kd_ref, cd_ref, gain_ref,
                o_ref, state_scr, heads):
    parts = q_ref.shape[2] // RET_CHUNK
    states = {}

    def piece(h, part):
        state = states[h] if part else state_scr[h]
        rows = slice(part * RET_CHUNK, (part + 1) * RET_CHUNK)
        qc = q_ref[0, h, rows, :]
        kc = k_ref[0, h, rows, :]
        vc = v_ref[0, h, rows, :]

        scores = lax.dot_general(qc, kc, (((1,), (1,)), ((), ())), preferred_element_type=F32)
        inner = jnp.dot((scores * din_ref[h]).astype(BF16), vc, preferred_element_type=F32)
        q_dec = (qc.astype(F32) * qd_ref[h]).astype(BF16)
        cross = jnp.dot(q_dec, state.astype(BF16), preferred_element_type=F32)
        k_dec = (kc.astype(F32) * kd_ref[h]).astype(BF16)
        kv = lax.dot_general(k_dec, vc, (((0,), (0,)), ((), ())), preferred_element_type=F32)
        states[h] = state * cd_ref[h] + kv
        if part == parts - 1:
            state_scr[h] = states[h]

        o = inner + cross
        mu = jnp.mean(o, axis=-1, keepdims=True)
        cen = o - mu
        var = jnp.mean(cen * cen, axis=-1, keepdims=True)
        y = cen * lax.rsqrt(var + EPS) * gain_ref[:, h * HEAD_DIM:(h + 1) * HEAD_DIM]
        o_ref[0, h, rows, :] = (y * g_ref[0, h, rows, :].astype(F32)).astype(BF16)

    return [functools.partial(piece, h, part) for part in range(parts) for h in range(heads)]


def _retention_tables(heads, c):
    log_gamma = np.log1p(-np.exp2(-5.0 - np.arange(heads, dtype=np.float64)))
    idx = np.arange(c, dtype=np.float64)
    rel = idx[:, None] - idx[None, :]
    din = np.where(rel[None] >= 0, np.exp(np.maximum(rel, 0.0)[None] * log_gamma[:, None, None]), 0.0)
    qd = np.exp((idx[None, :] + 1.0) * log_gamma[:, None])[..., None]
    kd = np.exp((c - 1.0 - idx[None, :]) * log_gamma[:, None])[..., None]
    cd = np.exp(c * log_gamma)[:, None, None]
    bc = lambda t: np.ascontiguousarray(
        np.broadcast_to(t, t.shape[:-1] + (HEAD_DIM,)), dtype=np.float32)
    return din.astype(np.float32), bc(qd), bc(kd), bc(cd)


def _out_kernel(a_ref, r_ref, w_ref, x_ref, mod_ref, o_ref, *, heads):
    parts = [a_ref[0, j] for j in range(heads)] + [r_ref[0, j] for j in range(heads)]
    mix = jnp.concatenate(parts, axis=-1)
    for c in range(w_ref.shape[1] // OUT_COLS):
        cols = slice(c * OUT_COLS, (c + 1) * OUT_COLS)
        y = jnp.dot(mix, w_ref[:, cols], preferred_element_type=F32)
        o_ref[0, :, cols] = x_ref[0, :, cols] + mod_ref[0, 2:3, cols] * y


def _out_proj(oa, ob, w_bf16, x, mod3):
    b, s, d = x.shape
    heads = oa.shape[1]
    tm = OUT_ROWS
    return pl.pallas_call(
        functools.partial(_out_kernel, heads=heads),
        grid=(b, s // tm),
        in_specs=[
            pl.BlockSpec((1, heads, tm, HEAD_DIM), lambda bi, si: (bi, 0, si, 0)),
            pl.BlockSpec((1, heads, tm, HEAD_DIM), lambda bi, si: (bi, 0, si, 0)),
            pl.BlockSpec(w_bf16.shape, lambda bi, si: (0, 0), pipeline_mode=pl.Buffered(1)),
            pl.BlockSpec((1, tm, d), lambda bi, si: (bi, si, 0)),
            pl.BlockSpec((1, 3, d), lambda bi, si: (bi, 0, 0)),
        ],
        out_specs=pl.BlockSpec((1, tm, d), lambda bi, si: (bi, si, 0)),
        out_shape=jax.ShapeDtypeStruct((b, s, d), F32),
        compiler_params=pltpu.CompilerParams(
            dimension_semantics=("parallel", "parallel"),
            vmem_limit_bytes=VMEM_LIMIT),
        name="out_proj",
    )(oa, ob, w_bf16, x, mod3)


def _rotary_tables(s):
    half = HEAD_DIM // 2
    inv_freq = ROPE_BASE ** (-np.arange(half, dtype=np.float64) / half)
    ang = np.arange(s, dtype=np.float64)[:, None] * inv_freq[None, :]
    cos, sin = np.cos(ang), np.sin(ang)
    return (np.concatenate([cos, cos], axis=-1).astype(np.float32),
            np.concatenate([-sin, sin], axis=-1).astype(np.float32))


def kernel(x, c, w_ada, b_ada, norm_gain, w_in, sb_q_gain, sb_k_gain, ret_norm_gain, w_out):
    b, s, d = x.shape
    depth = w_ada.shape[0]
    heads = w_in.shape[2] // (N_GROUPS * HEAD_DIM)
    cos_t, sin_t = _rotary_tables(s)
    din, qd, kd, cd = _retention_tables(heads, RET_CHUNK)
    c_pad = jnp.pad(c, ((0, ADALN_ROWS - b), (0, 0)))
    rope_spec = pl.BlockSpec((PROJ_ROWS, HEAD_DIM), lambda bi, si, g: (si, 0))
    for layer in range(depth):
        mod = _adaln(c_pad, w_ada[layer], b_ada[layer][None, :])
        mod3 = mod[:b].reshape(b, 3, d)
        w = w_in[layer]
        qk_gain = jnp.stack([sb_q_gain[layer], sb_k_gain[layer]])[:, None, :]
        h, sb_qk = _norm_qk(x, mod3, norm_gain[layer][None, :], w, qk_gain, heads,
                            q_mult=float(1.0 / np.sqrt(HEAD_DIM)))
        vals = _in_proj(h, w, "identity", 2, 4)
        gates = _in_proj(h, w, "silu", 3, 4)
        ret_qk = _in_proj(h, w, "rotary", 4, 1, (cos_t, sin_t), (rope_spec, rope_spec),
                          second_mult=float(HEAD_DIM ** -0.5))
        oa, ob = _mixers(sb_qk, ret_qk, vals, gates, heads, din, qd, kd, cd,
                         ret_norm_gain[layer][None, :])
        x = _out_proj(oa, ob, w_out[layer].astype(BF16), x, mod3)
    return x
```

```python
import functools

import jax
import jax.numpy as jnp
import numpy as np
from jax import lax
from jax.experimental import pallas as pl
from jax.experimental.pallas import tpu as pltpu

HEAD_DIM = 128
ROPE_BASE = 10000.0
EPS = 1e-6
N_GROUPS = 8
F32 = jnp.float32
BF16 = jnp.bfloat16

ADALN_ROWS = 16
ADALN_STEPS = 8
NORM_ROWS = 1024
NORM_UNIT_ROWS = 512
NORM_CHUNK = 32
PROJ_ROWS = 2048
PROJ_UNIT_ROWS = 1024
PROJ_COLS = 256
SB_Q = 128
SB_K = 256
SB_TAIL = 128
SB_QK_ROWS = 256
SB_STEP_ROWS = 2048
SB_HEADS_PER_STEP = 2
RET_CHUNK = 256
RET_PIECES_PER_GAP = 8
OUT_ROWS = 1024
OUT_COLS = 256
VMEM_LIMIT = 56 * 1024 * 1024
LOG_WEIGHT_FLOOR = -120.0
LOG2E = 1.4426950408889634


def _silu(v):
    half = 0.5 * v
    return half + half * jnp.tanh(half)


def _split_bf16(t):
    hi = t.astype(BF16)
    return hi, (t - hi.astype(F32)).astype(BF16)


def _adaln_kernel(c_ref, w_ref, b_ref, o_ref):
    s_hi, s_lo = _split_bf16(_silu(c_ref[...]))
    w_hi, w_lo = _split_bf16(w_ref[...])
    dot = functools.partial(jnp.dot, preferred_element_type=F32)
    o_ref[...] = dot(s_hi, w_hi) + (dot(s_hi, w_lo) + dot(s_lo, w_hi)) + b_ref[...]


def _adaln(c_pad, w, b):
    rows, d = c_pad.shape
    n = w.shape[1]
    tn = n // ADALN_STEPS
    return pl.pallas_call(
        _adaln_kernel,
        grid=(n // tn,),
        in_specs=[pl.BlockSpec((rows, d), lambda j: (0, 0)),
                  pl.BlockSpec((d, tn), lambda j: (0, j)),
                  pl.BlockSpec((1, tn), lambda j: (0, j))],
        out_specs=pl.BlockSpec((rows, tn), lambda j: (0, j)),
        out_shape=jax.ShapeDtypeStruct((rows, n), F32),
        compiler_params=pltpu.CompilerParams(
            dimension_semantics=("arbitrary",), vmem_limit_bytes=VMEM_LIMIT),
        name="adaln",
    )(c_pad, w, b)


def _norm_qk_kernel(x_ref, mod_ref, ng_ref, w_ref, gain_ref, h_ref, o_ref, *, heads, q_mult):
    shift = mod_ref[0, 0:1, :]
    gain_h = ng_ref[...] * (1.0 + mod_ref[0, 1:2, :])
    heads_per_chunk = PROJ_COLS // HEAD_DIM
    for r in range(x_ref.shape[1] // NORM_UNIT_ROWS):
        for ch in range(NORM_UNIT_ROWS // NORM_CHUNK):
            rows = slice(r * NORM_UNIT_ROWS + ch * NORM_CHUNK,
                         r * NORM_UNIT_ROWS + (ch + 1) * NORM_CHUNK)
            x = x_ref[0, rows, :]
            ms = jnp.mean(x * x, axis=-1, keepdims=True)
            h_ref[0, rows, :] = (x * lax.rsqrt(ms + EPS) * gain_h + shift).astype(BF16)
        rows = slice(r * NORM_UNIT_ROWS, (r + 1) * NORM_UNIT_ROWS)
        h = h_ref[0, rows, :]
        for c in range(2 * heads // heads_per_chunk):
            w_chunk = w_ref[:, c * PROJ_COLS:(c + 1) * PROJ_COLS].astype(BF16)
            acc = jnp.dot(h, w_chunk, preferred_element_type=F32)
            for j in range(heads_per_chunk):
                head = c * heads_per_chunk + j
                gain = gain_ref[head // heads] * (q_mult if head < heads else 1.0)
                a = acc[:, j * HEAD_DIM:(j + 1) * HEAD_DIM]
                ms = jnp.mean(a * a, axis=-1, keepdims=True)
                o_ref[0, head, rows, :] = (a * lax.rsqrt(ms + EPS) * gain).astype(BF16)


def _norm_qk(x, mod3, norm_gain, w, qk_gain, heads, q_mult):
    b, s, d = x.shape
    tm = NORM_ROWS
    cols = 2 * heads * HEAD_DIM
    return pl.pallas_call(
        functools.partial(_norm_qk_kernel, heads=heads, q_mult=q_mult),
        grid=(b, s // tm),
        in_specs=[pl.BlockSpec((1, tm, d), lambda bi, si: (bi, si, 0)),
                  pl.BlockSpec((1, 3, d), lambda bi, si: (bi, 0, 0)),
                  pl.BlockSpec((1, d), lambda bi, si: (0, 0)),
                  pl.BlockSpec((d, cols), lambda bi, si: (0, 0), pipeline_mode=pl.Buffered(1)),
                  pl.BlockSpec(qk_gain.shape, lambda bi, si: (0, 0, 0))],
        out_specs=[pl.BlockSpec((1, tm, d), lambda bi, si: (bi, si, 0)),
                   pl.BlockSpec((1, 2 * heads, tm, HEAD_DIM), lambda bi, si: (bi, 0, si, 0))],
        out_shape=[jax.ShapeDtypeStruct((b, s, d), BF16),
                   jax.ShapeDtypeStruct((b, 2 * heads, s, HEAD_DIM), BF16)],
        compiler_params=pltpu.CompilerParams(
            dimension_semantics=("parallel", "parallel"), vmem_limit_bytes=VMEM_LIMIT),
        name="norm_qk",
    )(x, mod3, norm_gain, w, qk_gain)


def _proj_kernel(h_ref, w_ref, *refs, kind, heads, first_mult, second_mult):
    o_ref = refs[-1]
    g = pl.program_id(2)
    mult = jnp.where(g == 0, first_mult, second_mult).astype(F32)
    if kind == "rotary":
        def epilogue(a, rows):
            return (a * (refs[0][rows, :] * mult)
                    + pltpu.roll(a, HEAD_DIM // 2, 1) * (refs[1][rows, :] * mult))
    elif kind == "silu":
        epilogue = lambda a, rows: _silu(a)
    else:
        epilogue = lambda a, rows: a

    heads_per_chunk = PROJ_COLS // HEAD_DIM
    for c in range(heads // heads_per_chunk):
        w_chunk = w_ref[:, c * PROJ_COLS:(c + 1) * PROJ_COLS].astype(BF16)
        for r in range(h_ref.shape[1] // PROJ_UNIT_ROWS):
            rows = slice(r * PROJ_UNIT_ROWS, (r + 1) * PROJ_UNIT_ROWS)
            acc = jnp.dot(h_ref[0, rows, :], w_chunk, preferred_element_type=F32)
            for j in range(heads_per_chunk):
                a = acc[:, j * HEAD_DIM:(j + 1) * HEAD_DIM]
                o_ref[0, c * heads_per_chunk + j, rows, :] = epilogue(a, rows).astype(BF16)


def _in_proj(h, w, kind, first_group, group_stride, extra=(), extra_specs=(),
             first_mult=1.0, second_mult=1.0):
    b, s, d = h.shape
    tn = w.shape[1] // N_GROUPS
    heads = tn // HEAD_DIM
    tm = PROJ_ROWS
    return pl.pallas_call(
        functools.partial(_proj_kernel, kind=kind, heads=heads,
                          first_mult=first_mult, second_mult=second_mult),
        grid=(b, s // tm, 2),
        in_specs=[
            pl.BlockSpec((1, tm, d), lambda bi, si, g: (bi, si, 0)),
            pl.BlockSpec((d, tn), lambda bi, si, g: (0, first_group + group_stride * g)),
            *extra_specs,
        ],
        out_specs=pl.BlockSpec((1, heads, tm, HEAD_DIM), lambda bi, si, g: (bi, g, si, 0)),
        out_shape=jax.ShapeDtypeStruct((b, 2 * heads, s, HEAD_DIM), BF16),
        compiler_params=pltpu.CompilerParams(
            dimension_semantics=("parallel", "parallel", "arbitrary"),
            vmem_limit_bytes=VMEM_LIMIT),
        name="in_proj_" + kind,
    )(h, w, *extra)


def _sb_scores(q, k_blk):
    z = lax.dot_general(q, k_blk, (((1,), (1,)), ((), ())), preferred_element_type=F32)
    sp = jnp.maximum(z, 0.0) + jnp.log(1.0 + jnp.exp2(jnp.abs(z) * (-LOG2E)))
    return sp, z - sp


def _sb_later(sp, neg_upper):
    return jnp.dot(sp.astype(BF16), neg_upper, preferred_element_type=F32)


def _sb_first_windows(q_ref, k_ref, v_ref, neg_upper, heads_per_step):
    win = 2 * SB_K
    z = jnp.concatenate(
        [lax.dot_general(q_ref[0, hh, :win, :], k_ref[0, hh, :win, :], (((1,), (1,)), ((), ())),
                         preferred_element_type=F32) for hh in range(heads_per_step)],
        axis=0).astype(BF16)
    n = z.shape[0]
    l = jnp.log(1.0 + jnp.exp(-jnp.abs(z)))
    logb = jnp.minimum(z, 0.0) - l
    query = lax.broadcasted_iota(jnp.int32, (n, win), 0) & (win - 1)
    key = lax.broadcasted_iota(jnp.int32, (n, win), 1)
    past = jnp.where(key < query, 1.0, 0.0).astype(BF16)
    sp = (jnp.maximum(z, 0.0) + l) * past
    sp_p, sp_d = sp[:, :SB_K], sp[:, SB_K:]
    w_d = jnp.exp(logb[:, SB_K:] + _sb_later(sp_d, neg_upper).astype(BF16)) * past[:, SB_K:]
    carry = -jnp.sum(sp_d.astype(F32), axis=1, keepdims=True)
    w_p = jnp.exp(logb[:, :SB_K] + (_sb_later(sp_p, neg_upper) + carry).astype(BF16)) * past[:, :SB_K]
    carry = carry - jnp.sum(sp_p.astype(F32), axis=1, keepdims=True)
    w = jnp.concatenate([w_p, w_d], axis=1)
    out = {}
    for hh in range(heads_per_step):
        for sub in range(win // SB_Q):
            rows = slice(hh * win + sub * SB_Q, hh * win + (sub + 1) * SB_Q)
            acc = jnp.dot(w[rows], v_ref[0, hh, :win, :], preferred_element_type=F32)
            out[(hh, sub)] = (carry[rows], acc, 0)
    return out


def _sb_interior_windows(q_ref, k_ref, v_ref, step_start, neg_upper, heads_per_step, first_block,
                         filler):
    win = 2 * SB_K
    per_mm = SB_QK_ROWS // SB_Q
    span = win + (per_mm - 1) * SB_Q
    w_start = step_start - (win - SB_Q)
    zs, keys = [], []
    for hh in range(heads_per_step):
        for m in range(first_block, SB_STEP_ROWS // SB_QK_ROWS):
            k_start = pl.multiple_of(w_start + m * SB_QK_ROWS, SB_Q)
            z = lax.dot_general(q_ref[0, hh, m * SB_QK_ROWS:(m + 1) * SB_QK_ROWS, :],
                                k_ref[0, hh, pl.ds(k_start, span), :],
                                (((1,), (1,)), ((), ())), preferred_element_type=F32)
            zs += [z[u * SB_Q:(u + 1) * SB_Q, u * SB_Q:u * SB_Q + win] for u in range(per_mm)]
            keys += [(hh, m * per_mm + u) for u in range(per_mm)]
    filler()
    z = jnp.concatenate(zs, axis=0).astype(BF16)
    n = z.shape[0]
    l = jnp.log(1.0 + jnp.exp(-jnp.abs(z)))
    sp = jnp.maximum(z, 0.0) + l
    logb = jnp.minimum(z, 0.0) - l
    filler()
    row = lax.broadcasted_iota(jnp.int32, (n, SB_Q), 0) & (SB_Q - 1)
    col = lax.broadcasted_iota(jnp.int32, (n, SB_Q), 1)
    past = jnp.where(col < row, 1.0, 0.0).astype(BF16)
    diag = lambda t: jnp.concatenate([t[:, :SB_K - SB_Q], t[:, SB_K - SB_Q:] * past], axis=1)
    sp_p, sp_d = sp[:, :SB_K], diag(sp[:, SB_K:])
    w_d = diag(jnp.exp(logb[:, SB_K:] + _sb_later(sp_d, neg_upper).astype(BF16)))
    filler()
    carry = -jnp.sum(sp_d.astype(F32), axis=1, keepdims=True)
    w_p = jnp.exp(logb[:, :SB_K] + (_sb_later(sp_p, neg_upper) + carry).astype(BF16))
    carry = carry - jnp.sum(sp_p.astype(F32), axis=1, keepdims=True)
    filler()
    w = jnp.concatenate([w_p, w_d], axis=1)
    out = {}
    for c, (hh, sub) in enumerate(keys):
        p_start = pl.multiple_of(w_start + sub * SB_Q, SB_Q)
        acc = jnp.dot(w[c * SB_Q:(c + 1) * SB_Q], v_ref[0, hh, pl.ds(p_start, win), :],
                      preferred_element_type=F32)
        out[(hh, sub)] = (carry[c * SB_Q:(c + 1) * SB_Q], acc, p_start)
    return out


def _mixer_kernel(q_ref, k_ref, v_ref, g_ref, rq_ref, rk_ref, rv_ref, rg_ref,
                  din_ref, qd_ref, kd_ref, cd_ref, gain_ref, o_ref, ro_ref, state_scr,
                  *, heads_per_step):
    i = pl.program_id(2)

    @pl.when(i == 0)
    def _():
        state_scr[...] = jnp.zeros_like(state_scr)

    urow = lax.broadcasted_iota(jnp.int32, (SB_K, SB_K), 0)
    ucol = lax.broadcasted_iota(jnp.int32, (SB_K, SB_K), 1)
    neg_upper = jnp.where(urow > ucol, -1.0, 0.0).astype(BF16)
    neg_upper_tail = neg_upper[:SB_TAIL, :SB_TAIL]
    subs = SB_STEP_ROWS // SB_Q

    def run(edge):
        step_start = i * SB_STEP_ROWS
        edge_blocks = -(-(2 * SB_K - SB_Q) // SB_QK_ROWS) if edge else 0
        pieces = _ret_pieces(rq_ref, rk_ref, rv_ref, rg_ref, din_ref, qd_ref, kd_ref, cd_ref,
                             gain_ref, ro_ref, state_scr, heads_per_step)

        def filler():
            for piece in [pieces.pop(0) for _ in range(min(RET_PIECES_PER_GAP, len(pieces)))]:
                piece()

        stacked = _sb_interior_windows(q_ref, k_ref, v_ref, step_start, neg_upper,
                                       heads_per_step, edge_blocks, filler)
        if edge:
            assert first_interior == 1 and edge_blocks * SB_QK_ROWS == 2 * SB_K
            stacked.update(_sb_first_windows(q_ref, k_ref, v_ref, neg_upper, heads_per_step))
        chains = []
        for hh in range(heads_per_step):
            for sub in range(subs):
                rows = slice(sub * SB_Q, (sub + 1) * SB_Q)
                q = q_ref[0, hh, rows, :]
                carry, acc, p_start = stacked[(hh, sub)]
                o_ref[0, hh, rows, :] = (acc * g_ref[0, hh, rows, :].astype(F32)).astype(BF16)
                live = (jnp.max(carry) > LOG_WEIGHT_FLOOR) & (p_start > 0)
                chains.append((hh, rows, q, carry, acc, p_start, live))

        while pieces:
            filler()

        def older_tiles(hh, rows, q, carry, acc, p_start, live):
            tiles_left = p_start // SB_TAIL

            def cond(state):
                jj, live, _, _ = state
                return (jj < tiles_left) & live

            def body(state):
                jj, _, carry, acc = state
                start = pl.multiple_of(p_start - (jj + 1) * SB_TAIL, SB_TAIL)
                sp, logb = _sb_scores(q, k_ref[0, hh, pl.ds(start, SB_TAIL), :])
                w = jnp.exp(logb + _sb_later(sp, neg_upper_tail) + carry)
                acc = acc + jnp.dot(w.astype(BF16), v_ref[0, hh, pl.ds(start, SB_TAIL), :],
                                    preferred_element_type=F32)
                carry = carry - jnp.sum(sp, axis=1, keepdims=True)
                return jj + 1, jnp.max(carry) > LOG_WEIGHT_FLOOR, carry, acc

            _, _, _, acc = lax.while_loop(cond, body, (jnp.int32(0), live, carry, acc))
            o_ref[0, hh, rows, :] = (acc * g_ref[0, hh, rows, :].astype(F32)).astype(BF16)

        any_live = functools.reduce(jnp.logical_or, [ch[-1] for ch in chains])

        @pl.when(any_live)
        def _():
            for chain in chains:
                pl.when(chain[-1])(functools.partial(older_tiles, *chain))

    first_interior = (SB_Q + SB_K + SB_STEP_ROWS - 1) // SB_STEP_ROWS
    pl.when(i >= first_interior)(lambda: run(False))
    pl.when(i < first_interior)(lambda: run(True))


def _mixers(sb_qk, ret_qk, vals, gates, heads, din, qd, kd, cd, gain):
    b, _, s, d = sb_qk.shape
    hb = SB_HEADS_PER_STEP
    groups = heads // hb
    t = SB_STEP_ROWS
    step = lambda off: pl.BlockSpec((1, hb, t, d), lambda bi, g, i: (bi, off * groups + g, i, 0))
    whole = lambda off: pl.BlockSpec((1, hb, s, d), lambda bi, g, i: (bi, off * groups + g, 0, 0))
    table = lambda arr: pl.BlockSpec((hb,) + arr.shape[1:], lambda bi, g, i: (g, 0, 0))
    out_shape = jax.ShapeDtypeStruct((b, heads, s, d), BF16)
    return pl.pallas_call(
        functools.partial(_mixer_kernel, heads_per_step=hb),
        grid=(b, groups, s // t),
        in_specs=[step(0), whole(1), whole(0), step(0),
                  step(0), step(1), step(1), step(1),
                  table(din), table(qd), table(kd), table(cd),
                  pl.BlockSpec((1, hb * d), lambda bi, g, i: (0, g))],
        out_specs=[step(0), step(0)],
        out_shape=[out_shape, out_shape],
        scratch_shapes=[pltpu.VMEM((hb, d, d), F32)],
        compiler_params=pltpu.CompilerParams(
            dimension_semantics=("parallel", "parallel", "arbitrary"),
            vmem_limit_bytes=VMEM_LIMIT),
        name="mixers",
    )(sb_qk, sb_qk, vals, gates, ret_qk, ret_qk, vals, gates, din, qd, kd, cd, gain)


def _ret_pieces(q_ref, k_ref, v_ref, g_ref, din_ref, qd_ref, kd_ref, cd_ref, gain_ref,
                o_ref, state_scr, heads):
    parts = q_ref.shape[2] // RET_CHUNK
    states = {}

    def piece(h, part):
        state = states[h] if part else state_scr[h]
        rows = slice(part * RET_CHUNK, (part + 1) * RET_CHUNK)
        qc = q_ref[0, h, rows, :]
        kc = k_ref[0, h, rows, :]
        vc = v_ref[0, h, rows, :]

        scores = lax.dot_general(qc, kc, (((1,), (1,)), ((), ())), preferred_element_type=F32)
        inner = jnp.dot((scores * din_ref[h]).astype(BF16), vc, preferred_element_type=F32)
        q_dec = (qc.astype(F32) * qd_ref[h]).astype(BF16)
        cross = jnp.dot(q_dec, state.astype(BF16), preferred_element_type=F32)
        k_dec = (kc.astype(F32) * kd_ref[h]).astype(BF16)
        kv = lax.dot_general(k_dec, vc, (((0,), (0,)), ((), ())), preferred_element_type=F32)
        states[h] = state * cd_ref[h] + kv
        if part == parts - 1:
            state_scr[h] = states[h]

        o = inner + cross
        mu = jnp.mean(o, axis=-1, keepdims=True)
        cen = o - mu
        var = jnp.mean(cen * cen, axis=-1, keepdims=True)
        y = cen * lax.rsqrt(var + EPS) * gain_ref[:, h * HEAD_DIM:(h + 1) * HEAD_DIM]
        o_ref[0, h, rows, :] = (y * g_ref[0, h, rows, :].astype(F32)).astype(BF16)

    return [functools.partial(piece, h, part) for part in range(parts) for h in range(heads)]


def _retention_tables(heads, c):
    log_gamma = np.log1p(-np.exp2(-5.0 - np.arange(heads, dtype=np.float64)))
    idx = np.arange(c, dtype=np.float64)
    rel = idx[:, None] - idx[None, :]
    din = np.where(rel[None] >= 0, np.exp(np.maximum(rel, 0.0)[None] * log_gamma[:, None, None]), 0.0)
    qd = np.exp((idx[None, :] + 1.0) * log_gamma[:, None])[..., None]
    kd = np.exp((c - 1.0 - idx[None, :]) * log_gamma[:, None])[..., None]
    cd = np.exp(c * log_gamma)[:, None, None]
    bc = lambda t: np.ascontiguousarray(
        np.broadcast_to(t, t.shape[:-1] + (HEAD_DIM,)), dtype=np.float32)
    return din.astype(np.float32), bc(qd), bc(kd), bc(cd)


def _out_kernel(a_ref, r_ref, w_ref, x_ref, mod_ref, o_ref, *, heads):
    parts = [a_ref[0, j] for j in range(heads)] + [r_ref[0, j] for j in range(heads)]
    mix = jnp.concatenate(parts, axis=-1)
    for c in range(w_ref.shape[1] // OUT_COLS):
        cols = slice(c * OUT_COLS, (c + 1) * OUT_COLS)
        y = jnp.dot(mix, w_ref[:, cols], preferred_element_type=F32)
        o_ref[0, :, cols] = x_ref[0, :, cols] + mod_ref[0, 2:3, cols] * y


def _out_proj(oa, ob, w_bf16, x, mod3):
    b, s, d = x.shape
    heads = oa.shape[1]
    tm = OUT_ROWS
    return pl.pallas_call(
        functools.partial(_out_kernel, heads=heads),
        grid=(b, s // tm),
        in_specs=[
            pl.BlockSpec((1, heads, tm, HEAD_DIM), lambda bi, si: (bi, 0, si, 0)),
            pl.BlockSpec((1, heads, tm, HEAD_DIM), lambda bi, si: (bi, 0, si, 0)),
            pl.BlockSpec(w_bf16.shape, lambda bi, si: (0, 0), pipeline_mode=pl.Buffered(1)),
            pl.BlockSpec((1, tm, d), lambda bi, si: (bi, si, 0)),
            pl.BlockSpec((1, 3, d), lambda bi, si: (bi, 0, 0)),
        ],
        out_specs=pl.BlockSpec((1, tm, d), lambda bi, si: (bi, si, 0)),
        out_shape=jax.ShapeDtypeStruct((b, s, d), F32),
        compiler_params=pltpu.CompilerParams(
            dimension_semantics=("parallel", "parallel"),
            vmem_limit_bytes=VMEM_LIMIT),
        name="out_proj",
    )(oa, ob, w_bf16, x, mod3)


def _rotary_tables(s):
    half = HEAD_DIM // 2
    inv_freq = ROPE_BASE ** (-np.arange(half, dtype=np.float64) / half)
    ang = np.arange(s, dtype=np.float64)[:, None] * inv_freq[None, :]
    cos, sin = np.cos(ang), np.sin(ang)
    return (np.concatenate([cos, cos], axis=-1).astype(np.float32),
            np.concatenate([-sin, sin], axis=-1).astype(np.float32))


def kernel(x, c, w_ada, b_ada, norm_gain, w_in, sb_q_gain, sb_k_gain, ret_norm_gain, w_out):
    b, s, d = x.shape
    depth = w_ada.shape[0]
    heads = w_in.shape[2] // (N_GROUPS * HEAD_DIM)
    cos_t, sin_t = _rotary_tables(s)
    din, qd, kd, cd = _retention_tables(heads, RET_CHUNK)
    c_pad = jnp.pad(c, ((0, ADALN_ROWS - b), (0, 0)))
    rope_spec = pl.BlockSpec((PROJ_ROWS, HEAD_DIM), lambda bi, si, g: (si, 0))
    for layer in range(depth):
        mod = _adaln(c_pad, w_ada[layer], b_ada[layer][None, :])
        mod3 = mod[:b].reshape(b, 3, d)
        w = w_in[layer]
        qk_gain = jnp.stack([sb_q_gain[layer], sb_k_gain[layer]])[:, None, :]
        h, sb_qk = _norm_qk(x, mod3, norm_gain[layer][None, :], w, qk_gain, heads,
                            q_mult=float(1.0 / np.sqrt(HEAD_DIM)))
        vals = _in_proj(h, w, "identity", 2, 4)
        gates = _in_proj(h, w, "silu", 3, 4)
        ret_qk = _in_proj(h, w, "rotary", 4, 1, (cos_t, sin_t), (rope_spec, rope_spec),
                          second_mult=float(HEAD_DIM ** -0.5))
        oa, ob = _mixers(sb_qk, ret_qk, vals, gates, heads, din, qd, kd, cd,
                         ret_norm_gain[layer][None, :])
        x = _out_proj(oa, ob, w_out[layer].astype(BF16), x, mod3)
    return x
```

```python
import functools

import jax
import jax.numpy as jnp
import numpy as np
from jax import lax
from jax.experimental import pallas as pl
from jax.experimental.pallas import tpu as pltpu

HEAD_DIM = 128
ROPE_BASE = 10000.0
EPS = 1e-6
N_GROUPS = 8
F32 = jnp.float32
BF16 = jnp.bfloat16

ADALN_ROWS = 16
ADALN_STEPS = 8
NORM_ROWS = 1024
NORM_UNIT_ROWS = 512
NORM_CHUNK = 32
PROJ_ROWS = 2048
PROJ_UNIT_ROWS = 1024
PROJ_COLS = 256
SB_Q = 128
SB_K = 256
SB_TAIL = 128
SB_QK_ROWS = 256
SB_STEP_ROWS = 2048
SB_HEADS_PER_STEP = 2
RET_CHUNK = 256
RET_PIECES_PER_GAP = 8
OUT_ROWS = 1024
OUT_COLS = 256
VMEM_LIMIT = 56 * 1024 * 1024
LOG_WEIGHT_FLOOR = -106.0
LOG2E = 1.4426950408889634


def _silu(v):
    half = 0.5 * v
    return half + half * jnp.tanh(half)


def _split_bf16(t):
    hi = t.astype(BF16)
    return hi, (t - hi.astype(F32)).astype(BF16)


def _adaln_kernel(c_ref, w_ref, b_ref, o_ref):
    s_hi, s_lo = _split_bf16(_silu(c_ref[...]))
    w_hi, w_lo = _split_bf16(w_ref[...])
    dot = functools.partial(jnp.dot, preferred_element_type=F32)
    o_ref[...] = dot(s_hi, w_hi) + (dot(s_hi, w_lo) + dot(s_lo, w_hi)) + b_ref[...]


def _adaln(c_pad, w, b):
    rows, d = c_pad.shape
    n = w.shape[1]
    tn = n // ADALN_STEPS
    return pl.pallas_call(
        _adaln_kernel,
        grid=(n // tn,),
        in_specs=[pl.BlockSpec((rows, d), lambda j: (0, 0)),
                  pl.BlockSpec((d, tn), lambda j: (0, j)),
                  pl.BlockSpec((1, tn), lambda j: (0, j))],
        out_specs=pl.BlockSpec((rows, tn), lambda j: (0, j)),
        out_shape=jax.ShapeDtypeStruct((rows, n), F32),
        compiler_params=pltpu.CompilerParams(
            dimension_semantics=("arbitrary",), vmem_limit_bytes=VMEM_LIMIT),
        name="adaln",
    )(c_pad, w, b)


def _norm_qk_kernel(x_ref, mod_ref, ng_ref, w_ref, gain_ref, h_ref, o_ref, *, heads, q_mult):
    shift = mod_ref[0, 0:1, :]
    gain_h = ng_ref[...] * (1.0 + mod_ref[0, 1:2, :])
    heads_per_chunk = PROJ_COLS // HEAD_DIM
    for r in range(x_ref.shape[1] // NORM_UNIT_ROWS):
        for ch in range(NORM_UNIT_ROWS // NORM_CHUNK):
            rows = slice(r * NORM_UNIT_ROWS + ch * NORM_CHUNK,
                         r * NORM_UNIT_ROWS + (ch + 1) * NORM_CHUNK)
            x = x_ref[0, rows, :]
            ms = jnp.mean(x * x, axis=-1, keepdims=True)
            h_ref[0, rows, :] = (x * lax.rsqrt(ms + EPS) * gain_h + shift).astype(BF16)
        rows = slice(r * NORM_UNIT_ROWS, (r + 1) * NORM_UNIT_ROWS)
        h = h_ref[0, rows, :]
        for c in range(2 * heads // heads_per_chunk):
            w_chunk = w_ref[:, c * PROJ_COLS:(c + 1) * PROJ_COLS].astype(BF16)
            acc = jnp.dot(h, w_chunk, preferred_element_type=F32)
            for j in range(heads_per_chunk):
                head = c * heads_per_chunk + j
                gain = gain_ref[head // heads] * (q_mult if head < heads else 1.0)
                a = acc[:, j * HEAD_DIM:(j + 1) * HEAD_DIM]
                ms = jnp.mean(a * a, axis=-1, keepdims=True)
                o_ref[0, head, rows, :] = (a * lax.rsqrt(ms + EPS) * gain).astype(BF16)


def _norm_qk(x, mod3, norm_gain, w, qk_gain, heads, q_mult):
    b, s, d = x.shape
    tm = NORM_ROWS
    cols = 2 * heads * HEAD_DIM
    return pl.pallas_call(
        functools.partial(_norm_qk_kernel, heads=heads, q_mult=q_mult),
        grid=(b, s // tm),
        in_specs=[pl.BlockSpec((1, tm, d), lambda bi, si: (bi, si, 0)),
                  pl.BlockSpec((1, 3, d), lambda bi, si: (bi, 0, 0)),
                  pl.BlockSpec((1, d), lambda bi, si: (0, 0)),
                  pl.BlockSpec((d, cols), lambda bi, si: (0, 0), pipeline_mode=pl.Buffered(1)),
                  pl.BlockSpec(qk_gain.shape, lambda bi, si: (0, 0, 0))],
        out_specs=[pl.BlockSpec((1, tm, d), lambda bi, si: (bi, si, 0)),
                   pl.BlockSpec((1, 2 * heads, tm, HEAD_DIM), lambda bi, si: (bi, 0, si, 0))],
        out_shape=[jax.ShapeDtypeStruct((b, s, d), BF16),
                   jax.ShapeDtypeStruct((b, 2 * heads, s, HEAD_DIM), BF16)],
        compiler_params=pltpu.CompilerParams(
            dimension_semantics=("parallel", "parallel"), vmem_limit_bytes=VMEM_LIMIT),
        name="norm_qk",
    )(x, mod3, norm_gain, w, qk_gain)


def _proj_kernel(h_ref, w_ref, *refs, kind, heads, first_mult, second_mult):
    o_ref = refs[-1]
    g = pl.program_id(2)
    mult = jnp.where(g == 0, first_mult, second_mult).astype(F32)
    if kind == "rotary":
        def epilogue(a, rows):
            return (a * (refs[0][rows, :] * mult)
                    + pltpu.roll(a, HEAD_DIM // 2, 1) * (refs[1][rows, :] * mult))
    elif kind == "silu":
        epilogue = lambda a, rows: _silu(a)
    else:
        epilogue = lambda a, rows: a

    heads_per_chunk = PROJ_COLS // HEAD_DIM
    for c in range(heads // heads_per_chunk):
        w_chunk = w_ref[:, c * PROJ_COLS:(c + 1) * PROJ_COLS].astype(BF16)
        for r in range(h_ref.shape[1] // PROJ_UNIT_ROWS):
            rows = slice(r * PROJ_UNIT_ROWS, (r + 1) * PROJ_UNIT_ROWS)
            acc = jnp.dot(h_ref[0, rows, :], w_chunk, preferred_element_type=F32)
            for j in range(heads_per_chunk):
                a = acc[:, j * HEAD_DIM:(j + 1) * HEAD_DIM]
                o_ref[0, c * heads_per_chunk + j, rows, :] = epilogue(a, rows).astype(BF16)


def _in_proj(h, w, kind, first_group, group_stride, extra=(), extra_specs=(),
             first_mult=1.0, second_mult=1.0):
    b, s, d = h.shape
    tn = w.shape[1] // N_GROUPS
    heads = tn // HEAD_DIM
    tm = PROJ_ROWS
    return pl.pallas_call(
        functools.partial(_proj_kernel, kind=kind, heads=heads,
                          first_mult=first_mult, second_mult=second_mult),
        grid=(b, s // tm, 2),
        in_specs=[
            pl.BlockSpec((1, tm, d), lambda bi, si, g: (bi, si, 0)),
            pl.BlockSpec((d, tn), lambda bi, si, g: (0, first_group + group_stride * g)),
            *extra_specs,
        ],
        out_specs=pl.BlockSpec((1, heads, tm, HEAD_DIM), lambda bi, si, g: (bi, g, si, 0)),
        out_shape=jax.ShapeDtypeStruct((b, 2 * heads, s, HEAD_DIM), BF16),
        compiler_params=pltpu.CompilerParams(
            dimension_semantics=("parallel", "parallel", "arbitrary"),
            vmem_limit_bytes=VMEM_LIMIT),
        name="in_proj_" + kind,
    )(h, w, *extra)


def _sb_scores(q, k_blk):
    z = lax.dot_general(q, k_blk, (((1,), (1,)), ((), ())), preferred_element_type=F32)
    sp = jnp.maximum(z, 0.0) + jnp.log(1.0 + jnp.exp2(jnp.abs(z) * (-LOG2E)))
    return sp, z - sp


def _sb_later(sp, neg_upper):
    return jnp.dot(sp.astype(BF16), neg_upper, preferred_element_type=F32)


def _sb_first_windows(q_ref, k_ref, v_ref, neg_upper, heads_per_step):
    win = 2 * SB_K
    z = jnp.concatenate(
        [lax.dot_general(q_ref[0, hh, :win, :], k_ref[0, hh, :win, :], (((1,), (1,)), ((), ())),
                         preferred_element_type=F32) for hh in range(heads_per_step)],
        axis=0).astype(BF16)
    n = z.shape[0]
    l = jnp.log(1.0 + jnp.exp(-jnp.abs(z)))
    logb = jnp.minimum(z, 0.0) - l
    query = lax.broadcasted_iota(jnp.int32, (n, win), 0) & (win - 1)
    key = lax.broadcasted_iota(jnp.int32, (n, win), 1)
    past = jnp.where(key < query, 1.0, 0.0).astype(BF16)
    sp = (jnp.maximum(z, 0.0) + l) * past
    sp_p, sp_d = sp[:, :SB_K], sp[:, SB_K:]
    w_d = jnp.exp(logb[:, SB_K:] + _sb_later(sp_d, neg_upper).astype(BF16)) * past[:, SB_K:]
    carry = -jnp.sum(sp_d.astype(F32), axis=1, keepdims=True)
    w_p = jnp.exp(logb[:, :SB_K] + (_sb_later(sp_p, neg_upper) + carry).astype(BF16)) * past[:, :SB_K]
    carry = carry - jnp.sum(sp_p.astype(F32), axis=1, keepdims=True)
    w = jnp.concatenate([w_p, w_d], axis=1)
    out = {}
    for hh in range(heads_per_step):
        for sub in range(win // SB_Q):
            rows = slice(hh * win + sub * SB_Q, hh * win + (sub + 1) * SB_Q)
            acc = jnp.dot(w[rows], v_ref[0, hh, :win, :], preferred_element_type=F32)
            out[(hh, sub)] = (carry[rows], acc, 0)
    return out


def _sb_interior_windows(q_ref, k_ref, v_ref, step_start, neg_upper, heads_per_step, first_block,
                         filler):
    win = 2 * SB_K
    per_mm = SB_QK_ROWS // SB_Q
    span = win + (per_mm - 1) * SB_Q
    w_start = step_start - (win - SB_Q)
    zs, keys = [], []
    for hh in range(heads_per_step):
        for m in range(first_block, SB_STEP_ROWS // SB_QK_ROWS):
            k_start = pl.multiple_of(w_start + m * SB_QK_ROWS, SB_Q)
            z = lax.dot_general(q_ref[0, hh, m * SB_QK_ROWS:(m + 1) * SB_QK_ROWS, :],
                                k_ref[0, hh, pl.ds(k_start, span), :],
                                (((1,), (1,)), ((), ())), preferred_element_type=F32)
            zs += [z[u * SB_Q:(u + 1) * SB_Q, u * SB_Q:u * SB_Q + win] for u in range(per_mm)]
            keys += [(hh, m * per_mm + u) for u in range(per_mm)]
    filler()
    z = jnp.concatenate(zs, axis=0).astype(BF16)
    n = z.shape[0]
    l = jnp.log(1.0 + jnp.exp(-jnp.abs(z)))
    sp = jnp.maximum(z, 0.0) + l
    logb = jnp.minimum(z, 0.0) - l
    filler()
    row = lax.broadcasted_iota(jnp.int32, (n, SB_Q), 0) & (SB_Q - 1)
    col = lax.broadcasted_iota(jnp.int32, (n, SB_Q), 1)
    past = jnp.where(col < row, 1.0, 0.0).astype(BF16)
    diag = lambda t: jnp.concatenate([t[:, :SB_K - SB_Q], t[:, SB_K - SB_Q:] * past], axis=1)
    sp_p, sp_d = sp[:, :SB_K], diag(sp[:, SB_K:])
    w_d = diag(jnp.exp(logb[:, SB_K:] + _sb_later(sp_d, neg_upper).astype(BF16)))
    filler()
    carry = -jnp.sum(sp_d.astype(F32), axis=1, keepdims=True)
    w_p = jnp.exp(logb[:, :SB_K] + (_sb_later(sp_p, neg_upper) + carry).astype(BF16))
    carry = carry - jnp.sum(sp_p.astype(F32), axis=1, keepdims=True)
    filler()
    w = jnp.concatenate([w_p, w_d], axis=1)
    out = {}
    for c, (hh, sub) in enumerate(keys):
        p_start = pl.multiple_of(w_start + sub * SB_Q, SB_Q)
        acc = jnp.dot(w[c * SB_Q:(c + 1) * SB_Q], v_ref[0, hh, pl.ds(p_start, win), :],
                      preferred_element_type=F32)
        out[(hh, sub)] = (carry[c * SB_Q:(c + 1) * SB_Q], acc, p_start)
    return out


def _mixer_kernel(q_ref, k_ref, v_ref, g_ref, rq_ref, rk_ref, rv_ref, rg_ref,
                  din_ref, qd_ref, kd_ref, cd_ref, gain_ref, o_ref, ro_ref, state_scr,
                  *, heads_per_step):
    i = pl.program_id(2)

    @pl.when(i == 0)
    def _():
        state_scr[...] = jnp.zeros_like(state_scr)

    urow = lax.broadcasted_iota(jnp.int32, (SB_K, SB_K), 0)
    ucol = lax.broadcasted_iota(jnp.int32, (SB_K, SB_K), 1)
    neg_upper = jnp.where(urow > ucol, -1.0, 0.0).astype(BF16)
    neg_upper_tail = neg_upper[:SB_TAIL, :SB_TAIL]
    subs = SB_STEP_ROWS // SB_Q

    def run(edge):
        step_start = i * SB_STEP_ROWS
        edge_blocks = -(-(2 * SB_K - SB_Q) // SB_QK_ROWS) if edge else 0
        pieces = _ret_pieces(rq_ref, rk_ref, rv_ref, rg_ref, din_ref, qd_ref, kd_ref, cd_ref,
                             gain_ref, ro_ref, state_scr, heads_per_step)

        def filler():
            for piece in [pieces.pop(0) for _ in range(min(RET_PIECES_PER_GAP, len(pieces)))]:
                piece()

        stacked = _sb_interior_windows(q_ref, k_ref, v_ref, step_start, neg_upper,
                                       heads_per_step, edge_blocks, filler)
        if edge:
            assert first_interior == 1 and edge_blocks * SB_QK_ROWS == 2 * SB_K
            stacked.update(_sb_first_windows(q_ref, k_ref, v_ref, neg_upper, heads_per_step))
        chains = []
        for hh in range(heads_per_step):
            for sub in range(subs):
                rows = slice(sub * SB_Q, (sub + 1) * SB_Q)
                q = q_ref[0, hh, rows, :]
                carry, acc, p_start = stacked[(hh, sub)]
                o_ref[0, hh, rows, :] = (acc * g_ref[0, hh, rows, :].astype(F32)).astype(BF16)
                live = (jnp.max(carry) > LOG_WEIGHT_FLOOR) & (p_start > 0)
                chains.append((hh, rows, q, carry, acc, p_start, live))

        while pieces:
            filler()

        def older_tiles(hh, rows, q, carry, acc, p_start, live):
            tiles_left = p_start // SB_TAIL

            def cond(state):
                jj, live, _, _ = state
                return (jj < tiles_left) & live

            def body(state):
                jj, _, carry, acc = state
                start = pl.multiple_of(p_start - (jj + 1) * SB_TAIL, SB_TAIL)
                sp, logb = _sb_scores(q, k_ref[0, hh, pl.ds(start, SB_TAIL), :])
                w = jnp.exp(logb + _sb_later(sp, neg_upper_tail) + carry)
                acc = acc + jnp.dot(w.astype(BF16), v_ref[0, hh, pl.ds(start, SB_TAIL), :],
                                    preferred_element_type=F32)
                carry = carry - jnp.sum(sp, axis=1, keepdims=True)
                return jj + 1, jnp.max(carry) > LOG_WEIGHT_FLOOR, carry, acc

            _, _, _, acc = lax.while_loop(cond, body, (jnp.int32(0), live, carry, acc))
            o_ref[0, hh, rows, :] = (acc * g_ref[0, hh, rows, :].astype(F32)).astype(BF16)

        any_live = functools.reduce(jnp.logical_or, [ch[-1] for ch in chains])

        @pl.when(any_live)
        def _():
            for chain in chains:
                pl.when(chain[-1])(functools.partial(older_tiles, *chain))

    first_interior = (SB_Q + SB_K + SB_STEP_ROWS - 1) // SB_STEP_ROWS
    pl.when(i >= first_interior)(lambda: run(False))
    pl.when(i < first_interior)(lambda: run(True))


def _mixers(sb_qk, ret_qk, vals, gates, heads, din, qd, kd, cd, gain):
    b, _, s, d = sb_qk.shape
    hb = SB_HEADS_PER_STEP
    groups = heads // hb
    t = SB_STEP_ROWS
    step = lambda off: pl.BlockSpec((1, hb, t, d), lambda bi, g, i: (bi, off * groups + g, i, 0))
    whole = lambda off: pl.BlockSpec((1, hb, s, d), lambda bi, g, i: (bi, off * groups + g, 0, 0))
    table = lambda arr: pl.BlockSpec((hb,) + arr.shape[1:], lambda bi, g, i: (g, 0, 0))
    out_shape = jax.ShapeDtypeStruct((b, heads, s, d), BF16)
    return pl.pallas_call(
        functools.partial(_mixer_kernel, heads_per_step=hb),
        grid=(b, groups, s // t),
        in_specs=[step(0), whole(1), whole(0), step(0),
                  step(0), step(1), step(1), step(1),
                  table(din), table(qd), table(kd), table(cd),
                  pl.BlockSpec((1, hb * d), lambda bi, g, i: (0, g))],
        out_specs=[step(0), step(0)],
        out_shape=[out_shape, out_shape],
        scratch_shapes=[pltpu.VMEM((hb, d, d), F32)],
        compiler_params=pltpu.CompilerParams(
            dimension_semantics=("parallel", "parallel", "arbitrary"),
            vmem_limit_bytes=VMEM_LIMIT),
        name="mixers",
    )(sb_qk, sb_qk, vals, gates, ret_qk, ret_qk, vals, gates, din, qd, kd, cd, gain)


def _ret_pieces(q_ref, k_ref, v_ref, g_ref, din_ref, qd_ref, kd_ref, cd_ref, gain_ref,
                o_ref, state_scr, heads):
    parts = q_ref.shape[2] // RET_CHUNK
    states = {}

    def piece(h, part):
        state = states[h] if part else state_scr[h]
        rows = slice(part * RET_CHUNK, (part + 1) * RET_CHUNK)
        qc = q_ref[0, h, rows, :]
        kc = k_ref[0, h, rows, :]
        vc = v_ref[0, h, rows, :]

        scores = lax.dot_general(qc, kc, (((1,), (1,)), ((), ())), preferred_element_type=F32)
        inner = jnp.dot((scores * din_ref[h]).astype(BF16), vc, preferred_element_type=F32)
        q_dec = (qc.astype(F32) * qd_ref[h]).astype(BF16)
        cross = jnp.dot(q_dec, state.astype(BF16), preferred_element_type=F32)
        k_dec = (kc.astype(F32) * kd_ref[h]).astype(BF16)
        kv = lax.dot_general(k_dec, vc, (((0,), (0,)), ((), ())), preferred_element_type=F32)
        states[h] = state * cd_ref[h] + kv
        if part == parts - 1:
            state_scr[h] = states[h]

        o = inner + cross
        mu = jnp.mean(o, axis=-1, keepdims=True)
        cen = o - mu
        var = jnp.mean(cen * cen, axis=-1, keepdims=True)
        y = cen * lax.rsqrt(var + EPS) * gain_ref[:, h * HEAD_DIM:(h + 1) * HEAD_DIM]
        o_ref[0, h, rows, :] = (y * g_ref[0, h, rows, :].astype(F32)).astype(BF16)

    return [functools.partial(piece, h, part) for part in range(parts) for h in range(heads)]


def _retention_tables(heads, c):
    log_gamma = np.log1p(-np.exp2(-5.0 - np.arange(heads, dtype=np.float64)))
    idx = np.arange(c, dtype=np.float64)
    rel = idx[:, None] - idx[None, :]
    din = np.where(rel[None] >= 0, np.exp(np.maximum(rel, 0.0)[None] * log_gamma[:, None, None]), 0.0)
    qd = np.exp((idx[None, :] + 1.0) * log_gamma[:, None])[..., None]
    kd = np.exp((c - 1.0 - idx[None, :]) * log_gamma[:, None])[..., None]
    cd = np.exp(c * log_gamma)[:, None, None]
    bc = lambda t: np.ascontiguousarray(
        np.broadcast_to(t, t.shape[:-1] + (HEAD_DIM,)), dtype=np.float32)
    return din.astype(np.float32), bc(qd), bc(kd), bc(cd)


def _out_kernel(a_ref, r_ref, w_ref, x_ref, mod_ref, o_ref, *, heads):
    parts = [a_ref[0, j] for j in range(heads)] + [r_ref[0, j] for j in range(heads)]
    mix = jnp.concatenate(parts, axis=-1)
    for c in range(w_ref.shape[1] // OUT_COLS):
        cols = slice(c * OUT_COLS, (c + 1) * OUT_COLS)
        y = jnp.dot(mix, w_ref[:, cols], preferred_element_type=F32)
        o_ref[0, :, cols] = x_ref[0, :, cols] + mod_ref[0, 2:3, cols] * y


def _out_proj(oa, ob, w_bf16, x, mod3):
    b, s, d = x.shape
    heads = oa.shape[1]
    tm = OUT_ROWS
    return pl.pallas_call(
        functools.partial(_out_kernel, heads=heads),
        grid=(b, s // tm),
        in_specs=[
            pl.BlockSpec((1, heads, tm, HEAD_DIM), lambda bi, si: (bi, 0, si, 0)),
            pl.BlockSpec((1, heads, tm, HEAD_DIM), lambda bi, si: (bi, 0, si, 0)),
            pl.BlockSpec(w_bf16.shape, lambda bi, si: (0, 0), pipeline_mode=pl.Buffered(1)),
            pl.BlockSpec((1, tm, d), lambda bi, si: (bi, si, 0)),
            pl.BlockSpec((1, 3, d), lambda bi, si: (bi, 0, 0)),
        ],
        out_specs=pl.BlockSpec((1, tm, d), lambda bi, si: (bi, si, 0)),
        out_shape=jax.ShapeDtypeStruct((b, s, d), F32),
        compiler_params=pltpu.CompilerParams(
            dimension_semantics=("parallel", "parallel"),
            vmem_limit_bytes=VMEM_LIMIT),
        name="out_proj",
    )(oa, ob, w_bf16, x, mod3)


def _rotary_tables(s):
    half = HEAD_DIM // 2
    inv_freq = ROPE_BASE ** (-np.arange(half, dtype=np.float64) / half)
    ang = np.arange(s, dtype=np.float64)[:, None] * inv_freq[None, :]
    cos, sin = np.cos(ang), np.sin(ang)
    return (np.concatenate([cos, cos], axis=-1).astype(np.float32),
            np.concatenate([-sin, sin], axis=-1).astype(np.float32))


def kernel(x, c, w_ada, b_ada, norm_gain, w_in, sb_q_gain, sb_k_gain, ret_norm_gain, w_out):
    b, s, d = x.shape
    depth = w_ada.shape[0]
    heads = w_in.shape[2] // (N_GROUPS * HEAD_DIM)
    cos_t, sin_t = _rotary_tables(s)
    din, qd, kd, cd = _retention_tables(heads, RET_CHUNK)
    c_pad = jnp.pad(c, ((0, ADALN_ROWS - b), (0, 0)))
    rope_spec = pl.BlockSpec((PROJ_ROWS, HEAD_DIM), lambda bi, si, g: (si, 0))
    for layer in range(depth):
        mod = _adaln(c_pad, w_ada[layer], b_ada[layer][None, :])
        mod3 = mod[:b].reshape(b, 3, d)
        w = w_in[layer]
        qk_gain = jnp.stack([sb_q_gain[layer], sb_k_gain[layer]])[:, None, :]
        h, sb_qk = _norm_qk(x, mod3, norm_gain[layer][None, :], w, qk_gain, heads,
                            q_mult=float(1.0 / np.sqrt(HEAD_DIM)))
        vals = _in_proj(h, w, "identity", 2, 4)
        gates = _in_proj(h, w, "silu", 3, 4)
        ret_qk = _in_proj(h, w, "rotary", 4, 1, (cos_t, sin_t), (rope_spec, rope_spec),
                          second_mult=float(HEAD_DIM ** -0.5))
        oa, ob = _mixers(sb_qk, ret_qk, vals, gates, heads, din, qd, kd, cd,
                         ret_norm_gain[layer][None, :])
        x = _out_proj(oa, ob, w_out[layer].astype(BF16), x, mod3)
    return x
```

```python
import functools

import jax
import jax.numpy as jnp
import numpy as np
from jax import lax
from jax.experimental import pallas as pl
from jax.experimental.pallas import tpu as pltpu

HEAD_DIM = 128
ROPE_BASE = 10000.0
EPS = 1e-6
N_GROUPS = 8
F32 = jnp.float32
BF16 = jnp.bfloat16

ADALN_ROWS = 16
ADALN_STEPS = 8
NORM_ROWS = 1024
NORM_UNIT_ROWS = 512
NORM_CHUNK = 32
PROJ_ROWS = 2048
PROJ_UNIT_ROWS = 1024
PROJ_COLS = 256
SB_Q = 128
SB_K = 256
SB_TAIL = 128
SB_QK_ROWS = 256
SB_STEP_ROWS = 2048
SB_HEADS_PER_STEP = 2
RET_CHUNK = 256
RET_PIECES_PER_GAP = 8
OUT_ROWS = 1024
OUT_COLS = 256
VMEM_LIMIT = 56 * 1024 * 1024
LOG_WEIGHT_FLOOR = -106.0
LOG2E = 1.4426950408889634


def _silu(v):
    half = 0.5 * v
    return half + half * jnp.tanh(half)


def _split_bf16(t):
    hi = t.astype(BF16)
    return hi, (t - hi.astype(F32)).astype(BF16)


def _adaln_kernel(c_ref, w_ref, b_ref, o_ref):
    s_hi, s_lo = _split_bf16(_silu(c_ref[...]))
    w_hi, w_lo = _split_bf16(w_ref[...])
    dot = functools.partial(jnp.dot, preferred_element_type=F32)
    o_ref[...] = dot(s_hi, w_hi) + (dot(s_hi, w_lo) + dot(s_lo, w_hi)) + b_ref[...]


def _adaln(c_pad, w, b):
    rows, d = c_pad.shape
    n = w.shape[1]
    tn = n // ADALN_STEPS
    return pl.pallas_call(
        _adaln_kernel,
        grid=(n // tn,),
        in_specs=[pl.BlockSpec((rows, d), lambda j: (0, 0)),
                  pl.BlockSpec((d, tn), lambda j: (0, j)),
                  pl.BlockSpec((1, tn), lambda j: (0, j))],
        out_specs=pl.BlockSpec((rows, tn), lambda j: (0, j)),
        out_shape=jax.ShapeDtypeStruct((rows, n), F32),
        compiler_params=pltpu.CompilerParams(
            dimension_semantics=("arbitrary",), vmem_limit_bytes=VMEM_LIMIT),
        name="adaln",
    )(c_pad, w, b)


def _norm_qk_kernel(x_ref, mod_ref, ng_ref, w_ref, gain_ref, h_ref, o_ref, *, heads, q_mult):
    shift = mod_ref[0, 0:1, :]
    gain_h = ng_ref[...] * (1.0 + mod_ref[0, 1:2, :])
    heads_per_chunk = PROJ_COLS // HEAD_DIM
    for r in range(x_ref.shape[1] // NORM_UNIT_ROWS):
        for ch in range(NORM_UNIT_ROWS // NORM_CHUNK):
            rows = slice(r * NORM_UNIT_ROWS + ch * NORM_CHUNK,
                         r * NORM_UNIT_ROWS + (ch + 1) * NORM_CHUNK)
            x = x_ref[0, rows, :]
            ms = jnp.mean(x * x, axis=-1, keepdims=True)
            h_ref[0, rows, :] = (x * lax.rsqrt(ms + EPS) * gain_h + shift).astype(BF16)
        rows = slice(r * NORM_UNIT_ROWS, (r + 1) * NORM_UNIT_ROWS)
        h = h_ref[0, rows, :]
        for c in range(2 * heads // heads_per_chunk):
            w_chunk = w_ref[:, c * PROJ_COLS:(c + 1) * PROJ_COLS].astype(BF16)
            acc = jnp.dot(h, w_chunk, preferred_element_type=F32)
            for j in range(heads_per_chunk):
                head = c * heads_per_chunk + j
                gain = gain_ref[head // heads] * (q_mult if head < heads else 1.0)
                a = acc[:, j * HEAD_DIM:(j + 1) * HEAD_DIM]
                ms = jnp.mean(a * a, axis=-1, keepdims=True)
                o_ref[0, head, rows, :] = (a * lax.rsqrt(ms + EPS) * gain).astype(BF16)


def _norm_qk(x, mod3, norm_gain, w, qk_gain, heads, q_mult):
    b, s, d = x.shape
    tm = NORM_ROWS
    cols = 2 * heads * HEAD_DIM
    return pl.pallas_call(
        functools.partial(_norm_qk_kernel, heads=heads, q_mult=q_mult),
        grid=(b, s // tm),
        in_specs=[pl.BlockSpec((1, tm, d), lambda bi, si: (bi, si, 0)),
                  pl.BlockSpec((1, 3, d), lambda bi, si: (bi, 0, 0)),
                  pl.BlockSpec((1, d), lambda bi, si: (0, 0)),
                  pl.BlockSpec((d, cols), lambda bi, si: (0, 0), pipeline_mode=pl.Buffered(1)),
                  pl.BlockSpec(qk_gain.shape, lambda bi, si: (0, 0, 0))],
        out_specs=[pl.BlockSpec((1, tm, d), lambda bi, si: (bi, si, 0)),
                   pl.BlockSpec((1, 2 * heads, tm, HEAD_DIM), lambda bi, si: (bi, 0, si, 0))],
        out_shape=[jax.ShapeDtypeStruct((b, s, d), BF16),
                   jax.ShapeDtypeStruct((b, 2 * heads, s, HEAD_DIM), BF16)],
        compiler_params=pltpu.CompilerParams(
            dimension_semantics=("parallel", "parallel"), vmem_limit_bytes=VMEM_LIMIT),
        name="norm_qk",
    )(x, mod3, norm_gain, w, qk_gain)


def _proj_kernel(h_ref, w_ref, *refs, kind, heads, first_mult, second_mult):
    o_ref = refs[-1]
    g = pl.program_id(2)
    mult = jnp.where(g == 0, first_mult, second_mult).astype(F32)
    if kind == "rotary":
        def epilogue(a, rows):
            return (a * (refs[0][rows, :] * mult)
                    + pltpu.roll(a, HEAD_DIM // 2, 1) * (refs[1][rows, :] * mult))
    elif kind == "silu":
        epilogue = lambda a, rows: _silu(a)
    else:
        epilogue = lambda a, rows: a

    heads_per_chunk = PROJ_COLS // HEAD_DIM
    for c in range(heads // heads_per_chunk):
        w_chunk = w_ref[:, c * PROJ_COLS:(c + 1) * PROJ_COLS].astype(BF16)
        for r in range(h_ref.shape[1] // PROJ_UNIT_ROWS):
            rows = slice(r * PROJ_UNIT_ROWS, (r + 1) * PROJ_UNIT_ROWS)
            acc = jnp.dot(h_ref[0, rows, :], w_chunk, preferred_element_type=F32)
            for j in range(heads_per_chunk):
                a = acc[:, j * HEAD_DIM:(j + 1) * HEAD_DIM]
                o_ref[0, c * heads_per_chunk + j, rows, :] = epilogue(a, rows).astype(BF16)


def _in_proj(h, w, kind, first_group, group_stride, extra=(), extra_specs=(),
             first_mult=1.0, second_mult=1.0):
    b, s, d = h.shape
    tn = w.shape[1] // N_GROUPS
    heads = tn // HEAD_DIM
    tm = PROJ_ROWS
    return pl.pallas_call(
        functools.partial(_proj_kernel, kind=kind, heads=heads,
                          first_mult=first_mult, second_mult=second_mult),
        grid=(b, s // tm, 2),
        in_specs=[
            pl.BlockSpec((1, tm, d), lambda bi, si, g: (bi, si, 0)),
            pl.BlockSpec((d, tn), lambda bi, si, g: (0, first_group + group_stride * g)),
            *extra_specs,
        ],
        out_specs=pl.BlockSpec((1, heads, tm, HEAD_DIM), lambda bi, si, g: (bi, g, si, 0)),
        out_shape=jax.ShapeDtypeStruct((b, 2 * heads, s, HEAD_DIM), BF16),
        compiler_params=pltpu.CompilerParams(
            dimension_semantics=("parallel", "parallel", "arbitrary"),
            vmem_limit_bytes=VMEM_LIMIT),
        name="in_proj_" + kind,
    )(h, w, *extra)


def _sb_scores(q, k_blk):
    z = lax.dot_general(q, k_blk, (((1,), (1,)), ((), ())), preferred_element_type=F32)
    sp = jnp.maximum(z, 0.0) + jnp.log(1.0 + jnp.exp2(jnp.abs(z) * (-LOG2E)))
    return sp, z - sp


def _log1p_exp_neg_abs(z):
    return jnp.log(1.0 + jnp.exp2(jnp.abs(z) * (-LOG2E))).astype(BF16)


def _sb_later(sp, neg_upper):
    return jnp.dot(sp.astype(BF16), neg_upper, preferred_element_type=F32)


def _sb_first_windows(q_ref, k_ref, v_ref, neg_upper, heads_per_step):
    win = 2 * SB_K
    z = jnp.concatenate(
        [lax.dot_general(q_ref[0, hh, :win, :], k_ref[0, hh, :win, :], (((1,), (1,)), ((), ())),
                         preferred_element_type=F32) for hh in range(heads_per_step)],
        axis=0)
    n = z.shape[0]
    l = _log1p_exp_neg_abs(z)
    z = z.astype(BF16)
    logb = jnp.minimum(z, 0.0) - l
    query = lax.broadcasted_iota(jnp.int32, (n, win), 0) & (win - 1)
    key = lax.broadcasted_iota(jnp.int32, (n, win), 1)
    past = jnp.where(key < query, 1.0, 0.0).astype(BF16)
    sp = (jnp.maximum(z, 0.0) + l) * past
    sp_p, sp_d = sp[:, :SB_K], sp[:, SB_K:]
    w_d = jnp.exp(logb[:, SB_K:] + _sb_later(sp_d, neg_upper).astype(BF16)) * past[:, SB_K:]
    carry = -jnp.sum(sp_d.astype(F32), axis=1, keepdims=True)
    w_p = jnp.exp(logb[:, :SB_K] + (_sb_later(sp_p, neg_upper) + carry).astype(BF16)) * past[:, :SB_K]
    carry = carry - jnp.sum(sp_p.astype(F32), axis=1, keepdims=True)
    w = jnp.concatenate([w_p, w_d], axis=1)
    out = {}
    for hh in range(heads_per_step):
        for sub in range(win // SB_Q):
            rows = slice(hh * win + sub * SB_Q, hh * win + (sub + 1) * SB_Q)
            acc = jnp.dot(w[rows], v_ref[0, hh, :win, :], preferred_element_type=F32)
            out[(hh, sub)] = (carry[rows], acc, 0)
    return out


def _sb_interior_windows(q_ref, k_ref, v_ref, step_start, neg_upper, heads_per_step, first_block,
                         filler):
    win = 2 * SB_K
    per_mm = SB_QK_ROWS // SB_Q
    span = win + (per_mm - 1) * SB_Q
    w_start = step_start - (win - SB_Q)
    zs, keys = [], []
    for hh in range(heads_per_step):
        for m in range(first_block, SB_STEP_ROWS // SB_QK_ROWS):
            k_start = pl.multiple_of(w_start + m * SB_QK_ROWS, SB_Q)
            z = lax.dot_general(q_ref[0, hh, m * SB_QK_ROWS:(m + 1) * SB_QK_ROWS, :],
                                k_ref[0, hh, pl.ds(k_start, span), :],
                                (((1,), (1,)), ((), ())), preferred_element_type=F32)
            zs += [z[u * SB_Q:(u + 1) * SB_Q, u * SB_Q:u * SB_Q + win] for u in range(per_mm)]
            keys += [(hh, m * per_mm + u) for u in range(per_mm)]
    filler()
    z = jnp.concatenate(zs, axis=0)
    n = z.shape[0]
    l = _log1p_exp_neg_abs(z)
    z = z.astype(BF16)
    sp = jnp.maximum(z, 0.0) + l
    logb = jnp.minimum(z, 0.0) - l
    filler()
    row = lax.broadcasted_iota(jnp.int32, (n, SB_Q), 0) & (SB_Q - 1)
    col = lax.broadcasted_iota(jnp.int32, (n, SB_Q), 1)
    past = jnp.where(col < row, 1.0, 0.0).astype(BF16)
    diag = lambda t: jnp.concatenate([t[:, :SB_K - SB_Q], t[:, SB_K - SB_Q:] * past], axis=1)
    sp_p, sp_d = sp[:, :SB_K], diag(sp[:, SB_K:])
    w_d = diag(jnp.exp(logb[:, SB_K:] + _sb_later(sp_d, neg_upper).astype(BF16)))
    filler()
    carry = -jnp.sum(sp_d.astype(F32), axis=1, keepdims=True)
    w_p = jnp.exp(logb[:, :SB_K] + (_sb_later(sp_p, neg_upper) + carry).astype(BF16))
    carry = carry - jnp.sum(sp_p.astype(F32), axis=1, keepdims=True)
    filler()
    w = jnp.concatenate([w_p, w_d], axis=1)
    out = {}
    for c, (hh, sub) in enumerate(keys):
        p_start = pl.multiple_of(w_start + sub * SB_Q, SB_Q)
        acc = jnp.dot(w[c * SB_Q:(c + 1) * SB_Q], v_ref[0, hh, pl.ds(p_start, win), :],
                      preferred_element_type=F32)
        out[(hh, sub)] = (carry[c * SB_Q:(c + 1) * SB_Q], acc, p_start)
    return out


def _mixer_kernel(q_ref, k_ref, v_ref, g_ref, rq_ref, rk_ref, rv_ref, rg_ref,
                  din_ref, qd_ref, kd_ref, cd_ref, gain_ref, o_ref, ro_ref, state_scr,
                  *, heads_per_step):
    i = pl.program_id(2)

    @pl.when(i == 0)
    def _():
        state_scr[...] = jnp.zeros_like(state_scr)

    urow = lax.broadcasted_iota(jnp.int32, (SB_K, SB_K), 0)
    ucol = lax.broadcasted_iota(jnp.int32, (SB_K, SB_K), 1)
    neg_upper = jnp.where(urow > ucol, -1.0, 0.0).astype(BF16)
    neg_upper_tail = neg_upper[:SB_TAIL, :SB_TAIL]
    subs = SB_STEP_ROWS // SB_Q

    def run(edge):
        step_start = i * SB_STEP_ROWS
        edge_blocks = -(-(2 * SB_K - SB_Q) // SB_QK_ROWS) if edge else 0
        pieces = _ret_pieces(rq_ref, rk_ref, rv_ref, rg_ref, din_ref, qd_ref, kd_ref, cd_ref,
                             gain_ref, ro_ref, state_scr, heads_per_step)

        def filler():
            for piece in [pieces.pop(0) for _ in range(min(RET_PIECES_PER_GAP, len(pieces)))]:
                piece()

        stacked = _sb_interior_windows(q_ref, k_ref, v_ref, step_start, neg_upper,
                                       heads_per_step, edge_blocks, filler)
        if edge:
            assert first_interior == 1 and edge_blocks * SB_QK_ROWS == 2 * SB_K
            stacked.update(_sb_first_windows(q_ref, k_ref, v_ref, neg_upper, heads_per_step))
        chains = []
        for hh in range(heads_per_step):
            for sub in range(subs):
                rows = slice(sub * SB_Q, (sub + 1) * SB_Q)
                q = q_ref[0, hh, rows, :]
                carry, acc, p_start = stacked[(hh, sub)]
                o_ref[0, hh, rows, :] = (acc * g_ref[0, hh, rows, :].astype(F32)).astype(BF16)
                live = (jnp.max(carry) > LOG_WEIGHT_FLOOR) & (p_start > 0)
                chains.append((hh, rows, q, carry, acc, p_start, live))

        while pieces:
            filler()

        def older_tiles(hh, rows, q, carry, acc, p_start, live):
            tiles_left = p_start // SB_TAIL

            def cond(state):
                jj, live, _, _ = state
                return (jj < tiles_left) & live

            def body(state):
                jj, _, carry, acc = state
                start = pl.multiple_of(p_start - (jj + 1) * SB_TAIL, SB_TAIL)
                sp, logb = _sb_scores(q, k_ref[0, hh, pl.ds(start, SB_TAIL), :])
                w = jnp.exp(logb + _sb_later(sp, neg_upper_tail) + carry)
                acc = acc + jnp.dot(w.astype(BF16), v_ref[0, hh, pl.ds(start, SB_TAIL), :],
                                    preferred_element_type=F32)
                carry = carry - jnp.sum(sp, axis=1, keepdims=True)
                return jj + 1, jnp.max(carry) > LOG_WEIGHT_FLOOR, carry, acc

            _, _, _, acc = lax.while_loop(cond, body, (jnp.int32(0), live, carry, acc))
            o_ref[0, hh, rows, :] = (acc * g_ref[0, hh, rows, :].astype(F32)).astype(BF16)

        any_live = functools.reduce(jnp.logical_or, [ch[-1] for ch in chains])

        @pl.when(any_live)
        def _():
            for chain in chains:
                pl.when(chain[-1])(functools.partial(older_tiles, *chain))

    first_interior = (SB_Q + SB_K + SB_STEP_ROWS - 1) // SB_STEP_ROWS
    pl.when(i >= first_interior)(lambda: run(False))
    pl.when(i < first_interior)(lambda: run(True))


def _mixers(sb_qk, ret_qk, vals, gates, heads, din, qd, kd, cd, gain):
    b, _, s, d = sb_qk.shape
    hb = SB_HEADS_PER_STEP
    groups = heads // hb
    t = SB_STEP_ROWS
    step = lambda off: pl.BlockSpec((1, hb, t, d), lambda bi, g, i: (bi, off * groups + g, i, 0))
    whole = lambda off: pl.BlockSpec((1, hb, s, d), lambda bi, g, i: (bi, off * groups + g, 0, 0))
    table = lambda arr: pl.BlockSpec((hb,) + arr.shape[1:], lambda bi, g, i: (g, 0, 0))
    out_shape = jax.ShapeDtypeStruct((b, heads, s, d), BF16)
    return pl.pallas_call(
        functools.partial(_mixer_kernel, heads_per_step=hb),
        grid=(b, groups, s // t),
        in_specs=[step(0), whole(1), whole(0), step(0),
                  step(0), step(1), step(1), step(1),
                  table(din), table(qd), table(kd), table(cd),
                  pl.BlockSpec((1, hb * d), lambda bi, g, i: (0, g))],
        out_specs=[step(0), step(0)],
        out_shape=[out_shape, out_shape],
        scratch_shapes=[pltpu.VMEM((hb, d, d), F32)],
        compiler_params=pltpu.CompilerParams(
            dimension_semantics=("parallel", "parallel", "arbitrary"),
            vmem_limit_bytes=VMEM_LIMIT),
        name="mixers",
    )(sb_qk, sb_qk, vals, gates, ret_qk, ret_qk, vals, gates, din, qd, kd, cd, gain)


def _ret_pieces(q_ref, k_ref, v_ref, g_ref, din_ref, qd_ref, kd_ref, cd_ref, gain_ref,
                o_ref, state_scr, heads):
    parts = q_ref.shape[2] // RET_CHUNK
    states = {}

    def piece(h, part):
        state = states[h] if part else state_scr[h]
        rows = slice(part * RET_CHUNK, (part + 1) * RET_CHUNK)
        qc = q_ref[0, h, rows, :]
        kc = k_ref[0, h, rows, :]
        vc = v_ref[0, h, rows, :]

        scores = lax.dot_general(qc, kc, (((1,), (1,)), ((), ())), preferred_element_type=F32)
        inner = jnp.dot((scores * din_ref[h]).astype(BF16), vc, preferred_element_type=F32)
        q_dec = (qc.astype(F32) * qd_ref[h]).astype(BF16)
        cross = jnp.dot(q_dec, state.astype(BF16), preferred_element_type=F32)
        k_dec = (kc.astype(F32) * kd_ref[h]).astype(BF16)
        kv = lax.dot_general(k_dec, vc, (((0,), (0,)), ((), ())), preferred_element_type=F32)
        states[h] = state * cd_ref[h] + kv
        if part == parts - 1:
            state_scr[h] = states[h]

        o = inner + cross
        mu = jnp.mean(o, axis=-1, keepdims=True)
        cen = o - mu
        var = jnp.mean(cen * cen, axis=-1, keepdims=True)
        y = cen * lax.rsqrt(var + EPS) * gain_ref[:, h * HEAD_DIM:(h + 1) * HEAD_DIM]
        o_ref[0, h, rows, :] = (y * g_ref[0, h, rows, :].astype(F32)).astype(BF16)

    return [functools.partial(piece, h, part) for part in range(parts) for h in range(heads)]


def _retention_tables(heads, c):
    log_gamma = np.log1p(-np.exp2(-5.0 - np.arange(heads, dtype=np.float64)))
    idx = np.arange(c, dtype=np.float64)
    rel = idx[:, None] - idx[None, :]
    din = np.where(rel[None] >= 0, np.exp(np.maximum(rel, 0.0)[None] * log_gamma[:, None, None]), 0.0)
    qd = np.exp((idx[None, :] + 1.0) * log_gamma[:, None])[..., None]
    kd = np.exp((c - 1.0 - idx[None, :]) * log_gamma[:, None])[..., None]
    cd = np.exp(c * log_gamma)[:, None, None]
    bc = lambda t: np.ascontiguousarray(
        np.broadcast_to(t, t.shape[:-1] + (HEAD_DIM,)), dtype=np.float32)
    return din.astype(np.float32), bc(qd), bc(kd), bc(cd)


def _out_kernel(a_ref, r_ref, w_ref, x_ref, mod_ref, o_ref, *, heads):
    parts = [a_ref[0, j] for j in range(heads)] + [r_ref[0, j] for j in range(heads)]
    mix = jnp.concatenate(parts, axis=-1)
    for c in range(w_ref.shape[1] // OUT_COLS):
        cols = slice(c * OUT_COLS, (c + 1) * OUT_COLS)
        y = jnp.dot(mix, w_ref[:, cols], preferred_element_type=F32)
        o_ref[0, :, cols] = x_ref[0, :, cols] + mod_ref[0, 2:3, cols] * y


def _out_proj(oa, ob, w_bf16, x, mod3):
    b, s, d = x.shape
    heads = oa.shape[1]
    tm = OUT_ROWS
    return pl.pallas_call(
        functools.partial(_out_kernel, heads=heads),
        grid=(b, s // tm),
        in_specs=[
            pl.BlockSpec((1, heads, tm, HEAD_DIM), lambda bi, si: (bi, 0, si, 0)),
            pl.BlockSpec((1, heads, tm, HEAD_DIM), lambda bi, si: (bi, 0, si, 0)),
            pl.BlockSpec(w_bf16.shape, lambda bi, si: (0, 0), pipeline_mode=pl.Buffered(1)),
            pl.BlockSpec((1, tm, d), lambda bi, si: (bi, si, 0)),
            pl.BlockSpec((1, 3, d), lambda bi, si: (bi, 0, 0)),
        ],
        out_specs=pl.BlockSpec((1, tm, d), lambda bi, si: (bi, si, 0)),
        out_shape=jax.ShapeDtypeStruct((b, s, d), F32),
        compiler_params=pltpu.CompilerParams(
            dimension_semantics=("parallel", "parallel"),
            vmem_limit_bytes=VMEM_LIMIT),
        name="out_proj",
    )(oa, ob, w_bf16, x, mod3)


def _rotary_tables(s):
    half = HEAD_DIM // 2
    inv_freq = ROPE_BASE ** (-np.arange(half, dtype=np.float64) / half)
    ang = np.arange(s, dtype=np.float64)[:, None] * inv_freq[None, :]
    cos, sin = np.cos(ang), np.sin(ang)
    return (np.concatenate([cos, cos], axis=-1).astype(np.float32),
            np.concatenate([-sin, sin], axis=-1).astype(np.float32))


def kernel(x, c, w_ada, b_ada, norm_gain, w_in, sb_q_gain, sb_k_gain, ret_norm_gain, w_out):
    b, s, d = x.shape
    depth = w_ada.shape[0]
    heads = w_in.shape[2] // (N_GROUPS * HEAD_DIM)
    cos_t, sin_t = _rotary_tables(s)
    din, qd, kd, cd = _retention_tables(heads, RET_CHUNK)
    c_pad = jnp.pad(c, ((0, ADALN_ROWS - b), (0, 0)))
    rope_spec = pl.BlockSpec((PROJ_ROWS, HEAD_DIM), lambda bi, si, g: (si, 0))
    for layer in range(depth):
        mod = _adaln(c_pad, w_ada[layer], b_ada[layer][None, :])
        mod3 = mod[:b].reshape(b, 3, d)
        w = w_in[layer]
        qk_gain = jnp.stack([sb_q_gain[layer], sb_k_gain[layer]])[:, None, :]
        h, sb_qk = _norm_qk(x, mod3, norm_gain[layer][None, :], w, qk_gain, heads,
                            q_mult=float(1.0 / np.sqrt(HEAD_DIM)))
        vals = _in_proj(h, w, "identity", 2, 4)
        gates = _in_proj(h, w, "silu", 3, 4)
        ret_qk = _in_proj(h, w, "rotary", 4, 1, (cos_t, sin_t), (rope_spec, rope_spec),
                          second_mult=float(HEAD_DIM ** -0.5))
        oa, ob = _mixers(sb_qk, ret_qk, vals, gates, heads, din, qd, kd, cd,
                         ret_norm_gain[layer][None, :])
        x = _out_proj(oa, ob, w_out[layer].astype(BF16), x, mod3)
    return x
```

```python
import functools

import jax
import jax.numpy as jnp
import numpy as np
from jax import lax
from jax.experimental import pallas as pl
from jax.experimental.pallas import tpu as pltpu

HEAD_DIM = 128
ROPE_BASE = 10000.0
EPS = 1e-6
N_GROUPS = 8
F32 = jnp.float32
BF16 = jnp.bfloat16

ADALN_ROWS = 16
ADALN_STEPS = 8
NORM_ROWS = 1024
NORM_UNIT_ROWS = 512
NORM_CHUNK = 32
PROJ_ROWS = 2048
PROJ_UNIT_ROWS = 1024
PROJ_COLS = 256
SB_Q = 128
SB_K = 256
SB_TAIL = 128
SB_QK_ROWS = 256
SB_STEP_ROWS = 2048
SB_HEADS_PER_STEP = 2
SB_STACK_HEADS = 1
RET_CHUNK = 256
RET_PIECES_PER_GAP = 2
OUT_ROWS = 1024
OUT_COLS = 256
VMEM_LIMIT = 56 * 1024 * 1024
LOG_WEIGHT_FLOOR = -106.0
LOG2E = 1.4426950408889634


def _silu(v):
    half = 0.5 * v
    return half + half * jnp.tanh(half)


def _split_bf16(t):
    hi = t.astype(BF16)
    return hi, (t - hi.astype(F32)).astype(BF16)


def _adaln_kernel(c_ref, w_ref, b_ref, o_ref):
    s_hi, s_lo = _split_bf16(_silu(c_ref[...]))
    w_hi, w_lo = _split_bf16(w_ref[...])
    dot = functools.partial(jnp.dot, preferred_element_type=F32)
    o_ref[...] = dot(s_hi, w_hi) + (dot(s_hi, w_lo) + dot(s_lo, w_hi)) + b_ref[...]


def _adaln(c_pad, w, b):
    rows, d = c_pad.shape
    n = w.shape[1]
    tn = n // ADALN_STEPS
    return pl.pallas_call(
        _adaln_kernel,
        grid=(n // tn,),
        in_specs=[pl.BlockSpec((rows, d), lambda j: (0, 0)),
                  pl.BlockSpec((d, tn), lambda j: (0, j)),
                  pl.BlockSpec((1, tn), lambda j: (0, j))],
        out_specs=pl.BlockSpec((rows, tn), lambda j: (0, j)),
        out_shape=jax.ShapeDtypeStruct((rows, n), F32),
        compiler_params=pltpu.CompilerParams(
            dimension_semantics=("arbitrary",), vmem_limit_bytes=VMEM_LIMIT),
        name="adaln",
    )(c_pad, w, b)


def _norm_qk_kernel(x_ref, mod_ref, ng_ref, w_ref, gain_ref, h_ref, o_ref, *, heads, q_mult):
    shift = mod_ref[0, 0:1, :]
    gain_h = ng_ref[...] * (1.0 + mod_ref[0, 1:2, :])
    heads_per_chunk = PROJ_COLS // HEAD_DIM
    for r in range(x_ref.shape[1] // NORM_UNIT_ROWS):
        for ch in range(NORM_UNIT_ROWS // NORM_CHUNK):
            rows = slice(r * NORM_UNIT_ROWS + ch * NORM_CHUNK,
                         r * NORM_UNIT_ROWS + (ch + 1) * NORM_CHUNK)
            x = x_ref[0, rows, :]
            ms = jnp.mean(x * x, axis=-1, keepdims=True)
            h_ref[0, rows, :] = (x * lax.rsqrt(ms + EPS) * gain_h + shift).astype(BF16)
        rows = slice(r * NORM_UNIT_ROWS, (r + 1) * NORM_UNIT_ROWS)
        h = h_ref[0, rows, :]
        for c in range(2 * heads // heads_per_chunk):
            w_chunk = w_ref[:, c * PROJ_COLS:(c + 1) * PROJ_COLS].astype(BF16)
            acc = jnp.dot(h, w_chunk, preferred_element_type=F32)
            for j in range(heads_per_chunk):
                head = c * heads_per_chunk + j
                gain = gain_ref[head // heads] * (q_mult if head < heads else 1.0)
                a = acc[:, j * HEAD_DIM:(j + 1) * HEAD_DIM]
                ms = jnp.mean(a * a, axis=-1, keepdims=True)
                o_ref[0, head, rows, :] = (a * lax.rsqrt(ms + EPS) * gain).astype(BF16)


def _norm_qk(x, mod3, norm_gain, w, qk_gain, heads, q_mult):
    b, s, d = x.shape
    tm = NORM_ROWS
    cols = 2 * heads * HEAD_DIM
    return pl.pallas_call(
        functools.partial(_norm_qk_kernel, heads=heads, q_mult=q_mult),
        grid=(b, s // tm),
        in_specs=[pl.BlockSpec((1, tm, d), lambda bi, si: (bi, si, 0)),
                  pl.BlockSpec((1, 3, d), lambda bi, si: (bi, 0, 0)),
                  pl.BlockSpec((1, d), lambda bi, si: (0, 0)),
                  pl.BlockSpec((d, cols), lambda bi, si: (0, 0), pipeline_mode=pl.Buffered(1)),
                  pl.BlockSpec(qk_gain.shape, lambda bi, si: (0, 0, 0))],
        out_specs=[pl.BlockSpec((1, tm, d), lambda bi, si: (bi, si, 0)),
                   pl.BlockSpec((1, 2 * heads, tm, HEAD_DIM), lambda bi, si: (bi, 0, si, 0))],
        out_shape=[jax.ShapeDtypeStruct((b, s, d), BF16),
                   jax.ShapeDtypeStruct((b, 2 * heads, s, HEAD_DIM), BF16)],
        compiler_params=pltpu.CompilerParams(
            dimension_semantics=("parallel", "parallel"), vmem_limit_bytes=VMEM_LIMIT),
        name="norm_qk",
    )(x, mod3, norm_gain, w, qk_gain)


def _proj_kernel(h_ref, w_ref, *refs, kind, heads, first_mult, second_mult):
    o_ref = refs[-1]
    g = pl.program_id(2)
    mult = jnp.where(g == 0, first_mult, second_mult).astype(F32)
    if kind == "rotary":
        def epilogue(a, rows):
            return (a * (refs[0][rows, :] * mult)
                    + pltpu.roll(a, HEAD_DIM // 2, 1) * (refs[1][rows, :] * mult))
    elif kind == "silu":
        epilogue = lambda a, rows: _silu(a)
    else:
        epilogue = lambda a, rows: a

    heads_per_chunk = PROJ_COLS // HEAD_DIM
    for c in range(heads // heads_per_chunk):
        w_chunk = w_ref[:, c * PROJ_COLS:(c + 1) * PROJ_COLS].astype(BF16)
        for r in range(h_ref.shape[1] // PROJ_UNIT_ROWS):
            rows = slice(r * PROJ_UNIT_ROWS, (r + 1) * PROJ_UNIT_ROWS)
            acc = jnp.dot(h_ref[0, rows, :], w_chunk, preferred_element_type=F32)
            for j in range(heads_per_chunk):
                a = acc[:, j * HEAD_DIM:(j + 1) * HEAD_DIM]
                o_ref[0, c * heads_per_chunk + j, rows, :] = epilogue(a, rows).astype(BF16)


def _in_proj(h, w, kind, first_group, group_stride, extra=(), extra_specs=(),
             first_mult=1.0, second_mult=1.0):
    b, s, d = h.shape
    tn = w.shape[1] // N_GROUPS
    heads = tn // HEAD_DIM
    tm = PROJ_ROWS
    return pl.pallas_call(
        functools.partial(_proj_kernel, kind=kind, heads=heads,
                          first_mult=first_mult, second_mult=second_mult),
        grid=(b, s // tm, 2),
        in_specs=[
            pl.BlockSpec((1, tm, d), lambda bi, si, g: (bi, si, 0)),
            pl.BlockSpec((d, tn), lambda bi, si, g: (0, first_group + group_stride * g)),
            *extra_specs,
        ],
        out_specs=pl.BlockSpec((1, heads, tm, HEAD_DIM), lambda bi, si, g: (bi, g, si, 0)),
        out_shape=jax.ShapeDtypeStruct((b, 2 * heads, s, HEAD_DIM), BF16),
        compiler_params=pltpu.CompilerParams(
            dimension_semantics=("parallel", "parallel", "arbitrary"),
            vmem_limit_bytes=VMEM_LIMIT),
        name="in_proj_" + kind,
    )(h, w, *extra)


def _sb_scores(q, k_blk):
    z = lax.dot_general(q, k_blk, (((1,), (1,)), ((), ())), preferred_element_type=F32)
    sp = jnp.maximum(z, 0.0) + jnp.log(1.0 + jnp.exp2(jnp.abs(z) * (-LOG2E)))
    return sp, z - sp


def _log1p_exp_neg_abs(z):
    return jnp.log(1.0 + jnp.exp2(jnp.abs(z) * (-LOG2E))).astype(BF16)


def _sb_later(sp, neg_upper):
    return jnp.dot(sp.astype(BF16), neg_upper, preferred_element_type=F32)


def _sb_first_windows(q_ref, k_ref, v_ref, neg_upper, heads_per_step):
    win = 2 * SB_K
    z = jnp.concatenate(
        [lax.dot_general(q_ref[0, hh, :win, :], k_ref[0, hh, :win, :], (((1,), (1,)), ((), ())),
                         preferred_element_type=F32) for hh in range(heads_per_step)],
        axis=0)
    n = z.shape[0]
    l = _log1p_exp_neg_abs(z)
    z = z.astype(BF16)
    logb = jnp.minimum(z, 0.0) - l
    query = lax.broadcasted_iota(jnp.int32, (n, win), 0) & (win - 1)
    key = lax.broadcasted_iota(jnp.int32, (n, win), 1)
    past = jnp.where(key < query, 1.0, 0.0).astype(BF16)
    sp = (jnp.maximum(z, 0.0) + l) * past
    sp_p, sp_d = sp[:, :SB_K], sp[:, SB_K:]
    w_d = jnp.exp(logb[:, SB_K:] + _sb_later(sp_d, neg_upper).astype(BF16)) * past[:, SB_K:]
    carry = -jnp.sum(sp_d.astype(F32), axis=1, keepdims=True)
    w_p = jnp.exp(logb[:, :SB_K] + (_sb_later(sp_p, neg_upper) + carry).astype(BF16)) * past[:, :SB_K]
    carry = carry - jnp.sum(sp_p.astype(F32), axis=1, keepdims=True)
    w = jnp.concatenate([w_p, w_d], axis=1)
    out = {}
    for hh in range(heads_per_step):
        for sub in range(win // SB_Q):
            rows = slice(hh * win + sub * SB_Q, hh * win + (sub + 1) * SB_Q)
            acc = jnp.dot(w[rows], v_ref[0, hh, :win, :], preferred_element_type=F32)
            out[(hh, sub)] = (carry[rows], acc, 0)
    return out


def _sb_interior_windows(q_ref, k_ref, v_ref, step_start, neg_upper, head_ids, first_block,
                         filler):
    win = 2 * SB_K
    per_mm = SB_QK_ROWS // SB_Q
    span = win + (per_mm - 1) * SB_Q
    w_start = step_start - (win - SB_Q)
    zs, keys = [], []
    for hh in head_ids:
        for m in range(first_block, SB_STEP_ROWS // SB_QK_ROWS):
            k_start = pl.multiple_of(w_start + m * SB_QK_ROWS, SB_Q)
            z = lax.dot_general(q_ref[0, hh, m * SB_QK_ROWS:(m + 1) * SB_QK_ROWS, :],
                                k_ref[0, hh, pl.ds(k_start, span), :],
                                (((1,), (1,)), ((), ())), preferred_element_type=F32)
            zs += [z[u * SB_Q:(u + 1) * SB_Q, u * SB_Q:u * SB_Q + win] for u in range(per_mm)]
            keys += [(hh, m * per_mm + u) for u in range(per_mm)]
    filler()
    z = jnp.concatenate(zs, axis=0)
    n = z.shape[0]
    l = _log1p_exp_neg_abs(z)
    z = z.astype(BF16)
    sp = jnp.maximum(z, 0.0) + l
    logb = jnp.minimum(z, 0.0) - l
    filler()
    row = lax.broadcasted_iota(jnp.int32, (n, SB_Q), 0) & (SB_Q - 1)
    col = lax.broadcasted_iota(jnp.int32, (n, SB_Q), 1)
    past = jnp.where(col < row, 1.0, 0.0).astype(BF16)
    diag = lambda t: jnp.concatenate([t[:, :SB_K - SB_Q], t[:, SB_K - SB_Q:] * past], axis=1)
    sp_p, sp_d = sp[:, :SB_K], diag(sp[:, SB_K:])
    w_d = diag(jnp.exp(logb[:, SB_K:] + _sb_later(sp_d, neg_upper).astype(BF16)))
    filler()
    carry = -jnp.sum(sp_d.astype(F32), axis=1, keepdims=True)
    w_p = jnp.exp(logb[:, :SB_K] + (_sb_later(sp_p, neg_upper) + carry).astype(BF16))
    carry = carry - jnp.sum(sp_p.astype(F32), axis=1, keepdims=True)
    filler()
    w = jnp.concatenate([w_p, w_d], axis=1)
    out = {}
    for c, (hh, sub) in enumerate(keys):
        p_start = pl.multiple_of(w_start + sub * SB_Q, SB_Q)
        acc = jnp.dot(w[c * SB_Q:(c + 1) * SB_Q], v_ref[0, hh, pl.ds(p_start, win), :],
                      preferred_element_type=F32)
        out[(hh, sub)] = (carry[c * SB_Q:(c + 1) * SB_Q], acc, p_start)
    return out


def _mixer_kernel(q_ref, k_ref, v_ref, g_ref, rq_ref, rk_ref, rv_ref, rg_ref,
                  din_ref, qd_ref, kd_ref, cd_ref, gain_ref, o_ref, ro_ref, state_scr,
                  *, heads_per_step):
    i = pl.program_id(2)

    @pl.when(i == 0)
    def _():
        state_scr[...] = jnp.zeros_like(state_scr)

    urow = lax.broadcasted_iota(jnp.int32, (SB_K, SB_K), 0)
    ucol = lax.broadcasted_iota(jnp.int32, (SB_K, SB_K), 1)
    neg_upper = jnp.where(urow > ucol, -1.0, 0.0).astype(BF16)
    neg_upper_tail = neg_upper[:SB_TAIL, :SB_TAIL]
    subs = SB_STEP_ROWS // SB_Q

    def run(edge):
        step_start = i * SB_STEP_ROWS
        edge_blocks = -(-(2 * SB_K - SB_Q) // SB_QK_ROWS) if edge else 0
        pieces = _ret_pieces(rq_ref, rk_ref, rv_ref, rg_ref, din_ref, qd_ref, kd_ref, cd_ref,
                             gain_ref, ro_ref, state_scr, heads_per_step)

        def filler():
            for piece in [pieces.pop(0) for _ in range(min(RET_PIECES_PER_GAP, len(pieces)))]:
                piece()

        stacked = {}
        for g0 in range(0, heads_per_step, SB_STACK_HEADS):
            stacked.update(_sb_interior_windows(
                q_ref, k_ref, v_ref, step_start, neg_upper,
                range(g0, g0 + SB_STACK_HEADS), edge_blocks, filler))
        if edge:
            assert first_interior == 1 and edge_blocks * SB_QK_ROWS == 2 * SB_K
            stacked.update(_sb_first_windows(q_ref, k_ref, v_ref, neg_upper, heads_per_step))
        chains = []
        for hh in range(heads_per_step):
            for sub in range(subs):
                rows = slice(sub * SB_Q, (sub + 1) * SB_Q)
                q = q_ref[0, hh, rows, :]
                carry, acc, p_start = stacked[(hh, sub)]
                o_ref[0, hh, rows, :] = (acc * g_ref[0, hh, rows, :].astype(F32)).astype(BF16)
                live = (jnp.max(carry) > LOG_WEIGHT_FLOOR) & (p_start > 0)
                chains.append((hh, rows, q, carry, acc, p_start, live))

        while pieces:
            filler()

        def older_tiles(hh, rows, q, carry, acc, p_start, live):
            tiles_left = p_start // SB_TAIL

            def cond(state):
                jj, live, _, _ = state
                return (jj < tiles_left) & live

            def body(state):
                jj, _, carry, acc = state
                start = pl.multiple_of(p_start - (jj + 1) * SB_TAIL, SB_TAIL)
                sp, logb = _sb_scores(q, k_ref[0, hh, pl.ds(start, SB_TAIL), :])
                w = jnp.exp(logb + _sb_later(sp, neg_upper_tail) + carry)
                acc = acc + jnp.dot(w.astype(BF16), v_ref[0, hh, pl.ds(start, SB_TAIL), :],
                                    preferred_element_type=F32)
                carry = carry - jnp.sum(sp, axis=1, keepdims=True)
                return jj + 1, jnp.max(carry) > LOG_WEIGHT_FLOOR, carry, acc

            _, _, _, acc = lax.while_loop(cond, body, (jnp.int32(0), live, carry, acc))
            o_ref[0, hh, rows, :] = (acc * g_ref[0, hh, rows, :].astype(F32)).astype(BF16)

        any_live = functools.reduce(jnp.logical_or, [ch[-1] for ch in chains])

        @pl.when(any_live)
        def _():
            for chain in chains:
                pl.when(chain[-1])(functools.partial(older_tiles, *chain))

    first_interior = (SB_Q + SB_K + SB_STEP_ROWS - 1) // SB_STEP_ROWS
    pl.when(i >= first_interior)(lambda: run(False))
    pl.when(i < first_interior)(lambda: run(True))


def _mixers(sb_qk, ret_qk, vals, gates, heads, din, qd, kd, cd, gain):
    b, _, s, d = sb_qk.shape
    hb = SB_HEADS_PER_STEP
    groups = heads // hb
    t = SB_STEP_ROWS
    step = lambda off: pl.BlockSpec((1, hb, t, d), lambda bi, g, i: (bi, off * groups + g, i, 0))
    whole = lambda off: pl.BlockSpec((1, hb, s, d), lambda bi, g, i: (bi, off * groups + g, 0, 0))
    table = lambda arr: pl.BlockSpec((hb,) + arr.shape[1:], lambda bi, g, i: (g, 0, 0))
    out_shape = jax.ShapeDtypeStruct((b, heads, s, d), BF16)
    return pl.pallas_call(
        functools.partial(_mixer_kernel, heads_per_step=hb),
        grid=(b, groups, s // t),
        in_specs=[step(0), whole(1), whole(0), step(0),
                  step(0), step(1), step(1), step(1),
                  table(din), table(qd), table(kd), table(cd),
                  pl.BlockSpec((1, hb * d), lambda bi, g, i: (0, g))],
        out_specs=[step(0), step(0)],
        out_shape=[out_shape, out_shape],
        scratch_shapes=[pltpu.VMEM((hb, d, d), F32)],
        compiler_params=pltpu.CompilerParams(
            dimension_semantics=("parallel", "parallel", "arbitrary"),
            vmem_limit_bytes=VMEM_LIMIT),
        name="mixers",
    )(sb_qk, sb_qk, vals, gates, ret_qk, ret_qk, vals, gates, din, qd, kd, cd, gain)


def _ret_pieces(q_ref, k_ref, v_ref, g_ref, din_ref, qd_ref, kd_ref, cd_ref, gain_ref,
                o_ref, state_scr, heads):
    parts = q_ref.shape[2] // RET_CHUNK
    states = {}

    def piece(h, part):
        state = states[h] if part else state_scr[h]
        rows = slice(part * RET_CHUNK, (part + 1) * RET_CHUNK)
        qc = q_ref[0, h, rows, :]
        kc = k_ref[0, h, rows, :]
        vc = v_ref[0, h, rows, :]

        scores = lax.dot_general(qc, kc, (((1,), (1,)), ((), ())), preferred_element_type=F32)
        inner = jnp.dot((scores * din_ref[h]).astype(BF16), vc, preferred_element_type=F32)
        q_dec = (qc.astype(F32) * qd_ref[h]).astype(BF16)
        cross = jnp.dot(q_dec, state.astype(BF16), preferred_element_type=F32)
        k_dec = (kc.astype(F32) * kd_ref[h]).astype(BF16)
        kv = lax.dot_general(k_dec, vc, (((0,), (0,)), ((), ())), preferred_element_type=F32)
        states[h] = state * cd_ref[h] + kv
        if part == parts - 1:
            state_scr[h] = states[h]

        o = inner + cross
        mu = jnp.mean(o, axis=-1, keepdims=True)
        cen = o - mu
        var = jnp.mean(cen * cen, axis=-1, keepdims=True)
        y = cen * lax.rsqrt(var + EPS) * gain_ref[:, h * HEAD_DIM:(h + 1) * HEAD_DIM]
        o_ref[0, h, rows, :] = (y * g_ref[0, h, rows, :].astype(F32)).astype(BF16)

    return [functools.partial(piece, h, part) for part in range(parts) for h in range(heads)]


def _retention_tables(heads, c):
    log_gamma = np.log1p(-np.exp2(-5.0 - np.arange(heads, dtype=np.float64)))
    idx = np.arange(c, dtype=np.float64)
    rel = idx[:, None] - idx[None, :]
    din = np.where(rel[None] >= 0, np.exp(np.maximum(rel, 0.0)[None] * log_gamma[:, None, None]), 0.0)
    qd = np.exp((idx[None, :] + 1.0) * log_gamma[:, None])[..., None]
    kd = np.exp((c - 1.0 - idx[None, :]) * log_gamma[:, None])[..., None]
    cd = np.exp(c * log_gamma)[:, None, None]
    bc = lambda t: np.ascontiguousarray(
        np.broadcast_to(t, t.shape[:-1] + (HEAD_DIM,)), dtype=np.float32)
    return din.astype(np.float32), bc(qd), bc(kd), bc(cd)


def _out_kernel(a_ref, r_ref, w_ref, x_ref, mod_ref, o_ref, *, heads):
    parts = [a_ref[0, j] for j in range(heads)] + [r_ref[0, j] for j in range(heads)]
    mix = jnp.concatenate(parts, axis=-1)
    for c in range(w_ref.shape[1] // OUT_COLS):
        cols = slice(c * OUT_COLS, (c + 1) * OUT_COLS)
        y = jnp.dot(mix, w_ref[:, cols], preferred_element_type=F32)
        o_ref[0, :, cols] = x_ref[0, :, cols] + mod_ref[0, 2:3, cols] * y


def _out_proj(oa, ob, w_bf16, x, mod3):
    b, s, d = x.shape
    heads = oa.shape[1]
    tm = OUT_ROWS
    return pl.pallas_call(
        functools.partial(_out_kernel, heads=heads),
        grid=(b, s // tm),
        in_specs=[
            pl.BlockSpec((1, heads, tm, HEAD_DIM), lambda bi, si: (bi, 0, si, 0)),
            pl.BlockSpec((1, heads, tm, HEAD_DIM), lambda bi, si: (bi, 0, si, 0)),
            pl.BlockSpec(w_bf16.shape, lambda bi, si: (0, 0), pipeline_mode=pl.Buffered(1)),
            pl.BlockSpec((1, tm, d), lambda bi, si: (bi, si, 0)),
            pl.BlockSpec((1, 3, d), lambda bi, si: (bi, 0, 0)),
        ],
        out_specs=pl.BlockSpec((1, tm, d), lambda bi, si: (bi, si, 0)),
        out_shape=jax.ShapeDtypeStruct((b, s, d), F32),
        compiler_params=pltpu.CompilerParams(
            dimension_semantics=("parallel", "parallel"),
            vmem_limit_bytes=VMEM_LIMIT),
        name="out_proj",
    )(oa, ob, w_bf16, x, mod3)


def _rotary_tables(s):
    half = HEAD_DIM // 2
    inv_freq = ROPE_BASE ** (-np.arange(half, dtype=np.float64) / half)
    ang = np.arange(s, dtype=np.float64)[:, None] * inv_freq[None, :]
    cos, sin = np.cos(ang), np.sin(ang)
    return (np.concatenate([cos, cos], axis=-1).astype(np.float32),
            np.concatenate([-sin, sin], axis=-1).astype(np.float32))


def kernel(x, c, w_ada, b_ada, norm_gain, w_in, sb_q_gain, sb_k_gain, ret_norm_gain, w_out):
    b, s, d = x.shape
    depth = w_ada.shape[0]
    heads = w_in.shape[2] // (N_GROUPS * HEAD_DIM)
    cos_t, sin_t = _rotary_tables(s)
    din, qd, kd, cd = _retention_tables(heads, RET_CHUNK)
    c_pad = jnp.pad(c, ((0, ADALN_ROWS - b), (0, 0)))
    rope_spec = pl.BlockSpec((PROJ_ROWS, HEAD_DIM), lambda bi, si, g: (si, 0))
    for layer in range(depth):
        mod = _adaln(c_pad, w_ada[layer], b_ada[layer][None, :])
        mod3 = mod[:b].reshape(b, 3, d)
        w = w_in[layer]
        qk_gain = jnp.stack([sb_q_gain[layer], sb_k_gain[layer]])[:, None, :]
        h, sb_qk = _norm_qk(x, mod3, norm_gain[layer][None, :], w, qk_gain, heads,
                            q_mult=float(1.0 / np.sqrt(HEAD_DIM)))
        vals = _in_proj(h, w, "identity", 2, 4)
        gates = _in_proj(h, w, "silu", 3, 4)
        ret_qk = _in_proj(h, w, "rotary", 4, 1, (cos_t, sin_t), (rope_spec, rope_spec),
                          second_mult=float(HEAD_DIM ** -0.5))
        oa, ob = _mixers(sb_qk, ret_qk, vals, gates, heads, din, qd, kd, cd,
                         ret_norm_gain[layer][None, :])
        x = _out_proj(oa, ob, w_out[layer].astype(BF16), x, mod3)
    return x
```
